```python
import math
import jax
import jax.numpy as jnp
from jax import lax
import numpy as np


D_MODEL = 1024
BATCH = 8
SEQ = 2048
DEPTH = 2

GRID_W = 64
CTX_LEN = 256

NA_HEADS = 8
NA_HEAD_DIM = 64
NA_W = NA_HEADS * NA_HEAD_DIM
NA_KH = 8
NA_KW = 16

HY_W = 256
HY_ORDER = 2
HY_SHORT = 3
HY_BANDS = 8
HY_EMB = 1 + 2 * HY_BANDS
HY_FFN = 64
HY_FAST_DECAY = 0.3
HY_SLOW_DECAY = 1.5
HY_TARGET = 1e-2

RET_HEADS = 4
RET_HEAD_DIM = 64
RET_W = RET_HEADS * RET_HEAD_DIM
RET_CHUNK = 128

MIX_W = NA_W + HY_W + RET_W
IN_W = 4 * NA_W + 4 * HY_W + 4 * RET_W
ROPE_BASE = 10000.0
EPS = 1e-6
NEG_INF = -1e30
F32 = jnp.float32

kernel_name = 'hybrid_na_hyena_retention_dit_block'


def rms_norm(x, eps=EPS):
    xf = x.astype(F32)
    return (xf * lax.rsqrt(jnp.mean(xf * xf, axis=-1, keepdims=True) + eps)).astype(x.dtype)


def ada_modulation(cond, ada_w, ada_b):
    return jax.nn.silu(cond) @ ada_w + ada_b


def modulate(h, mod, norm_w):
    shift, scale, gate = jnp.split(mod, 3, axis=-1)
    return rms_norm(h) * norm_w * (1 + scale) + shift, gate


def split_projection(u):
    sizes = (3 * NA_W, NA_W, 3 * HY_W, HY_W, 3 * RET_W, RET_W)
    cuts = [int(s) for s in np.cumsum(sizes)[:-1]]
    return jnp.split(u, cuts, axis=-1)


def heads(u, n_heads, head_dim):
    return u.reshape(u.shape[0], u.shape[1], n_heads, head_dim)


def na_queries(u_na, q_gain):
    q = heads(u_na[..., :NA_W], NA_HEADS, NA_HEAD_DIM)
    return rms_norm(q) * q_gain * (NA_HEAD_DIM ** -0.5)


def na_keys_values(u_na, k_gain):
    k = heads(u_na[..., NA_W:2 * NA_W], NA_HEADS, NA_HEAD_DIM)
    v = heads(u_na[..., 2 * NA_W:], NA_HEADS, NA_HEAD_DIM)
    return rms_norm(k) * k_gain, v


def neighborhood_attention(q, k, v, k_ctx, v_ctx, rpb):
    B, L, H, Dh = q.shape
    rows = L // GRID_W
    kh = min(NA_KH, rows)
    r = jnp.arange(rows)
    row_idx = jnp.clip(r - kh // 2, 0, rows - kh)[:, None] + jnp.arange(kh)[None, :]
    cq = jnp.arange(GRID_W)
    ck = jnp.arange(GRID_W)
    col_start = jnp.clip(cq - NA_KW // 2, 0, GRID_W - NA_KW)
    col_ok = (ck[None, :] >= col_start[:, None]) & (ck[None, :] < col_start[:, None] + NA_KW)
    dr = row_idx - r[:, None] + (NA_KH - 1)
    dc = jnp.clip(ck[None, :] - cq[:, None] + (NA_KW - 1), 0, 2 * NA_KW - 2)
    bias = rpb[:, dr[:, None, :, None], dc[None, :, None, :]].astype(F32)

    qg = q.reshape(B, rows, GRID_W, H, Dh)
    kg = k.reshape(B, rows, GRID_W, H, Dh)[:, row_idx]
    vg = v.reshape(B, rows, GRID_W, H, Dh)[:, row_idx]
    s_loc = jnp.einsum('brqhd,brjkhd->bhrqjk', qg, kg).astype(F32) + bias[None]
    s_loc = jnp.where(col_ok[:, None, :], s_loc, NEG_INF)
    s_ctx = jnp.einsum('brqhd,bchd->bhrqc', qg, k_ctx).astype(F32)
    n_loc = kh * GRID_W
    s = jnp.concatenate([s_loc.reshape(B, H, rows, GRID_W, n_loc), s_ctx], axis=-1)
    p = jax.nn.softmax(s, axis=-1).astype(v.dtype)
    p_loc = p[..., :n_loc].reshape(B, H, rows, GRID_W, kh, GRID_W)
    p_ctx = p[..., n_loc:]
    o = jnp.einsum('bhrqjk,brjkhd->brqhd', p_loc, vg) + jnp.einsum('bhrqc,bchd->brqhd', p_ctx, v_ctx)
    return o.reshape(B, L, H * Dh)


def context_attention(q, k, v):
    B, Lc, H, Dh = q.shape
    s = jnp.einsum('bqhd,bkhd->bhqk', q, k).astype(F32)
    p = jax.nn.softmax(s, axis=-1).astype(v.dtype)
    return jnp.einsum('bhqk,bkhd->bqhd', p, v).reshape(B, Lc, H * Dh)


def hyena_filter_spectrum(L, w1, b1, w2, b2, w3, sin_freq):
    t = jnp.linspace(0.0, 1.0, L, dtype=F32)[:, None]
    omega = 2.0 * math.pi * jnp.arange(L, dtype=F32)[:, None] / L
    bands = jnp.linspace(1e-4, HY_BANDS - 1, HY_BANDS, dtype=F32)[None, :]
    z = jnp.concatenate([t, jnp.cos(bands * omega), -jnp.sin(bands * omega)], axis=-1)
    h = jnp.sin(sin_freq[0] * (z @ w1 + b1))
    h = jnp.sin(sin_freq[1] * (h @ w2 + b2))
    h = (h @ w3).astype(F32).reshape(L, HY_ORDER, 2, HY_W)
    deltas = jnp.abs(jnp.linspace(math.log(HY_TARGET) / HY_SLOW_DECAY,
                                  math.log(HY_TARGET) / HY_FAST_DECAY, HY_W, dtype=F32))
    h = h * jnp.exp(-t * deltas)[:, None, None, :]
    h_fwd = h[:, :, 0]
    h_bwd = h[1:, :, 1][::-1]
    filt = jnp.concatenate([h_fwd, jnp.zeros((1, HY_ORDER, HY_W), F32), h_bwd], axis=0)
    filt = filt / jnp.sum(jnp.abs(filt), axis=0, keepdims=True)
    return jnp.fft.rfft(filt, axis=0)


def fft_long_conv(u, filt_f, skip):
    L = u.shape[1]
    uf = u.astype(F32)
    y = jnp.fft.irfft(jnp.fft.rfft(uf, n=2 * L, axis=1) * filt_f[None], n=2 * L, axis=1)[:, :L]
    return (y + uf * skip).astype(u.dtype)


def short_conv(u, w, b):
    L = u.shape[1]
    pad = HY_SHORT // 2
    up = jnp.pad(u, ((0, 0), (pad, pad), (0, 0)))
    return sum(up[:, j:j + L] * w[j] for j in range(HY_SHORT)) + b


def hyena_mixer(u, conv_w, conv_b, filt_f, skip):
    z = short_conv(u, conv_w, conv_b)
    v, x1, x2 = jnp.split(z, 3, axis=-1)
    y = x1 * fft_long_conv(v, filt_f[:, 0], skip[0])
    return x2 * fft_long_conv(y, filt_f[:, 1], skip[1])


def retention_heads(u):
    q, k, v = [heads(a, RET_HEADS, RET_HEAD_DIM).astype(F32) for a in jnp.split(u, 3, axis=-1)]
    return q, k * (RET_HEAD_DIM ** -0.5), v


def axial_rope(x):
    L, D = x.shape[1], x.shape[-1]
    half = D // 2
    quarter = half // 2
    t = jnp.arange(L)
    pos = jnp.stack([t // GRID_W, t % GRID_W], axis=0).astype(F32)
    freqs = ROPE_BASE ** (-jnp.arange(quarter, dtype=F32) / quarter)
    ang = pos[:, :, None] * freqs
    outs = []
    for a in range(2):
        xa = x[..., a * half:(a + 1) * half]
        x1, x2 = xa[..., :quarter], xa[..., quarter:]
        cos = jnp.cos(ang[a])[None, :, None, :]
        sin = jnp.sin(ang[a])[None, :, None, :]
        outs += [x1 * cos - x2 * sin, x1 * sin + x2 * cos]
    return jnp.concatenate(outs, axis=-1).astype(x.dtype)


def retention_chunks(q, k, v, log_gamma, state0):
    B, L, H, Dk = q.shape
    Dv = v.shape[-1]
    n = L // RET_CHUNK
    qc = q.reshape(B, n, RET_CHUNK, H, Dk)
    kc = k.reshape(B, n, RET_CHUNK, H, Dk)
    vc = v.reshape(B, n, RET_CHUNK, H, Dv)
    pos = jnp.arange(RET_CHUNK, dtype=F32)
    diff = pos[:, None] - pos[None, :]
    decay_in = jnp.where(diff >= 0, jnp.exp(log_gamma[:, None, None] * jnp.maximum(diff, 0.0)), 0.0)
    scores = jnp.einsum('bnjhd,bnlhd->bnhjl', qc, kc) * decay_in
    o = jnp.einsum('bnhjl,bnlhe->bnjhe', scores, vc)
    zeta = jnp.exp(log_gamma[:, None] * (RET_CHUNK - 1 - pos))
    chunk_kv = jnp.einsum('bnlhd,hl,bnlhe->nbhde', kc, zeta, vc)
    chunk_decay = jnp.exp(log_gamma * RET_CHUNK)[None, :, None, None]

    def step(state, kv):
        return chunk_decay * state + kv, state

    _, prev = lax.scan(step, state0, chunk_kv)
    xi = jnp.exp(log_gamma[:, None] * (pos + 1.0))
    o = o + jnp.einsum('bnjhd,nbhde,hj->bnjhe', qc, prev, xi)
    return o.reshape(B, L, H, Dv)


def bidir_retention(q, k, v, log_gamma, state_fwd, state_bwd):
    flip = lambda a: a[:, ::-1]
    o_fwd = retention_chunks(q, k, v, log_gamma[0], state_fwd)
    o_bwd = retention_chunks(flip(q), flip(k), flip(v), log_gamma[1], state_bwd)
    return o_fwd + flip(o_bwd)


def context_final_states(k, v, log_gamma):
    Lc = k.shape[1]
    pos = jnp.arange(Lc, dtype=F32)
    w_fwd = jnp.exp(log_gamma[0][:, None] * (Lc - 1 - pos))
    w_bwd = jnp.exp(log_gamma[1][:, None] * pos)
    s_fwd = jnp.einsum('bmhd,hm,bmhe->bhde', k, w_fwd, v)
    s_bwd = jnp.einsum('bmhd,hm,bmhe->bhde', k, w_bwd, v)
    return s_fwd, s_bwd


def merge_branches(a, g_a, y, g_y, r, g_r, w_out):
    B, L, _ = a.shape
    r = rms_norm(r).reshape(B, L, RET_W).astype(a.dtype)
    z = jnp.concatenate([a * jax.nn.silu(g_a), y * jax.nn.silu(g_y), r * jax.nn.silu(g_r)], axis=-1)
    return z @ w_out


def hybrid_layer(x, ctx, mod_x, mod_c, norm_w, w_in, w_out, q_gain, k_gain, rpb,
                 conv_w, conv_b, hw1, hb1, hw2, hb2, hw3, sin_freq, skip, ret_log_rate,
                 with_ctx_out):
    B, L, _ = x.shape
    Lc = ctx.shape[1]
    hx, gate_x = modulate(x, mod_x[:, None, :], norm_w)
    hc, gate_c = modulate(ctx, mod_c, norm_w)
    na_x, ga_x, hy_x, gh_x, re_x, gr_x = split_projection(hx @ w_in)
    na_c, ga_c, hy_c, gh_c, re_c, gr_c = split_projection(hc @ w_in)
    log_gamma = -jnp.exp(ret_log_rate.astype(F32))

    k_c, v_c = na_keys_values(na_c, k_gain)
    k_x, v_x = na_keys_values(na_x, k_gain)
    a_x = neighborhood_attention(na_queries(na_x, q_gain), k_x, v_x, k_c, v_c, rpb)
    y_x = hyena_mixer(hy_x, conv_w, conv_b,
                      hyena_filter_spectrum(L, hw1, hb1, hw2, hb2, hw3, sin_freq), skip)
    rq_c, rk_c, rv_c = retention_heads(re_c)
    s_fwd, s_bwd = context_final_states(rk_c, rv_c, log_gamma)
    rq_x, rk_x, rv_x = retention_heads(re_x)
    r_x = bidir_retention(axial_rope(rq_x), axial_rope(rk_x), rv_x, log_gamma, s_fwd, s_bwd)

    x_new = x + gate_x * merge_branches(a_x, ga_x, y_x, gh_x, r_x, gr_x, w_out)
    if not with_ctx_out:
        return x_new, None

    a_c = context_attention(na_queries(na_c, q_gain), k_c, v_c)
    y_c = hyena_mixer(hy_c, conv_w, conv_b,
                      hyena_filter_spectrum(Lc, hw1, hb1, hw2, hb2, hw3, sin_freq), skip)
    zero_state = jnp.zeros((B, RET_HEADS, RET_HEAD_DIM, RET_HEAD_DIM), F32)
    r_c = bidir_retention(rq_c, rk_c, rv_c, log_gamma, zero_state, zero_state)
    ctx_new = ctx + gate_c * merge_branches(a_c, ga_c, y_c, gh_c, r_c, gr_c, w_out)
    return x_new, ctx_new


def setup_inputs(seed: int = 0) -> dict:
    key = jax.random.key(seed)
    ks = jax.random.split(key, 24)
    nrm = lambda k, shape, s: jax.random.normal(k, shape, F32) * s
    base_rate = jnp.log(-jnp.log1p(-(2.0 ** (-5.0 - jnp.arange(RET_HEADS, dtype=F32)))))
    return {
        'x': nrm(ks[0], (BATCH, SEQ, D_MODEL), 1.0),
        'c': nrm(ks[1], (BATCH, D_MODEL), 1.0),
        'ctx': nrm(ks[2], (BATCH, CTX_LEN, D_MODEL), 1.0),
        'c_ctx': nrm(ks[3], (D_MODEL,), 1.0),
        'norm_w': 1.0 + nrm(ks[4], (DEPTH, D_MODEL), 0.02),
        'ada_w': nrm(ks[5], (DEPTH, D_MODEL, 3 * D_MODEL), 0.5 * D_MODEL ** -0.5),
        'ada_b': nrm(ks[6], (DEPTH, 3 * D_MODEL), 0.02),
        'w_in': nrm(ks[7], (DEPTH, D_MODEL, IN_W), D_MODEL ** -0.5),
        'w_out': nrm(ks[8], (DEPTH, MIX_W, D_MODEL), MIX_W ** -0.5),
        'na_q_gain': 1.0 + nrm(ks[9], (DEPTH, NA_HEAD_DIM), 0.02),
        'na_k_gain': 1.0 + nrm(ks[10], (DEPTH, NA_HEAD_DIM), 0.02),
        'na_rpb': nrm(ks[11], (DEPTH, NA_HEADS, 2 * NA_KH - 1, 2 * NA_KW - 1), 0.05),
        'hy_conv_w': nrm(ks[12], (DEPTH, HY_SHORT, 3 * HY_W), HY_SHORT ** -0.5),
        'hy_conv_b': nrm(ks[13], (DEPTH, 3 * HY_W), 0.02),
        'hy_w1': nrm(ks[14], (DEPTH, HY_EMB, HY_FFN), HY_EMB ** -0.5),
        'hy_b1': nrm(ks[15], (DEPTH, HY_FFN), 0.02),
        'hy_w2': nrm(ks[16], (DEPTH, HY_FFN, HY_FFN), HY_FFN ** -0.5),
        'hy_b2': nrm(ks[17], (DEPTH, HY_FFN), 0.02),
        'hy_w3': nrm(ks[18], (DEPTH, HY_FFN, HY_ORDER * 2 * HY_W), HY_FFN ** -0.5),
        'hy_sin_freq': 1.0 + nrm(ks[19], (DEPTH, 2, HY_FFN), 0.05),
        'hy_skip': nrm(ks[20], (DEPTH, HY_ORDER, HY_W), 1.0),
        'ret_log_rate': base_rate[None, None, :] + nrm(ks[21], (DEPTH, 2, RET_HEADS), 0.05),
    }


def reference(x, c, ctx, c_ctx, norm_w, ada_w, ada_b, w_in, w_out, na_q_gain, na_k_gain, na_rpb,
              hy_conv_w, hy_conv_b, hy_w1, hy_b1, hy_w2, hy_b2, hy_w3, hy_sin_freq, hy_skip,
              ret_log_rate):
    for i in range(DEPTH):
        mod_x = ada_modulation(c, ada_w[i], ada_b[i])
        mod_c = ada_modulation(c_ctx, ada_w[i], ada_b[i])
        x, ctx = hybrid_layer(x, ctx, mod_x, mod_c, norm_w[i], w_in[i], w_out[i],
                              na_q_gain[i], na_k_gain[i], na_rpb[i],
                              hy_conv_w[i], hy_conv_b[i], hy_w1[i], hy_b1[i], hy_w2[i], hy_b2[i],
                              hy_w3[i], hy_sin_freq[i], hy_skip[i], ret_log_rate[i],
                              i < DEPTH - 1)
    return x
```

```python
import functools
import math

import numpy as np
import jax
import jax.numpy as jnp
from jax import lax
from jax.experimental import pallas as pl
from jax.experimental.pallas import tpu as pltpu

F32 = jnp.float32
BF16 = jnp.bfloat16

D_MODEL = 1024
DEPTH = 2
GRID_W = 64
NA_HEADS = 8
NA_HEAD_DIM = 64
NA_W = NA_HEADS * NA_HEAD_DIM
NA_KH = 8
NA_KW = 16
HY_W = 256
HY_BANDS = 8
HY_EMB = 1 + 2 * HY_BANDS
HY_EMB_PAD = 32
HY_FFN = 64
HY_FAST_DECAY = 0.3
HY_SLOW_DECAY = 1.5
HY_TARGET = 1e-2
RET_HEADS = 4
RET_HEAD_DIM = 64
RET_W = RET_HEADS * RET_HEAD_DIM
ROPE_BASE = 10000.0
EPS = 1e-6
NEG_INF = -1e30
IN_W = 4 * NA_W + 4 * HY_W + 4 * RET_W
MIX_W = NA_W + HY_W + RET_W

CB = 256
CB_HY_V, CB_HY_X1, CB_HY_X2, CB_HY_G = 8, 9, 10, 11
CB_RE_Q, CB_RE_K, CB_RE_V, CB_RE_G = 12, 13, 14, 15

NA_G = 4
NA_WIN = NA_G + NA_KH
NA_DR_MIN = -4
NA_DR_NUM = 23
RET_CHUNK = 256
HY_ROWS = 512
VMEM_LIMIT = 56 * 1024 * 1024


def _cparams(*sem):
    return pltpu.CompilerParams(dimension_semantics=sem, vmem_limit_bytes=VMEM_LIMIT)


def _resident(shape, index_map):
    return pl.BlockSpec(shape, index_map, pipeline_mode=pl.Buffered(1))


def _silu(x):
    return x * (1.0 / (1.0 + jnp.exp(-x)))


def _dot(a, b):
    return jnp.dot(a, b, preferred_element_type=F32)


def _dot_nt(a, b):
    return lax.dot_general(a, b, (((1,), (1,)), ((), ())), preferred_element_type=F32)


def _dot_tn(a, b):
    return lax.dot_general(a, b, (((0,), (0,)), ((), ())), preferred_element_type=F32)


@functools.lru_cache(maxsize=None)
def _dft_matrices(L):
    kt = (np.arange(L, dtype=np.int64)[:, None] * np.arange(L, dtype=np.int64)[None, :]) % (2 * L)
    ang = kt.astype(np.float64) * (math.pi / L)
    return np.cos(ang).astype(np.float32), np.sin(ang).astype(np.float32)


@functools.lru_cache(maxsize=None)
def _filter_features(L):
    t = np.linspace(0.0, 1.0, L)[:, None]
    omega = 2.0 * math.pi * np.arange(L)[:, None] / L
    bands = np.linspace(1e-4, HY_BANDS - 1, HY_BANDS)[None, :]
    z = np.concatenate([t, np.cos(bands * omega), -np.sin(bands * omega)], axis=-1)
    z = np.pad(z, ((0, 0), (0, HY_EMB_PAD - HY_EMB)))
    deltas = np.abs(np.linspace(math.log(HY_TARGET) / HY_SLOW_DECAY, math.log(HY_TARGET) / HY_FAST_DECAY, HY_W))
    window = np.exp(-t * deltas[None, :])
    return z.astype(np.float32), window.astype(np.float32)


@functools.lru_cache(maxsize=None)
def _rope_tables(L):
    half = RET_HEAD_DIM // 2
    quarter = half // 2
    t = np.arange(L)
    pos = np.stack([t // GRID_W, t % GRID_W], axis=0).astype(np.float64)
    freqs = ROPE_BASE ** (-np.arange(quarter, dtype=np.float64) / quarter)
    cos = np.zeros((L, RET_HEAD_DIM))
    sin = np.zeros((L, RET_HEAD_DIM))
    for a in range(2):
        ang = pos[a][:, None] * freqs[None, :]
        base = a * half
        cos[:, base:base + quarter] = np.cos(ang)
        cos[:, base + quarter:base + half] = np.cos(ang)
        sin[:, base:base + quarter] = -np.sin(ang)
        sin[:, base + quarter:base + half] = np.sin(ang)
    return (np.tile(cos, (1, RET_HEADS)).astype(np.float32), np.tile(sin, (1, RET_HEADS)).astype(np.float32))


@functools.lru_cache(maxsize=None)
def _head_block_ones(width):
    i = np.arange(width) // NA_HEAD_DIM
    return (i[:, None] == i[None, :]).astype(np.float32)


def _mod_kernel(c_ref, w_ref, b_ref, o_ref):
    s = _silu(c_ref[...])
    o_ref[...] = jnp.dot(s, w_ref[...], preferred_element_type=F32, precision=lax.Precision.HIGHEST) + b_ref[...]


def _modulation(cc, ada_w, ada_b):
    R = cc.shape[0]
    tn = 1024
    return pl.pallas_call(
        _mod_kernel,
        grid=(DEPTH, 3 * D_MODEL // tn),
        in_specs=[
            pl.BlockSpec((R, D_MODEL), lambda i, j: (0, 0)),
            pl.BlockSpec((None, D_MODEL, tn), lambda i, j: (i, 0, j)),
            pl.BlockSpec((None, 1, tn), lambda i, j: (i, 0, j)),
        ],
        out_specs=pl.BlockSpec((None, R, tn), lambda i, j: (i, 0, j)),
        out_shape=jax.ShapeDtypeStruct((DEPTH, R, 3 * D_MODEL), F32),
        compiler_params=_cparams("parallel", "parallel"),
        name="modulation",
    )(cc, ada_w, ada_b.reshape(DEPTH, 1, 3 * D_MODEL))


def _inproj_kernel(x_ref, mod_ref, nw_ref, w_ref, o_ref):
    x = x_ref[...]
    xn = x * lax.rsqrt(jnp.mean(x * x, axis=-1, keepdims=True) + EPS)
    shift = mod_ref[:, 0:D_MODEL]
    scale = mod_ref[:, D_MODEL:2 * D_MODEL]
    h = (xn * nw_ref[...] * (1.0 + scale) + shift).astype(BF16)
    tn = 1024
    for j in range(IN_W // tn):
        o_ref[:, j * tn:(j + 1) * tn] = _dot(h, w_ref[:, j * tn:(j + 1) * tn]).astype(o_ref.dtype)


def _in_projection(xf, mod, mod_index, norm_w, w_bf, tm):
    R = xf.shape[0]
    return pl.pallas_call(
        _inproj_kernel,
        grid=(R // tm,),
        in_specs=[
            pl.BlockSpec((tm, D_MODEL), lambda i: (i, 0)),
            pl.BlockSpec((None, 1, 3 * D_MODEL), lambda i: (mod_index(i), 0, 0)),
            _resident((1, D_MODEL), lambda i: (0, 0)),
            _resident((D_MODEL, IN_W), lambda i: (0, 0)),
        ],
        out_specs=pl.BlockSpec((tm, IN_W), lambda i: (i, 0)),
        out_shape=jax.ShapeDtypeStruct((R, IN_W), BF16),
        compiler_params=_cparams("parallel"),
        name="in_projection",
    )(xf, mod, norm_w, w_bf)


def _outproj_kernel(za_ref, zy_ref, zr_ref, x_ref, mod_ref, w_ref, o_ref):
    acc = _dot(za_ref[...], w_ref[0:NA_W, :])
    acc += _dot(zy_ref[...], w_ref[NA_W:NA_W + HY_W, :])
    acc += _dot(zr_ref[...], w_ref[NA_W + HY_W:MIX_W, :])
    gate = mod_ref[:, 2 * D_MODEL:3 * D_MODEL]
    o_ref[...] = x_ref[...] + gate * acc


def _out_projection(za, zy, zr, xf, mod, mod_index, w_bf, tm):
    R = xf.shape[0]
    return pl.pallas_call(
        _outproj_kernel,
        grid=(R // tm,),
        in_specs=[
            pl.BlockSpec((tm, NA_W), lambda i: (i, 0)),
            pl.BlockSpec((tm, HY_W), lambda i: (i, 0)),
            pl.BlockSpec((tm, RET_W), lambda i: (i, 0)),
            pl.BlockSpec((tm, D_MODEL), lambda i: (i, 0)),
            pl.BlockSpec((None, 1, 3 * D_MODEL), lambda i: (mod_index(i), 0, 0)),
            _resident((MIX_W, D_MODEL), lambda i: (0, 0)),
        ],
        out_specs=pl.BlockSpec((tm, D_MODEL), lambda i: (i, 0)),
        out_shape=jax.ShapeDtypeStruct((R, D_MODEL), F32),
        compiler_params=_cparams("parallel"),
        name="out_projection",
    )(za, zy, zr, xf, mod, w_bf)


def _head_rms(x, ones_bf, gain):
    ss = _dot((x * x).astype(BF16), ones_bf)
    return x * lax.rsqrt(ss * (1.0 / NA_HEAD_DIM) + EPS) * gain


def _pair_masks():
    lane = lax.broadcasted_iota(jnp.int32, (1, 2 * NA_HEAD_DIM), 1)
    return lane < NA_HEAD_DIM


def _attend_heads(q, key_parts, val_parts, bias_fn):
    first = _pair_masks()
    outs = []
    for pair in range(NA_HEADS // 2):
        lo = pair * 2 * NA_HEAD_DIM
        hi = lo + 2 * NA_HEAD_DIM
        qp = q[:, lo:hi]
        o_pair = None
        for sub in range(2):
            h = 2 * pair + sub
            sel = first if sub == 0 else jnp.logical_not(first)
            qm = jnp.where(sel, qp, 0.0).astype(BF16)
            scores = []
            for part, kp in enumerate(key_parts):
                s = _dot_nt(qm, kp[:, lo:hi])
                b = bias_fn(h, part)
                if b is not None:
                    s = s + b
                scores.append(s)
            m = scores[0].max(axis=-1, keepdims=True)
            for s in scores[1:]:
                m = jnp.maximum(m, s.max(axis=-1, keepdims=True))
            denom = None
            acc = None
            for s, vp in zip(scores, val_parts):
                p = jnp.exp(s - m)
                ps = p.sum(axis=-1, keepdims=True)
                denom = ps if denom is None else denom + ps
                pv = _dot(p.astype(BF16), vp[:, lo:hi])
                acc = pv if acc is None else acc + pv
            o_h = acc * (1.0 / denom)
            o_pair = o_h if o_pair is None else jnp.where(first, o_pair, o_h)
        outs.append(o_pair)
    return jnp.concatenate(outs, axis=-1)


def _na_kernel(q_ref, k_ref, v_ref, g_ref, kc_ref, vc_ref, qg_ref, kg_ref, ones_ref, bias_ref, o_ref,
               kn_scr, kcn_scr, *, rows):
    grp = pl.program_id(1)
    ones_bf = ones_ref[...]
    tq = NA_G * GRID_W
    nwin = NA_WIN * GRID_W

    @pl.when(grp == 0)
    def _():
        kgain = kg_ref[...]
        ck = 256
        for i in range(k_ref.shape[0] // ck):
            kk = k_ref[i * ck:(i + 1) * ck, :].astype(F32)
            kn_scr[i * ck:(i + 1) * ck, :] = _head_rms(kk, ones_bf, kgain).astype(BF16)
        for i in range(kc_ref.shape[0] // ck):
            kk = kc_ref[i * ck:(i + 1) * ck, :].astype(F32)
            kcn_scr[i * ck:(i + 1) * ck, :] = _head_rms(kk, ones_bf, kgain).astype(BF16)

    r0 = grp * NA_G
    ws = jnp.clip(r0 - NA_KH // 2, 0, rows - NA_WIN)
    q = _head_rms(q_ref[...].astype(F32), ones_bf, qg_ref[...] * (NA_HEAD_DIM ** -0.5))

    key0 = pl.multiple_of(ws * GRID_W, GRID_W)
    kwin = kn_scr[pl.ds(key0, nwin), :]
    vwin = v_ref[pl.ds(key0, nwin), :]

    row = lax.broadcasted_iota(jnp.int32, (tq, nwin), 0)
    col = lax.broadcasted_iota(jnp.int32, (tq, nwin), 1)
    start = jnp.clip(r0 + row // GRID_W - NA_KH // 2, 0, rows - NA_KH)
    lo = (start - ws) * GRID_W
    rowmask = jnp.where((col >= lo) & (col < lo + NA_KH * GRID_W), 0.0, NEG_INF).astype(F32)

    dr_base = ws - r0 + (NA_KH - 1) - NA_DR_MIN

    def bias_fn(h, part):
        if part == 1:
            return None
        blocks = []
        for g in range(NA_G):
            blocks.append(jnp.concatenate(
                [bias_ref[dr_base + (w - g), h] for w in range(0, NA_WIN, 2)], axis=-1))
        return jnp.concatenate(blocks, axis=0) + rowmask

    o = _attend_heads(q, [kwin, kcn_scr[...]], [vwin, vc_ref[...]], bias_fn)
    o_ref[...] = (o * _silu(g_ref[...].astype(F32))).astype(o_ref.dtype)


def _na_bias_table(rpb):
    cq = np.arange(GRID_W)
    ck = np.arange(GRID_W)
    col_start = np.clip(cq - NA_KW // 2, 0, GRID_W - NA_KW)
    col_ok = (ck[None, :] >= col_start[:, None]) & (ck[None, :] < col_start[:, None] + NA_KW)
    dc = np.clip(ck[None, :] - cq[:, None] + (NA_KW - 1), 0, 2 * NA_KW - 2)
    t = rpb[:, :, dc].astype(F32)
    t = jnp.where(col_ok[None, None], t, NEG_INF)
    t = jnp.transpose(t, (1, 0, 2, 3))
    n_dr = 2 * NA_KH - 1
    lo_pad = -NA_DR_MIN
    hi_pad = NA_DR_NUM + 1 - lo_pad - n_dr
    t = jnp.pad(t, ((lo_pad, hi_pad), (0, 0), (0, 0), (0, 0)))
    return jnp.concatenate([t[:-1], t[1:]], axis=-1)


def _neighborhood_attention(u, uc, q_gain, k_gain, rpb, B, L, Lc):
    rows = L // GRID_W
    ngrp = rows // NA_G
    tq = NA_G * GRID_W
    nb = NA_W // CB * 0 + 1
    del nb
    ones = jnp.asarray(_head_block_ones(NA_W)).astype(BF16)
    bias = _na_bias_table(rpb)
    qg = jnp.tile(q_gain, NA_HEADS)[None, :]
    kg = jnp.tile(k_gain, NA_HEADS)[None, :]
    return pl.pallas_call(
        functools.partial(_na_kernel, rows=rows),
        grid=(B, ngrp),
        in_specs=[
            pl.BlockSpec((tq, NA_W), lambda b, g: (b * ngrp + g, 0)),
            pl.BlockSpec((L, NA_W), lambda b, g: (b, 1)),
            pl.BlockSpec((L, NA_W), lambda b, g: (b, 2)),
            pl.BlockSpec((tq, NA_W), lambda b, g: (b * ngrp + g, 3)),
            pl.BlockSpec((Lc, NA_W), lambda b, g: (b, 1)),
            pl.BlockSpec((Lc, NA_W), lambda b, g: (b, 2)),
            _resident((1, NA_W), lambda b, g: (0, 0)),
            _resident((1, NA_W), lambda b, g: (0, 0)),
            _resident((NA_W, NA_W), lambda b, g: (0, 0)),
            _resident((NA_DR_NUM, NA_HEADS, GRID_W, 2 * GRID_W), lambda b, g: (0, 0, 0, 0)),
        ],
        out_specs=pl.BlockSpec((tq, NA_W), lambda b, g: (b * ngrp + g, 0)),
        out_shape=jax.ShapeDtypeStruct((B * L, NA_W), BF16),
        scratch_shapes=[pltpu.VMEM((L, NA_W), BF16), pltpu.VMEM((Lc, NA_W), BF16)],
        compiler_params=_cparams("parallel", "arbitrary"),
        name="neighborhood_attention",
    )(u, u, u, u, uc, uc, qg, kg, ones, bias)


def _ctx_attn_kernel(q_ref, k_ref, v_ref, g_ref, qg_ref, kg_ref, ones_ref, o_ref):
    ones_bf = ones_ref[...]
    q = _head_rms(q_ref[...].astype(F32), ones_bf, qg_ref[...] * (NA_HEAD_DIM ** -0.5))
    k = _head_rms(k_ref[...].astype(F32), ones_bf, kg_ref[...]).astype(BF16)
    o = _attend_heads(q, [k], [v_ref[...]], lambda h, part: None)
    o_ref[...] = (o * _silu(g_ref[...].astype(F32))).astype(o_ref.dtype)


def _context_attention(uc, q_gain, k_gain, B, Lc):
    ones = jnp.asarray(_head_block_ones(NA_W)).astype(BF16)
    qg = jnp.tile(q_gain, NA_HEADS)[None, :]
    kg = jnp.tile(k_gain, NA_HEADS)[None, :]
    return pl.pallas_call(
        _ctx_attn_kernel,
        grid=(B,),
        in_specs=[
            pl.BlockSpec((Lc, NA_W), lambda b: (b, 0)),
            pl.BlockSpec((Lc, NA_W), lambda b: (b, 1)),
            pl.BlockSpec((Lc, NA_W), lambda b: (b, 2)),
            pl.BlockSpec((Lc, NA_W), lambda b: (b, 3)),
            _resident((1, NA_W), lambda b: (0, 0)),
            _resident((1, NA_W), lambda b: (0, 0)),
            _resident((NA_W, NA_W), lambda b: (0, 0)),
        ],
        out_specs=pl.BlockSpec((Lc, NA_W), lambda b: (b, 0)),
        out_shape=jax.ShapeDtypeStruct((B * Lc, NA_W), BF16),
        compiler_params=_cparams("parallel"),
        name="context_attention",
    )(uc, uc, uc, uc, qg, kg, ones)


def _alt_sign(shape):
    row = lax.broadcasted_iota(jnp.int32, shape, 0)
    return (1 - 2 * (row & 1)).astype(F32)


def _filter_kernel(z_ref, win_ref, w1_ref, b1_ref, w2_ref, b2_ref, w3_ref, sf_ref, c_ref, s_ref,
                   hr_ref, hi_ref, hn_ref, *, L):
    hp = lax.Precision.HIGHEST
    h = jnp.sin(sf_ref[0:1, :] * (jnp.dot(z_ref[...], w1_ref[...], preferred_element_type=F32, precision=hp) + b1_ref[...]))
    h = jnp.sin(sf_ref[1:2, :] * (jnp.dot(h, w2_ref[...], preferred_element_type=F32, precision=hp) + b2_ref[...]))
    win = win_ref[...]
    row = lax.broadcasted_iota(jnp.int32, (L, HY_W), 0)
    sign = _alt_sign((L, HY_W))
    inv_n = 1.0 / (2 * L)
    hf = jnp.dot(h, w3_ref[:, 0:HY_W], preferred_element_type=F32, precision=hp) * win
    hb = jnp.dot(h, w3_ref[:, HY_W:2 * HY_W], preferred_element_type=F32, precision=hp) * win
    hb = jnp.where(row == 0, 0.0, hb)
    norm = jnp.sum(jnp.abs(hf), axis=0, keepdims=True) + jnp.sum(jnp.abs(hb), axis=0, keepdims=True)
    inv = 1.0 / norm
    even = (hf + hb) * inv
    odd = (hf - hb) * inv
    e_hi = even.astype(BF16)
    e_lo = (even - e_hi.astype(F32)).astype(BF16)
    o_hi = odd.astype(BF16)
    o_lo = (odd - o_hi.astype(F32)).astype(BF16)
    wgt = jnp.where(row == 0, inv_n, 2.0 * inv_n)
    hr_ref[...] = (_dot(c_ref[...], e_hi) + _dot(c_ref[...], e_lo)) * wgt
    hi_ref[...] = -(_dot(s_ref[...], o_hi) + _dot(s_ref[...], o_lo)) * wgt
    hn_ref[...] = jnp.sum(even * sign, axis=0, keepdims=True) * inv_n


def _hyena_filter(L, w1, b1, w2, b2, w3, sin_freq, c_bf, s_bf):
    z, window = _filter_features(L)
    w1p = jnp.pad(w1, ((0, HY_EMB_PAD - HY_EMB), (0, 0)))
    full = lambda *shape: _resident(shape, lambda o: (0,) * len(shape))
    per_order = lambda rows: pl.BlockSpec((None, rows, HY_W), lambda o: (o, 0, 0))
    return pl.pallas_call(
        functools.partial(_filter_kernel, L=L),
        grid=(2,),
        in_specs=[
            full(L, HY_EMB_PAD), full(L, HY_W), full(HY_EMB_PAD, HY_FFN), full(1, HY_FFN),
            full(HY_FFN, HY_FFN), full(1, HY_FFN), pl.BlockSpec((HY_FFN, 2 * HY_W), lambda o: (0, o)),
            full(2, HY_FFN), full(L, L), full(L, L),
        ],
        out_specs=[per_order(L), per_order(L), per_order(1)],
        out_shape=[jax.ShapeDtypeStruct((2, L, HY_W), F32), jax.ShapeDtypeStruct((2, L, HY_W), F32),
                   jax.ShapeDtypeStruct((2, 1, HY_W), F32)],
        compiler_params=_cparams("parallel"),
        name="hyena_filter",
    )(jnp.asarray(z), jnp.asarray(window), w1p, b1[None, :], w2, b2[None, :], w3, sin_freq, c_bf, s_bf)


def _hyena_kernel(v_ref, x1_ref, x2_ref, g_ref, cw_ref, cb_ref, skip_ref, c_ref, s_ref, hr_ref, hi_ref, hn_ref,
                  o_ref, a_scr, abf_scr, x_scr, zr_scr, zi_scr, *, L):
    ck = min(HY_ROWS, L)
    halo = 16
    chunks = [(t0, t0 + ck) for t0 in range(0, L, ck)]
    sign = _alt_sign((ck, HY_W))

    def short_conv(ref, j, t0, t1):
        lo, hi = max(t0 - halo, 0), min(t1 + halo, L)
        u = ref[lo:hi, :].astype(F32)
        row = lo + lax.broadcasted_iota(jnp.int32, (hi - lo, HY_W), 0)
        prev = jnp.where(row == 0, 0.0, pltpu.roll(u, 1, 0))
        nxt = jnp.where(row == L - 1, 0.0, pltpu.roll(u, hi - lo - 1, 0))
        w = cw_ref[:, j * HY_W:(j + 1) * HY_W]
        z = prev * w[0:1, :] + u * w[1:2, :] + nxt * w[2:3, :] + cb_ref[:, j * HY_W:(j + 1) * HY_W]
        return z[t0 - lo:t1 - lo, :]

    def spectrum_product(order):
        for k0, k1 in chunks:
            ur = _dot(c_ref[k0:k1, :], abf_scr[...])
            us = _dot(s_ref[k0:k1, :], abf_scr[...])
            hr = hr_ref[order, k0:k1, :]
            hi = hi_ref[order, k0:k1, :]
            zr_scr[k0:k1, :] = (ur * hr + us * hi).astype(BF16)
            zi_scr[k0:k1, :] = (us * hr - ur * hi).astype(BF16)

    def long_conv_chunk(order, t0, t1, nyq):
        a = a_scr[t0:t1, :]
        y = _dot(c_ref[t0:t1, :], zr_scr[...]) + _dot(s_ref[t0:t1, :], zi_scr[...])
        return y + sign * nyq + a * skip_ref[order:order + 1, :]

    nyq = jnp.zeros((1, HY_W), F32)
    for t0, t1 in chunks:
        v = short_conv(v_ref, 0, t0, t1)
        a_scr[t0:t1, :] = v
        abf_scr[t0:t1, :] = v.astype(BF16)
        x_scr[t0:t1, :] = short_conv(x1_ref, 1, t0, t1)
        nyq += jnp.sum(v * sign, axis=0, keepdims=True)
    spectrum_product(0)
    nyq0 = nyq * hn_ref[0]
    nyq = jnp.zeros((1, HY_W), F32)
    for t0, t1 in chunks:
        y = x_scr[t0:t1, :] * long_conv_chunk(0, t0, t1, nyq0)
        a_scr[t0:t1, :] = y
        abf_scr[t0:t1, :] = y.astype(BF16)
        nyq += jnp.sum(y * sign, axis=0, keepdims=True)
    spectrum_product(1)
    nyq1 = nyq * hn_ref[1]
    for t0, t1 in chunks:
        y = short_conv(x2_ref, 2, t0, t1) * long_conv_chunk(1, t0, t1, nyq1)
        o_ref[t0:t1, :] = (y * _silu(g_ref[t0:t1, :].astype(F32))).astype(o_ref.dtype)


def _hyena(u, conv_w, conv_b, skip, c_bf, s_bf, hr, hi, hn, B, L):
    col = lambda j: pl.BlockSpec((L, CB), lambda b: (b, j))
    return pl.pallas_call(
        functools.partial(_hyena_kernel, L=L),
        grid=(B,),
        in_specs=[
            col(CB_HY_V), col(CB_HY_X1), col(CB_HY_X2), col(CB_HY_G),
            _resident((3, 3 * HY_W), lambda b: (0, 0)),
            _resident((1, 3 * HY_W), lambda b: (0, 0)),
            _resident((2, HY_W), lambda b: (0, 0)),
            _resident((L, L), lambda b: (0, 0)),
            _resident((L, L), lambda b: (0, 0)),
            _resident((2, L, HY_W), lambda b: (0, 0, 0)),
            _resident((2, L, HY_W), lambda b: (0, 0, 0)),
            _resident((2, 1, HY_W), lambda b: (0, 0, 0)),
        ],
        out_specs=pl.BlockSpec((L, HY_W), lambda b: (b, 0)),
        out_shape=jax.ShapeDtypeStruct((B * L, HY_W), BF16),
        scratch_shapes=[
            pltpu.VMEM((L, HY_W), F32), pltpu.VMEM((L, HY_W), BF16), pltpu.VMEM((L, HY_W), F32),
            pltpu.VMEM((L, HY_W), BF16), pltpu.VMEM((L, HY_W), BF16),
        ],
        compiler_params=_cparams("parallel"),
        name="hyena",
    )(u, u, u, u, conv_w, conv_b[None, :], skip, c_bf, s_bf, hr, hi, hn)


def _ret_kernel(*refs, L, Lc, has_init):
    if has_init:
        (q_ref, k_ref, v_ref, g_ref, kc_ref, vc_ref, cos_ref, sin_ref, rate_ref, rrow_ref, ones_ref,
         o_ref, q_scr, k_scr, sf_scr, sb_scr, r_scr) = refs
    else:
        (q_ref, k_ref, v_ref, g_ref, cos_ref, sin_ref, rate_ref, rrow_ref, ones_ref,
         o_ref, q_scr, k_scr, sf_scr, sb_scr, r_scr) = refs
    C = min(RET_CHUNK, L)
    nch = L // C
    W = RET_W
    quarter = RET_HEAD_DIM // 4
    block = ones_ref[...]
    lg = -jnp.exp(rate_ref[...])
    lg_f, lg_b = lg[0:1, :], lg[1:2, :]

    lane = lax.broadcasted_iota(jnp.int32, (L, W), 1)
    first_quarter = (lane % (2 * quarter)) < quarter

    def rope(a):
        swapped = jnp.where(first_quarter, pltpu.roll(a, W - quarter, 1), pltpu.roll(a, quarter, 1))
        return a * cos_ref[...] + swapped * sin_ref[...]

    if has_init:
        q_scr[...] = rope(q_ref[...].astype(F32)).astype(BF16)
        k_scr[...] = rope(k_ref[...].astype(F32) * (RET_HEAD_DIM ** -0.5)).astype(BF16)
    else:
        q_scr[...] = q_ref[...]
        k_scr[...] = (k_ref[...].astype(F32) * (RET_HEAD_DIM ** -0.5)).astype(BF16)

    def decays(n_rows):
        pos = lax.broadcasted_iota(jnp.int32, (n_rows, W), 0).astype(F32)
        return (jnp.exp(lg_f * (n_rows - 1.0 - pos)),
                jnp.exp(lg_b * pos))

    def chunk_states(k_bf, v_bf, zf, zb):
        kf = (k_bf.astype(F32) * zf).astype(BF16)
        kb = (k_bf.astype(F32) * zb).astype(BF16)
        return _dot_tn(kf, v_bf) * block, _dot_tn(kb, v_bf) * block

    zeta_f, zeta_b = decays(C)
    if has_init:
        zc_f, zc_b = decays(Lc)
        kc = (kc_ref[...].astype(F32) * (RET_HEAD_DIM ** -0.5)).astype(BF16)
        s0_f, s0_b = chunk_states(kc, vc_ref[...], zc_f, zc_b)
    else:
        s0_f = jnp.zeros((W, W), F32)
        s0_b = jnp.zeros((W, W), F32)

    for n in range(nch):
        kv_f, kv_b = chunk_states(k_scr[n * C:(n + 1) * C, :], v_ref[n * C:(n + 1) * C, :], zeta_f, zeta_b)
        sf_scr[n] = kv_f
        sb_scr[n] = kv_b
    dec_f = jnp.exp(lg_f * float(C))
    dec_b = jnp.exp(lg_b * float(C))
    state = s0_f
    for n in range(nch):
        kv = sf_scr[n]
        sf_scr[n] = state
        state = dec_f * state + kv
    state = s0_b
    for n in range(nch - 1, -1, -1):
        kv = sb_scr[n]
        sb_scr[n] = state
        state = dec_b * state + kv

    posc = lax.broadcasted_iota(jnp.int32, (C, W), 0).astype(F32)
    xi_f = jnp.exp(lg_f * (posc + 1.0))
    xi_b = jnp.exp(lg_b * (float(C) - posc))
    diff = (lax.broadcasted_iota(jnp.int32, (C, C), 0) - lax.broadcasted_iota(jnp.int32, (C, C), 1)).astype(F32)
    lane_c = lax.broadcasted_iota(jnp.int32, (1, W), 1)
    dmask = []
    for h in range(RET_HEADS):
        rf = -jnp.exp(rrow_ref[h:h + 1, 0:C])
        rb = -jnp.exp(rrow_ref[RET_HEADS + h:RET_HEADS + h + 1, 0:C])
        dmask.append(jnp.where(diff >= 0, jnp.exp(rf * jnp.maximum(diff, 0.0)), 0.0)
                     + jnp.where(diff <= 0, jnp.exp(rb * jnp.maximum(-diff, 0.0)), 0.0))

    for n in range(nch):
        qn = q_scr[n * C:(n + 1) * C, :]
        kn = k_scr[n * C:(n + 1) * C, :]
        vn = v_ref[n * C:(n + 1) * C, :]
        qf = qn.astype(F32)
        o = _dot((qf * xi_f).astype(BF16), sf_scr[n].astype(BF16))
        o += _dot((qf * xi_b).astype(BF16), sb_scr[n].astype(BF16))
        for h in range(RET_HEADS):
            sel = (lane_c // RET_HEAD_DIM) == h
            s = _dot_nt(jnp.where(sel, qn, jnp.zeros_like(qn)), kn) * dmask[h]
            o += jnp.where(sel, _dot(s.astype(BF16), vn), 0.0)
        r_scr[n * C:(n + 1) * C, :] = o

    r = r_scr[...]
    ss = _dot((r * r).astype(BF16), block.astype(BF16))
    rn = r * lax.rsqrt(ss * (1.0 / RET_HEAD_DIM) + EPS)
    o_ref[...] = (rn * _silu(g_ref[...].astype(F32))).astype(o_ref.dtype)


def _retention(u, uc, ret_log_rate, B, L, Lc, has_init):
    C = min(RET_CHUNK, L)
    nch = L // C
    col = lambda j: pl.BlockSpec((L, CB), lambda b: (b, j))
    ccol = lambda j: pl.BlockSpec((Lc, CB), lambda b: (b, j))
    cos, sin = _rope_tables(L)
    rate_lane = jnp.repeat(ret_log_rate, RET_HEAD_DIM, axis=1)
    rate_row = jnp.broadcast_to(ret_log_rate.reshape(2 * RET_HEADS, 1), (2 * RET_HEADS, RET_W))
    ones = jnp.asarray(_head_block_ones(RET_W))
    in_specs = [col(CB_RE_Q), col(CB_RE_K), col(CB_RE_V), col(CB_RE_G)]
    args = [u, u, u, u]
    if has_init:
        in_specs += [ccol(CB_RE_K), ccol(CB_RE_V)]
        args += [uc, uc]
    in_specs += [
        _resident((L, RET_W), lambda b: (0, 0)),
        _resident((L, RET_W), lambda b: (0, 0)),
        _resident((2, RET_W), lambda b: (0, 0)),
        _resident((2 * RET_HEADS, RET_W), lambda b: (0, 0)),
        _resident((RET_W, RET_W), lambda b: (0, 0)),
    ]
    args += [jnp.asarray(cos), jnp.asarray(sin), rate_lane, rate_row, ones]
    return pl.pallas_call(
        functools.partial(_ret_kernel, L=L, Lc=Lc, has_init=has_init),
        grid=(B,),
        in_specs=in_specs,
        out_specs=pl.BlockSpec((L, RET_W), lambda b: (b, 0)),
        out_shape=jax.ShapeDtypeStruct((B * L, RET_W), BF16),
        scratch_shapes=[
            pltpu.VMEM((L, RET_W), BF16), pltpu.VMEM((L, RET_W), BF16),
            pltpu.VMEM((nch, RET_W, RET_W), F32), pltpu.VMEM((nch, RET_W, RET_W), F32),
            pltpu.VMEM((L, RET_W), F32),
        ],
        compiler_params=_cparams("parallel"),
        name="retention",
    )(*args)


def kernel(x, c, ctx, c_ctx, norm_w, ada_w, ada_b, w_in, w_out, na_q_gain, na_k_gain, na_rpb, hy_conv_w, hy_conv_b,
           hy_w1, hy_b1, hy_w2, hy_b2, hy_w3, hy_sin_freq, hy_skip, ret_log_rate):
    B, L, D = x.shape
    Lc = ctx.shape[1]
    assert D == D_MODEL and L % (GRID_W * NA_G) == 0 and L // GRID_W >= NA_WIN

    n_cond = 16
    cc = jnp.concatenate([c, c_ctx[None, :], jnp.zeros((n_cond - B - 1, D), F32)], axis=0)
    mods = _modulation(cc, ada_w, ada_b)

    cx, sx = (jnp.asarray(a).astype(BF16) for a in _dft_matrices(L))
    ccx, scx = (jnp.asarray(a).astype(BF16) for a in _dft_matrices(Lc))

    tm_x = 512
    xf = x.reshape(B * L, D)
    cf = ctx.reshape(B * Lc, D)
    x_mod = lambda i: i // (L // tm_x)
    c_mod = lambda i: B

    for i in range(DEPTH):
        mod = mods[i].reshape(n_cond, 1, 3 * D)
        w_in_bf = w_in[i].astype(BF16)
        w_out_bf = w_out[i].astype(BF16)
        nw = norm_w[i][None, :]
        filt = (hy_w1[i], hy_b1[i], hy_w2[i], hy_b2[i], hy_w3[i], hy_sin_freq[i])

        u = _in_projection(xf, mod, x_mod, nw, w_in_bf, tm_x)
        uc = _in_projection(cf, mod, c_mod, nw, w_in_bf, Lc)

        za = _neighborhood_attention(u, uc, na_q_gain[i], na_k_gain[i], na_rpb[i], B, L, Lc)
        hr, hi, hn = _hyena_filter(L, *filt, cx, sx)
        zy = _hyena(u, hy_conv_w[i], hy_conv_b[i], hy_skip[i], cx, sx, hr, hi, hn, B, L)
        zr = _retention(u, uc, ret_log_rate[i], B, L, Lc, True)
        x_new = _out_projection(za, zy, zr, xf, mod, x_mod, w_out_bf, tm_x)

        if i < DEPTH - 1:
            zac = _context_attention(uc, na_q_gain[i], na_k_gain[i], B, Lc)
            hrc, hic, hnc = _hyena_filter(Lc, *filt, ccx, scx)
            zyc = _hyena(uc, hy_conv_w[i], hy_conv_b[i], hy_skip[i], ccx, scx, hrc, hic, hnc, B, Lc)
            zrc = _retention(uc, uc, ret_log_rate[i], B, Lc, Lc, False)
            cf = _out_projection(zac, zyc, zrc, cf, mod, c_mod, w_out_bf, Lc)
        xf = x_new

    return xf.reshape(B, L, D)
```

```python
import functools
import math

import numpy as np
import jax
import jax.numpy as jnp
from jax import lax
from jax.experimental import pallas as pl
from jax.experimental.pallas import tpu as pltpu

F32 = jnp.float32
BF16 = jnp.bfloat16

D_MODEL = 1024
DEPTH = 2
GRID_W = 64
NA_HEADS = 8
NA_HEAD_DIM = 64
NA_W = NA_HEADS * NA_HEAD_DIM
NA_KH = 8
NA_KW = 16
HY_W = 256
HY_BANDS = 8
HY_EMB = 1 + 2 * HY_BANDS
HY_EMB_PAD = 32
HY_FFN = 64
HY_FAST_DECAY = 0.3
HY_SLOW_DECAY = 1.5
HY_TARGET = 1e-2
RET_HEADS = 4
RET_HEAD_DIM = 64
RET_W = RET_HEADS * RET_HEAD_DIM
ROPE_BASE = 10000.0
EPS = 1e-6
NEG_INF = -1e30
LOG2E = 1.4426950408889634
IN_W = 4 * NA_W + 4 * HY_W + 4 * RET_W
MIX_W = NA_W + HY_W + RET_W

CB = 256
CB_HY_V, CB_HY_X1, CB_HY_X2, CB_HY_G = 8, 9, 10, 11
CB_RE_Q, CB_RE_K, CB_RE_V, CB_RE_G = 12, 13, 14, 15

NA_G = 4
NA_WIN = NA_G + NA_KH
NA_DR_MIN = -4
NA_DR_NUM = 23
RET_CHUNK = 256
HY_ROWS = 512
VMEM_LIMIT = 56 * 1024 * 1024


def _cparams(*sem):
    return pltpu.CompilerParams(dimension_semantics=sem, vmem_limit_bytes=VMEM_LIMIT)


def _resident(shape, index_map):
    return pl.BlockSpec(shape, index_map, pipeline_mode=pl.Buffered(1))


def _silu(x):
    return x * (1.0 / (1.0 + jnp.exp(-x)))


def _dot(a, b):
    return jnp.dot(a, b, preferred_element_type=F32)


def _dot_nt(a, b):
    return lax.dot_general(a, b, (((1,), (1,)), ((), ())), preferred_element_type=F32)


def _dot_tn(a, b):
    return lax.dot_general(a, b, (((0,), (0,)), ((), ())), preferred_element_type=F32)


@functools.lru_cache(maxsize=None)
def _dft_matrices(L):
    kt = (np.arange(L, dtype=np.int64)[:, None] * np.arange(L, dtype=np.int64)[None, :]) % (2 * L)
    ang = kt.astype(np.float64) * (math.pi / L)
    return np.cos(ang).astype(np.float32), np.sin(ang).astype(np.float32)


@functools.lru_cache(maxsize=None)
def _filter_features(L):
    t = np.linspace(0.0, 1.0, L)[:, None]
    omega = 2.0 * math.pi * np.arange(L)[:, None] / L
    bands = np.linspace(1e-4, HY_BANDS - 1, HY_BANDS)[None, :]
    z = np.concatenate([t, np.cos(bands * omega), -np.sin(bands * omega)], axis=-1)
    z = np.pad(z, ((0, 0), (0, HY_EMB_PAD - HY_EMB)))
    deltas = np.abs(np.linspace(math.log(HY_TARGET) / HY_SLOW_DECAY, math.log(HY_TARGET) / HY_FAST_DECAY, HY_W))
    window = np.exp(-t * deltas[None, :])
    return z.astype(np.float32), window.astype(np.float32)


@functools.lru_cache(maxsize=None)
def _rope_tables(L):
    half = RET_HEAD_DIM // 2
    quarter = half // 2
    t = np.arange(L)
    pos = np.stack([t // GRID_W, t % GRID_W], axis=0).astype(np.float64)
    freqs = ROPE_BASE ** (-np.arange(quarter, dtype=np.float64) / quarter)
    cos = np.zeros((L, RET_HEAD_DIM))
    sin = np.zeros((L, RET_HEAD_DIM))
    for a in range(2):
        ang = pos[a][:, None] * freqs[None, :]
        base = a * half
        cos[:, base:base + quarter] = np.cos(ang)
        cos[:, base + quarter:base + half] = np.cos(ang)
        sin[:, base:base + quarter] = -np.sin(ang)
        sin[:, base + quarter:base + half] = np.sin(ang)
    return (np.tile(cos, (1, RET_HEADS)).astype(np.float32), np.tile(sin, (1, RET_HEADS)).astype(np.float32))


@functools.lru_cache(maxsize=None)
def _head_block_ones(width):
    i = np.arange(width) // NA_HEAD_DIM
    return (i[:, None] == i[None, :]).astype(np.float32)


def _mod_kernel(c_ref, w_ref, b_ref, o_ref):
    s = _silu(c_ref[...])
    o_ref[...] = jnp.dot(s, w_ref[...], preferred_element_type=F32, precision=lax.Precision.HIGHEST) + b_ref[...]


def _modulation(cc, ada_w, ada_b):
    R = cc.shape[0]
    tn = 1024
    return pl.pallas_call(
        _mod_kernel,
        grid=(DEPTH, 3 * D_MODEL // tn),
        in_specs=[
            pl.BlockSpec((R, D_MODEL), lambda i, j: (0, 0)),
            pl.BlockSpec((None, D_MODEL, tn), lambda i, j: (i, 0, j)),
            pl.BlockSpec((None, 1, tn), lambda i, j: (i, 0, j)),
        ],
        out_specs=pl.BlockSpec((None, R, tn), lambda i, j: (i, 0, j)),
        out_shape=jax.ShapeDtypeStruct((DEPTH, R, 3 * D_MODEL), F32),
        compiler_params=_cparams("parallel", "parallel"),
        name="modulation",
    )(cc, ada_w, ada_b.reshape(DEPTH, 1, 3 * D_MODEL))


def _inproj_kernel(x_ref, mod_ref, nw_ref, w_ref, o_ref):
    x = x_ref[...]
    xn = x * lax.rsqrt(jnp.mean(x * x, axis=-1, keepdims=True) + EPS)
    shift = mod_ref[:, 0:D_MODEL]
    scale = mod_ref[:, D_MODEL:2 * D_MODEL]
    h = (xn * nw_ref[...] * (1.0 + scale) + shift).astype(BF16)
    tn = 1024
    for j in range(IN_W // tn):
        o_ref[:, j * tn:(j + 1) * tn] = _dot(h, w_ref[:, j * tn:(j + 1) * tn]).astype(o_ref.dtype)


def _in_projection(xf, mod, mod_index, norm_w, w_bf, tm):
    R = xf.shape[0]
    return pl.pallas_call(
        _inproj_kernel,
        grid=(R // tm,),
        in_specs=[
            pl.BlockSpec((tm, D_MODEL), lambda i: (i, 0)),
            pl.BlockSpec((None, 1, 3 * D_MODEL), lambda i: (mod_index(i), 0, 0)),
            _resident((1, D_MODEL), lambda i: (0, 0)),
            _resident((D_MODEL, IN_W), lambda i: (0, 0)),
        ],
        out_specs=pl.BlockSpec((tm, IN_W), lambda i: (i, 0)),
        out_shape=jax.ShapeDtypeStruct((R, IN_W), BF16),
        compiler_params=_cparams("parallel"),
        name="in_projection",
    )(xf, mod, norm_w, w_bf)


def _outproj_kernel(za_ref, zy_ref, zr_ref, x_ref, mod_ref, w_ref, o_ref):
    acc = _dot(za_ref[...], w_ref[0:NA_W, :])
    acc += _dot(zy_ref[...], w_ref[NA_W:NA_W + HY_W, :])
    acc += _dot(zr_ref[...], w_ref[NA_W + HY_W:MIX_W, :])
    gate = mod_ref[:, 2 * D_MODEL:3 * D_MODEL]
    o_ref[...] = x_ref[...] + gate * acc


def _out_projection(za, zy, zr, xf, mod, mod_index, w_bf, tm):
    R = xf.shape[0]
    return pl.pallas_call(
        _outproj_kernel,
        grid=(R // tm,),
        in_specs=[
            pl.BlockSpec((tm, NA_W), lambda i: (i, 0)),
            pl.BlockSpec((tm, HY_W), lambda i: (i, 0)),
            pl.BlockSpec((tm, RET_W), lambda i: (i, 0)),
            pl.BlockSpec((tm, D_MODEL), lambda i: (i, 0)),
            pl.BlockSpec((None, 1, 3 * D_MODEL), lambda i: (mod_index(i), 0, 0)),
            _resident((MIX_W, D_MODEL), lambda i: (0, 0)),
        ],
        out_specs=pl.BlockSpec((tm, D_MODEL), lambda i: (i, 0)),
        out_shape=jax.ShapeDtypeStruct((R, D_MODEL), F32),
        compiler_params=_cparams("parallel"),
        name="out_projection",
    )(za, zy, zr, xf, mod, w_bf)


def _head_rms(x, ones_bf, gain):
    ss = _dot((x * x).astype(BF16), ones_bf)
    return x * lax.rsqrt(ss * (1.0 / NA_HEAD_DIM) + EPS) * gain


def _pair_masks():
    lane = lax.broadcasted_iota(jnp.int32, (1, 2 * NA_HEAD_DIM), 1)
    return lane < NA_HEAD_DIM


def _attend_heads(q, key_parts, val_parts, bias_fn):
    first = _pair_masks()
    outs = []
    for pair in range(NA_HEADS // 2):
        lo = pair * 2 * NA_HEAD_DIM
        hi = lo + 2 * NA_HEAD_DIM
        qp = q[:, lo:hi]
        o_pair = None
        for sub in range(2):
            h = 2 * pair + sub
            sel = first if sub == 0 else jnp.logical_not(first)
            qm = jnp.where(sel, qp, 0.0).astype(BF16)
            scores = []
            for part, kp in enumerate(key_parts):
                s = _dot_nt(qm, kp[:, lo:hi])
                b = bias_fn(h, part)
                if b is not None:
                    s = s + b
                scores.append(s)
            m = scores[0].max(axis=-1, keepdims=True)
            for s in scores[1:]:
                m = jnp.maximum(m, s.max(axis=-1, keepdims=True))
            denom = None
            acc = None
            for s, vp in zip(scores, val_parts):
                p = jnp.exp(s - m)
                ps = p.sum(axis=-1, keepdims=True)
                denom = ps if denom is None else denom + ps
                pv = _dot(p.astype(BF16), vp[:, lo:hi])
                acc = pv if acc is None else acc + pv
            o_h = acc * (1.0 / denom)
            o_pair = o_h if o_pair is None else jnp.where(first, o_pair, o_h)
        outs.append(o_pair)
    return jnp.concatenate(outs, axis=-1)


def _na_kernel(q_ref, k_ref, v_ref, g_ref, kc_ref, vc_ref, qg_ref, kg_ref, ones_ref, rpb_ref, o_ref,
               kn_scr, kcn_scr, va_scr, vb_scr, vca_scr, vcb_scr, bias_scr, mask_scr, s0_scr, s1_scr, p0_scr, p1_scr,
               m0_scr, m1_scr, qm_scr, o_scr, *, rows):
    batch = pl.program_id(0)
    grp = pl.program_id(1)
    ones_bf = ones_ref[...]
    tq = NA_G * GRID_W
    nwin = NA_WIN * GRID_W
    Lc = kc_ref.shape[0]
    pair_w = 2 * NA_HEAD_DIM
    first = lax.broadcasted_iota(jnp.int32, (1, pair_w), 1) < NA_HEAD_DIM

    @pl.when((batch == 0) & (grp == 0))
    def _():
        cq = lax.broadcasted_iota(jnp.int32, (GRID_W, pair_w), 0)
        lane = lax.broadcasted_iota(jnp.int32, (GRID_W, pair_w), 1)
        ck = lane % GRID_W
        col_start = jnp.clip(cq - NA_KW // 2, 0, GRID_W - NA_KW)
        col_ok = (ck >= col_start) & (ck < col_start + NA_KW)
        left = lane < GRID_W

        def body(i, carry):
            for h in range(NA_HEADS):
                v0 = jnp.broadcast_to(rpb_ref[i, h:h + 1, :], (GRID_W, pair_w))
                v1 = jnp.broadcast_to(rpb_ref[i + 1, h:h + 1, :], (GRID_W, pair_w))
                t0 = pltpu.roll(v0, pair_w - (NA_KW - 1), 1, stride=1, stride_axis=0)
                t1 = pltpu.roll(v1, GRID_W - (NA_KW - 1), 1, stride=1, stride_axis=0)
                bias_scr[i, h] = jnp.where(col_ok, jnp.where(left, t0, t1), NEG_INF)
            return carry

        lax.fori_loop(0, NA_DR_NUM, body, 0)

    @pl.when(grp == 0)
    def _():
        kgain = kg_ref[...]
        first_w = (lax.broadcasted_iota(jnp.int32, (1, NA_W), 1) % pair_w) < NA_HEAD_DIM
        one = jnp.ones((), BF16)
        ck = 256
        for src, dst, va, vb, vsrc in ((k_ref, kn_scr, va_scr, vb_scr, v_ref), (kc_ref, kcn_scr, vca_scr, vcb_scr, vc_ref)):
            for i in range(src.shape[0] // ck):
                sl = slice(i * ck, (i + 1) * ck)
                dst[sl, :] = _head_rms(src[sl, :].astype(F32), ones_bf, kgain).astype(BF16)
                vv = vsrc[sl, :]
                va[sl, :] = jnp.where(first_w, vv, one)
                vb[sl, :] = jnp.where(first_w, one, vv)

    r0 = grp * NA_G
    ws = jnp.clip(r0 - NA_KH // 2, 0, rows - NA_WIN)
    key0 = pl.multiple_of(ws * GRID_W, GRID_W)
    dr_base = ws - r0 + (NA_KH - 1) - NA_DR_MIN

    g_id = lax.broadcasted_iota(jnp.int32, (8, nwin), 0)
    col = lax.broadcasted_iota(jnp.int32, (8, nwin), 1)
    lo = (jnp.clip(r0 + g_id - NA_KH // 2, 0, rows - NA_KH) - ws) * GRID_W
    mask_scr[...] = jnp.where((col >= lo) & (col < lo + NA_KH * GRID_W), 0.0, NEG_INF).astype(F32)

    rt = 32
    kt = 256
    n_kt = (nwin + Lc) // kt
    n_rc = tq // rt

    s_bufs = (s0_scr, s1_scr)
    p_bufs = (p0_scr, p1_scr)
    m_bufs = (m0_scr, m1_scr)
    row_zero = jnp.minimum(grp, 0)

    def rows_of(start, size):
        return pl.ds(pl.multiple_of(row_zero + start, rt), size)

    def head_lanes(h):
        return slice((h // 2) * pair_w, (h // 2 + 1) * pair_w)

    def key_rows(j):
        return pl.ds(pl.multiple_of(key0 + j * kt, GRID_W), kt)

    q = _head_rms(q_ref[...].astype(F32), ones_bf, qg_ref[...] * (NA_HEAD_DIM ** -0.5 * LOG2E))
    for h in range(NA_HEADS):
        sel = first if h % 2 == 0 else jnp.logical_not(first)
        qm_scr[h, rows_of(0, tq), :] = jnp.where(sel, q[:, head_lanes(h)], 0.0).astype(BF16)

    def qk_piece(h, j):
        last = j == n_kt - 1
        keys = kcn_scr[:, head_lanes(h)] if last else kn_scr[key_rows(j), head_lanes(h)]
        sv = _dot_nt(qm_scr[h, rows_of(0, tq), :], keys)
        for g in range(NA_G):
            rows_g = rows_of(g * GRID_W, GRID_W)
            sg = sv[g * GRID_W:(g + 1) * GRID_W, :]
            if not last:
                w0 = j * (kt // GRID_W)
                bias = jnp.concatenate(
                    [bias_scr[dr_base + (w0 + dw - g), h] for dw in range(0, kt // GRID_W, 2)], axis=-1)
                sg = sg + bias + mask_scr[g:g + 1, j * kt:(j + 1) * kt]
            s_bufs[h % 2][rows_g, j * kt:(j + 1) * kt] = sg
            mg = jnp.maximum(sg[:, 0:pair_w], sg[:, pair_w:])
            m_bufs[h % 2][rows_g, :] = mg if j == 0 else jnp.maximum(m_bufs[h % 2][rows_g, :], mg)

    def exp_chunk(h, c):
        rows_c = rows_of(c * rt, rt)
        m = m_bufs[h % 2][rows_c, :].max(axis=-1, keepdims=True)
        p_bufs[h % 2][rows_c, :] = jnp.exp2(s_bufs[h % 2][rows_c, :] - m).astype(BF16)

    def pv(h):
        vw, vc = (va_scr, vca_scr) if h % 2 == 0 else (vb_scr, vcb_scr)
        ov = (_dot(p_bufs[h % 2][rows_of(0, tq), 0:nwin], vw[pl.ds(key0, nwin), head_lanes(h)])
              + _dot(p_bufs[h % 2][rows_of(0, tq), nwin:], vc[:, head_lanes(h)]))
        o_h = ov * (1.0 / pltpu.roll(ov, NA_HEAD_DIM, 1))
        if h % 2 == 0:
            o_scr[rows_of(0, tq), head_lanes(h)] = o_h
        else:
            o_scr[rows_of(0, tq), head_lanes(h)] = jnp.where(first, o_scr[rows_of(0, tq), head_lanes(h)], o_h)

    for t in range(NA_HEADS + 2):
        for i in range(n_rc):
            if 0 <= t - 1 < NA_HEADS:
                exp_chunk(t - 1, i)
            if i % 2 == 0 and t < NA_HEADS and i // 2 < n_kt:
                qk_piece(t, i // 2)
            if i == 1 and 0 <= t - 2:
                pv(t - 2)

    o_ref[...] = (o_scr[rows_of(0, tq), :] * _silu(g_ref[...].astype(F32))).astype(o_ref.dtype)


def _neighborhood_attention(u, uc, q_gain, k_gain, rpb, B, L, Lc):
    rows = L // GRID_W
    ngrp = rows // NA_G
    tq = NA_G * GRID_W
    ones = jnp.asarray(_head_block_ones(NA_W)).astype(BF16)
    qg = jnp.tile(q_gain, NA_HEADS)[None, :]
    kg = jnp.tile(k_gain, NA_HEADS)[None, :]
    n_dr = 2 * NA_KH - 1
    rpb_t = jnp.transpose(rpb.astype(F32) * LOG2E, (1, 0, 2))
    rpb_t = jnp.pad(rpb_t, ((-NA_DR_MIN, NA_DR_NUM + 1 + NA_DR_MIN - n_dr), (0, 0), (0, 2 * GRID_W - (2 * NA_KW - 1))))
    return pl.pallas_call(
        functools.partial(_na_kernel, rows=rows),
        grid=(B, ngrp),
        in_specs=[
            pl.BlockSpec((tq, NA_W), lambda b, g: (b * ngrp + g, 0)),
            pl.BlockSpec((L, NA_W), lambda b, g: (b, 1)),
            pl.BlockSpec((L, NA_W), lambda b, g: (b, 2)),
            pl.BlockSpec((tq, NA_W), lambda b, g: (b * ngrp + g, 3)),
            pl.BlockSpec((Lc, NA_W), lambda b, g: (b, 1)),
            pl.BlockSpec((Lc, NA_W), lambda b, g: (b, 2)),
            _resident((1, NA_W), lambda b, g: (0, 0)),
            _resident((1, NA_W), lambda b, g: (0, 0)),
            _resident((NA_W, NA_W), lambda b, g: (0, 0)),
            _resident((NA_DR_NUM + 1, NA_HEADS, 2 * GRID_W), lambda b, g: (0, 0, 0)),
        ],
        out_specs=pl.BlockSpec((tq, NA_W), lambda b, g: (b * ngrp + g, 0)),
        out_shape=jax.ShapeDtypeStruct((B * L, NA_W), BF16),
        scratch_shapes=[
            pltpu.VMEM((L, NA_W), BF16), pltpu.VMEM((Lc, NA_W), BF16),
            pltpu.VMEM((L, NA_W), BF16), pltpu.VMEM((L, NA_W), BF16),
            pltpu.VMEM((Lc, NA_W), BF16), pltpu.VMEM((Lc, NA_W), BF16),
            pltpu.VMEM((NA_DR_NUM, NA_HEADS, GRID_W, 2 * GRID_W), F32),
            pltpu.VMEM((8, NA_WIN * GRID_W), F32),
            pltpu.VMEM((tq, NA_WIN * GRID_W + Lc), F32), pltpu.VMEM((tq, NA_WIN * GRID_W + Lc), F32),
            pltpu.VMEM((tq, NA_WIN * GRID_W + Lc), BF16), pltpu.VMEM((tq, NA_WIN * GRID_W + Lc), BF16),
            pltpu.VMEM((tq, 2 * NA_HEAD_DIM), F32), pltpu.VMEM((tq, 2 * NA_HEAD_DIM), F32),
            pltpu.VMEM((NA_HEADS, tq, 2 * NA_HEAD_DIM), BF16), pltpu.VMEM((tq, NA_W), F32),
        ],
        compiler_params=_cparams("arbitrary", "arbitrary"),
        name="neighborhood_attention",
    )(u, u, u, u, uc, uc, qg, kg, ones, rpb_t)


def _ctx_attn_kernel(q_ref, k_ref, v_ref, g_ref, qg_ref, kg_ref, ones_ref, o_ref):
    ones_bf = ones_ref[...]
    q = _head_rms(q_ref[...].astype(F32), ones_bf, qg_ref[...] * (NA_HEAD_DIM ** -0.5))
    k = _head_rms(k_ref[...].astype(F32), ones_bf, kg_ref[...]).astype(BF16)
    o = _attend_heads(q, [k], [v_ref[...]], lambda h, part: None)
    o_ref[...] = (o * _silu(g_ref[...].astype(F32))).astype(o_ref.dtype)


def _context_attention(uc, q_gain, k_gain, B, Lc):
    ones = jnp.asarray(_head_block_ones(NA_W)).astype(BF16)
    qg = jnp.tile(q_gain, NA_HEADS)[None, :]
    kg = jnp.tile(k_gain, NA_HEADS)[None, :]
    return pl.pallas_call(
        _ctx_attn_kernel,
        grid=(B,),
        in_specs=[
            pl.BlockSpec((Lc, NA_W), lambda b: (b, 0)),
            pl.BlockSpec((Lc, NA_W), lambda b: (b, 1)),
            pl.BlockSpec((Lc, NA_W), lambda b: (b, 2)),
            pl.BlockSpec((Lc, NA_W), lambda b: (b, 3)),
            _resident((1, NA_W), lambda b: (0, 0)),
            _resident((1, NA_W), lambda b: (0, 0)),
            _resident((NA_W, NA_W), lambda b: (0, 0)),
        ],
        out_specs=pl.BlockSpec((Lc, NA_W), lambda b: (b, 0)),
        out_shape=jax.ShapeDtypeStruct((B * Lc, NA_W), BF16),
        compiler_params=_cparams("parallel"),
        name="context_attention",
    )(uc, uc, uc, uc, qg, kg, ones)


def _alt_sign(shape):
    row = lax.broadcasted_iota(jnp.int32, shape, 0)
    return (1 - 2 * (row & 1)).astype(F32)


def _filter_kernel(z_ref, win_ref, w1_ref, b1_ref, w2_ref, b2_ref, w3_ref, sf_ref, c_ref, s_ref,
                   hr_ref, hi_ref, hn_ref, *, L):
    hp = lax.Precision.HIGHEST
    h = jnp.sin(sf_ref[0:1, :] * (jnp.dot(z_ref[...], w1_ref[...], preferred_element_type=F32, precision=hp) + b1_ref[...]))
    h = jnp.sin(sf_ref[1:2, :] * (jnp.dot(h, w2_ref[...], preferred_element_type=F32, precision=hp) + b2_ref[...]))
    win = win_ref[...]
    row = lax.broadcasted_iota(jnp.int32, (L, HY_W), 0)
    sign = _alt_sign((L, HY_W))
    inv_n = 1.0 / (2 * L)
    hf = jnp.dot(h, w3_ref[:, 0:HY_W], preferred_element_type=F32, precision=hp) * win
    hb = jnp.dot(h, w3_ref[:, HY_W:2 * HY_W], preferred_element_type=F32, precision=hp) * win
    hb = jnp.where(row == 0, 0.0, hb)
    norm = jnp.sum(jnp.abs(hf), axis=0, keepdims=True) + jnp.sum(jnp.abs(hb), axis=0, keepdims=True)
    inv = 1.0 / norm
    even = (hf + hb) * inv
    odd = (hf - hb) * inv
    e_hi = even.astype(BF16)
    e_lo = (even - e_hi.astype(F32)).astype(BF16)
    o_hi = odd.astype(BF16)
    o_lo = (odd - o_hi.astype(F32)).astype(BF16)
    wgt = jnp.where(row == 0, inv_n, 2.0 * inv_n)
    hr_ref[...] = (_dot(c_ref[...], e_hi) + _dot(c_ref[...], e_lo)) * wgt
    hi_ref[...] = -(_dot(s_ref[...], o_hi) + _dot(s_ref[...], o_lo)) * wgt
    hn_ref[...] = jnp.sum(even * sign, axis=0, keepdims=True) * inv_n


def _hyena_filter(L, w1, b1, w2, b2, w3, sin_freq, c_bf, s_bf):
    z, window = _filter_features(L)
    w1p = jnp.pad(w1, ((0, HY_EMB_PAD - HY_EMB), (0, 0)))
    full = lambda *shape: _resident(shape, lambda o: (0,) * len(shape))
    per_order = lambda rows: pl.BlockSpec((None, rows, HY_W), lambda o: (o, 0, 0))
    return pl.pallas_call(
        functools.partial(_filter_kernel, L=L),
        grid=(2,),
        in_specs=[
            full(L, HY_EMB_PAD), full(L, HY_W), full(HY_EMB_PAD, HY_FFN), full(1, HY_FFN),
            full(HY_FFN, HY_FFN), full(1, HY_FFN), pl.BlockSpec((HY_FFN, 2 * HY_W), lambda o: (0, o)),
            full(2, HY_FFN), full(L, L), full(L, L),
        ],
        out_specs=[per_order(L), per_order(L), per_order(1)],
        out_shape=[jax.ShapeDtypeStruct((2, L, HY_W), F32), jax.ShapeDtypeStruct((2, L, HY_W), F32),
                   jax.ShapeDtypeStruct((2, 1, HY_W), F32)],
        compiler_params=_cparams("parallel"),
        name="hyena_filter",
    )(jnp.asarray(z), jnp.asarray(window), w1p, b1[None, :], w2, b2[None, :], w3, sin_freq, c_bf, s_bf)


def _hyena_kernel(v_ref, x1_ref, x2_ref, g_ref, cw_ref, cb_ref, skip_ref, c_ref, s_ref, hr_ref, hi_ref, hn_ref,
                  o_ref, a_scr, abf_scr, x_scr, zr_scr, zi_scr, *, L):
    ck = min(HY_ROWS, L)
    halo = 16
    chunks = [(t0, t0 + ck) for t0 in range(0, L, ck)]
    sign = _alt_sign((ck, HY_W))

    def short_conv(ref, j, t0, t1):
        lo, hi = max(t0 - halo, 0), min(t1 + halo, L)
        u = ref[lo:hi, :].astype(F32)
        row = lo + lax.broadcasted_iota(jnp.int32, (hi - lo, HY_W), 0)
        prev = jnp.where(row == 0, 0.0, pltpu.roll(u, 1, 0))
        nxt = jnp.where(row == L - 1, 0.0, pltpu.roll(u, hi - lo - 1, 0))
        w = cw_ref[:, j * HY_W:(j + 1) * HY_W]
        z = prev * w[0:1, :] + u * w[1:2, :] + nxt * w[2:3, :] + cb_ref[:, j * HY_W:(j + 1) * HY_W]
        return z[t0 - lo:t1 - lo, :]

    def spectrum_product(order):
        for k0, k1 in chunks:
            ur = _dot(c_ref[k0:k1, :], abf_scr[...])
            us = _dot(s_ref[k0:k1, :], abf_scr[...])
            hr = hr_ref[order, k0:k1, :]
            hi = hi_ref[order, k0:k1, :]
            zr_scr[k0:k1, :] = (ur * hr + us * hi).astype(BF16)
            zi_scr[k0:k1, :] = (us * hr - ur * hi).astype(BF16)

    def long_conv_chunk(order, t0, t1, nyq):
        a = a_scr[t0:t1, :]
        y = _dot(c_ref[t0:t1, :], zr_scr[...]) + _dot(s_ref[t0:t1, :], zi_scr[...])
        return y + sign * nyq + a * skip_ref[order:order + 1, :]

    nyq = jnp.zeros((1, HY_W), F32)
    for t0, t1 in chunks:
        v = short_conv(v_ref, 0, t0, t1)
        a_scr[t0:t1, :] = v
        abf_scr[t0:t1, :] = v.astype(BF16)
        x_scr[t0:t1, :] = short_conv(x1_ref, 1, t0, t1)
        nyq += jnp.sum(v * sign, axis=0, keepdims=True)
    spectrum_product(0)
    nyq0 = nyq * hn_ref[0]
    nyq = jnp.zeros((1, HY_W), F32)
    for t0, t1 in chunks:
        y = x_scr[t0:t1, :] * long_conv_chunk(0, t0, t1, nyq0)
        a_scr[t0:t1, :] = y
        abf_scr[t0:t1, :] = y.astype(BF16)
        nyq += jnp.sum(y * sign, axis=0, keepdims=True)
    spectrum_product(1)
    nyq1 = nyq * hn_ref[1]
    for t0, t1 in chunks:
        y = short_conv(x2_ref, 2, t0, t1) * long_conv_chunk(1, t0, t1, nyq1)
        o_ref[t0:t1, :] = (y * _silu(g_ref[t0:t1, :].astype(F32))).astype(o_ref.dtype)


def _hyena(u, conv_w, conv_b, skip, c_bf, s_bf, hr, hi, hn, B, L):
    col = lambda j: pl.BlockSpec((L, CB), lambda b: (b, j))
    return pl.pallas_call(
        functools.partial(_hyena_kernel, L=L),
        grid=(B,),
        in_specs=[
            col(CB_HY_V), col(CB_HY_X1), col(CB_HY_X2), col(CB_HY_G),
            _resident((3, 3 * HY_W), lambda b: (0, 0)),
            _resident((1, 3 * HY_W), lambda b: (0, 0)),
            _resident((2, HY_W), lambda b: (0, 0)),
            _resident((L, L), lambda b: (0, 0)),
            _resident((L, L), lambda b: (0, 0)),
            _resident((2, L, HY_W), lambda b: (0, 0, 0)),
            _resident((2, L, HY_W), lambda b: (0, 0, 0)),
            _resident((2, 1, HY_W), lambda b: (0, 0, 0)),
        ],
        out_specs=pl.BlockSpec((L, HY_W), lambda b: (b, 0)),
        out_shape=jax.ShapeDtypeStruct((B * L, HY_W), BF16),
        scratch_shapes=[
            pltpu.VMEM((L, HY_W), F32), pltpu.VMEM((L, HY_W), BF16), pltpu.VMEM((L, HY_W), F32),
            pltpu.VMEM((L, HY_W), BF16), pltpu.VMEM((L, HY_W), BF16),
        ],
        compiler_params=_cparams("parallel"),
        name="hyena",
    )(u, u, u, u, conv_w, conv_b[None, :], skip, c_bf, s_bf, hr, hi, hn)


def _ret_kernel(*refs, L, Lc, has_init):
    if has_init:
        (q_ref, k_ref, v_ref, g_ref, kc_ref, vc_ref, cos_ref, sin_ref, rate_ref, rrow_ref, ones_ref,
         o_ref, q_scr, k_scr, sf_scr, sb_scr, r_scr) = refs
    else:
        (q_ref, k_ref, v_ref, g_ref, cos_ref, sin_ref, rate_ref, rrow_ref, ones_ref,
         o_ref, q_scr, k_scr, sf_scr, sb_scr, r_scr) = refs
    C = min(RET_CHUNK, L)
    nch = L // C
    W = RET_W
    quarter = RET_HEAD_DIM // 4
    block = ones_ref[...]
    lg = -jnp.exp(rate_ref[...])
    lg_f, lg_b = lg[0:1, :], lg[1:2, :]

    lane = lax.broadcasted_iota(jnp.int32, (L, W), 1)
    first_quarter = (lane % (2 * quarter)) < quarter

    def rope(a):
        swapped = jnp.where(first_quarter, pltpu.roll(a, W - quarter, 1), pltpu.roll(a, quarter, 1))
        return a * cos_ref[...] + swapped * sin_ref[...]

    if has_init:
        q_scr[...] = rope(q_ref[...].astype(F32)).astype(BF16)
        k_scr[...] = rope(k_ref[...].astype(F32) * (RET_HEAD_DIM ** -0.5)).astype(BF16)
    else:
        q_scr[...] = q_ref[...]
        k_scr[...] = (k_ref[...].astype(F32) * (RET_HEAD_DIM ** -0.5)).astype(BF16)

    def decays(n_rows):
        pos = lax.broadcasted_iota(jnp.int32, (n_rows, W), 0).astype(F32)
        return (jnp.exp(lg_f * (n_rows - 1.0 - pos)),
                jnp.exp(lg_b * pos))

    def chunk_states(k_bf, v_bf, zf, zb):
        kf = (k_bf.astype(F32) * zf).astype(BF16)
        kb = (k_bf.astype(F32) * zb).astype(BF16)
        return _dot_tn(kf, v_bf) * block, _dot_tn(kb, v_bf) * block

    zeta_f, zeta_b = decays(C)
    if has_init:
        zc_f, zc_b = decays(Lc)
        kc = (kc_ref[...].astype(F32) * (RET_HEAD_DIM ** -0.5)).astype(BF16)
        s0_f, s0_b = chunk_states(kc, vc_ref[...], zc_f, zc_b)
    else:
        s0_f = jnp.zeros((W, W), F32)
        s0_b = jnp.zeros((W, W), F32)

    for n in range(nch):
        kv_f, kv_b = chunk_states(k_scr[n * C:(n + 1) * C, :], v_ref[n * C:(n + 1) * C, :], zeta_f, zeta_b)
        sf_scr[n] = kv_f
        sb_scr[n] = kv_b
    dec_f = jnp.exp(lg_f * float(C))
    dec_b = jnp.exp(lg_b * float(C))
    state = s0_f
    for n in range(nch):
        kv = sf_scr[n]
        sf_scr[n] = state
        state = dec_f * state + kv
    state = s0_b
    for n in range(nch - 1, -1, -1):
        kv = sb_scr[n]
        sb_scr[n] = state
        state = dec_b * state + kv

    posc = lax.broadcasted_iota(jnp.int32, (C, W), 0).astype(F32)
    xi_f = jnp.exp(lg_f * (posc + 1.0))
    xi_b = jnp.exp(lg_b * (float(C) - posc))
    diff = (lax.broadcasted_iota(jnp.int32, (C, C), 0) - lax.broadcasted_iota(jnp.int32, (C, C), 1)).astype(F32)
    lane_c = lax.broadcasted_iota(jnp.int32, (1, W), 1)
    dmask = []
    for h in range(RET_HEADS):
        rf = -jnp.exp(rrow_ref[h:h + 1, 0:C])
        rb = -jnp.exp(rrow_ref[RET_HEADS + h:RET_HEADS + h + 1, 0:C])
        dmask.append(jnp.where(diff >= 0, jnp.exp(rf * jnp.maximum(diff, 0.0)), 0.0)
                     + jnp.where(diff <= 0, jnp.exp(rb * jnp.maximum(-diff, 0.0)), 0.0))

    for n in range(nch):
        qn = q_scr[n * C:(n + 1) * C, :]
        kn = k_scr[n * C:(n + 1) * C, :]
        vn = v_ref[n * C:(n + 1) * C, :]
        qf = qn.astype(F32)
        o = _dot((qf * xi_f).astype(BF16), sf_scr[n].astype(BF16))
        o += _dot((qf * xi_b).astype(BF16), sb_scr[n].astype(BF16))
        for h in range(RET_HEADS):
            sel = (lane_c // RET_HEAD_DIM) == h
            s = _dot_nt(jnp.where(sel, qn, jnp.zeros_like(qn)), kn) * dmask[h]
            o += jnp.where(sel, _dot(s.astype(BF16), vn), 0.0)
        r_scr[n * C:(n + 1) * C, :] = o

    r = r_scr[...]
    ss = _dot((r * r).astype(BF16), block.astype(BF16))
    rn = r * lax.rsqrt(ss * (1.0 / RET_HEAD_DIM) + EPS)
    o_ref[...] = (rn * _silu(g_ref[...].astype(F32))).astype(o_ref.dtype)


def _retention(u, uc, ret_log_rate, B, L, Lc, has_init):
    C = min(RET_CHUNK, L)
    nch = L // C
    col = lambda j: pl.BlockSpec((L, CB), lambda b: (b, j))
    ccol = lambda j: pl.BlockSpec((Lc, CB), lambda b: (b, j))
    cos, sin = _rope_tables(L)
    rate_lane = jnp.repeat(ret_log_rate, RET_HEAD_DIM, axis=1)
    rate_row = jnp.broadcast_to(ret_log_rate.reshape(2 * RET_HEADS, 1), (2 * RET_HEADS, RET_W))
    ones = jnp.asarray(_head_block_ones(RET_W))
    in_specs = [col(CB_RE_Q), col(CB_RE_K), col(CB_RE_V), col(CB_RE_G)]
    args = [u, u, u, u]
    if has_init:
        in_specs += [ccol(CB_RE_K), ccol(CB_RE_V)]
        args += [uc, uc]
    in_specs += [
        _resident((L, RET_W), lambda b: (0, 0)),
        _resident((L, RET_W), lambda b: (0, 0)),
        _resident((2, RET_W), lambda b: (0, 0)),
        _resident((2 * RET_HEADS, RET_W), lambda b: (0, 0)),
        _resident((RET_W, RET_W), lambda b: (0, 0)),
    ]
    args += [jnp.asarray(cos), jnp.asarray(sin), rate_lane, rate_row, ones]
    return pl.pallas_call(
        functools.partial(_ret_kernel, L=L, Lc=Lc, has_init=has_init),
        grid=(B,),
        in_specs=in_specs,
        out_specs=pl.BlockSpec((L, RET_W), lambda b: (b, 0)),
        out_shape=jax.ShapeDtypeStruct((B * L, RET_W), BF16),
        scratch_shapes=[
            pltpu.VMEM((L, RET_W), BF16), pltpu.VMEM((L, RET_W), BF16),
            pltpu.VMEM((nch, RET_W, RET_W), F32), pltpu.VMEM((nch, RET_W, RET_W), F32),
            pltpu.VMEM((L, RET_W), F32),
        ],
        compiler_params=_cparams("parallel"),
        name="retention",
    )(*args)


def kernel(x, c, ctx, c_ctx, norm_w, ada_w, ada_b, w_in, w_out, na_q_gain, na_k_gain, na_rpb, hy_conv_w, hy_conv_b,
           hy_w1, hy_b1, hy_w2, hy_b2, hy_w3, hy_sin_freq, hy_skip, ret_log_rate):
    B, L, D = x.shape
    Lc = ctx.shape[1]
    assert D == D_MODEL and L % (GRID_W * NA_G) == 0 and L // GRID_W >= NA_WIN

    n_cond = 16
    cc = jnp.concatenate([c, c_ctx[None, :], jnp.zeros((n_cond - B - 1, D), F32)], axis=0)
    mods = _modulation(cc, ada_w, ada_b)

    cx, sx = (jnp.asarray(a).astype(BF16) for a in _dft_matrices(L))
    ccx, scx = (jnp.asarray(a).astype(BF16) for a in _dft_matrices(Lc))

    tm_x = 512
    xf = x.reshape(B * L, D)
    cf = ctx.reshape(B * Lc, D)
    x_mod = lambda i: i // (L // tm_x)
    c_mod = lambda i: B

    for i in range(DEPTH):
        mod = mods[i].reshape(n_cond, 1, 3 * D)
        w_in_bf = w_in[i].astype(BF16)
        w_out_bf = w_out[i].astype(BF16)
        nw = norm_w[i][None, :]
        filt = (hy_w1[i], hy_b1[i], hy_w2[i], hy_b2[i], hy_w3[i], hy_sin_freq[i])

        u = _in_projection(xf, mod, x_mod, nw, w_in_bf, tm_x)
        uc = _in_projection(cf, mod, c_mod, nw, w_in_bf, Lc)

        za = _neighborhood_attention(u, uc, na_q_gain[i], na_k_gain[i], na_rpb[i], B, L, Lc)
        hr, hi, hn = _hyena_filter(L, *filt, cx, sx)
        zy = _hyena(u, hy_conv_w[i], hy_conv_b[i], hy_skip[i], cx, sx, hr, hi, hn, B, L)
        zr = _retention(u, uc, ret_log_rate[i], B, L, Lc, True)
        x_new = _out_projection(za, zy, zr, xf, mod, x_mod, w_out_bf, tm_x)

        if i < DEPTH - 1:
            zac = _context_attention(uc, na_q_gain[i], na_k_gain[i], B, Lc)
            hrc, hic, hnc = _hyena_filter(Lc, *filt, ccx, scx)
            zyc = _hyena(uc, hy_conv_w[i], hy_conv_b[i], hy_skip[i], ccx, scx, hrc, hic, hnc, B, Lc)
            zrc = _retention(uc, uc, ret_log_rate[i], B, Lc, Lc, False)
            cf = _out_projection(zac, zyc, zrc, cf, mod, c_mod, w_out_bf, Lc)
        xf = x_new

    return xf.reshape(B, L, D)
```

```python
import functools
import math

import numpy as np
import jax
import jax.numpy as jnp
from jax import lax
from jax.experimental import pallas as pl
from jax.experimental.pallas import tpu as pltpu

F32 = jnp.float32
BF16 = jnp.bfloat16

D_MODEL = 1024
DEPTH = 2
GRID_W = 64
NA_HEADS = 8
NA_HEAD_DIM = 64
NA_W = NA_HEADS * NA_HEAD_DIM
NA_KH = 8
NA_KW = 16
HY_W = 256
HY_BANDS = 8
HY_EMB = 1 + 2 * HY_BANDS
HY_EMB_PAD = 32
HY_FFN = 64
HY_FAST_DECAY = 0.3
HY_SLOW_DECAY = 1.5
HY_TARGET = 1e-2
RET_HEADS = 4
RET_HEAD_DIM = 64
RET_W = RET_HEADS * RET_HEAD_DIM
ROPE_BASE = 10000.0
EPS = 1e-6
NEG_INF = -1e30
LOG2E = 1.4426950408889634
IN_W = 4 * NA_W + 4 * HY_W + 4 * RET_W
MIX_W = NA_W + HY_W + RET_W

CB = 256
CB_HY_V, CB_HY_X1, CB_HY_X2, CB_HY_G = 8, 9, 10, 11
CB_RE_Q, CB_RE_K, CB_RE_V, CB_RE_G = 12, 13, 14, 15

NA_G = 4
NA_WIN = NA_G + NA_KH
NA_GPS = 2
NA_KT = 256
NA_DR_MIN = -4
NA_DR_NUM = 23
RET_CHUNK = 256
HY_ROWS = 512
VMEM_LIMIT = 56 * 1024 * 1024


def _cparams(*sem):
    return pltpu.CompilerParams(dimension_semantics=sem, vmem_limit_bytes=VMEM_LIMIT)


def _resident(shape, index_map):
    return pl.BlockSpec(shape, index_map, pipeline_mode=pl.Buffered(1))


def _silu(x):
    return x * (1.0 / (1.0 + jnp.exp(-x)))


def _dot(a, b):
    return jnp.dot(a, b, preferred_element_type=F32)


def _dot_nt(a, b):
    return lax.dot_general(a, b, (((1,), (1,)), ((), ())), preferred_element_type=F32)


def _dot_tn(a, b):
    return lax.dot_general(a, b, (((0,), (0,)), ((), ())), preferred_element_type=F32)


@functools.lru_cache(maxsize=None)
def _dft_matrices(L):
    kt = (np.arange(L, dtype=np.int64)[:, None] * np.arange(L, dtype=np.int64)[None, :]) % (2 * L)
    ang = kt.astype(np.float64) * (math.pi / L)
    return np.cos(ang).astype(np.float32), np.sin(ang).astype(np.float32)


@functools.lru_cache(maxsize=None)
def _filter_features(L):
    t = np.linspace(0.0, 1.0, L)[:, None]
    omega = 2.0 * math.pi * np.arange(L)[:, None] / L
    bands = np.linspace(1e-4, HY_BANDS - 1, HY_BANDS)[None, :]
    z = np.concatenate([t, np.cos(bands * omega), -np.sin(bands * omega)], axis=-1)
    z = np.pad(z, ((0, 0), (0, HY_EMB_PAD - HY_EMB)))
    deltas = np.abs(np.linspace(math.log(HY_TARGET) / HY_SLOW_DECAY, math.log(HY_TARGET) / HY_FAST_DECAY, HY_W))
    window = np.exp(-t * deltas[None, :])
    return z.astype(np.float32), window.astype(np.float32)


@functools.lru_cache(maxsize=None)
def _rope_tables(L):
    half = RET_HEAD_DIM // 2
    quarter = half // 2
    t = np.arange(L)
    pos = np.stack([t // GRID_W, t % GRID_W], axis=0).astype(np.float64)
    freqs = ROPE_BASE ** (-np.arange(quarter, dtype=np.float64) / quarter)
    cos = np.zeros((L, RET_HEAD_DIM))
    sin = np.zeros((L, RET_HEAD_DIM))
    for a in range(2):
        ang = pos[a][:, None] * freqs[None, :]
        base = a * half
        cos[:, base:base + quarter] = np.cos(ang)
        cos[:, base + quarter:base + half] = np.cos(ang)
        sin[:, base:base + quarter] = -np.sin(ang)
        sin[:, base + quarter:base + half] = np.sin(ang)
    return (np.tile(cos, (1, RET_HEADS)).astype(np.float32), np.tile(sin, (1, RET_HEADS)).astype(np.float32))


@functools.lru_cache(maxsize=None)
def _head_block_ones(width):
    i = np.arange(width) // NA_HEAD_DIM
    return (i[:, None] == i[None, :]).astype(np.float32)


def _mod_kernel(c_ref, w_ref, b_ref, o_ref):
    s = _silu(c_ref[...])
    o_ref[...] = jnp.dot(s, w_ref[...], preferred_element_type=F32, precision=lax.Precision.HIGHEST) + b_ref[...]


def _modulation(cc, ada_w, ada_b):
    R = cc.shape[0]
    tn = 1024
    return pl.pallas_call(
        _mod_kernel,
        grid=(DEPTH, 3 * D_MODEL // tn),
        in_specs=[
            pl.BlockSpec((R, D_MODEL), lambda i, j: (0, 0)),
            pl.BlockSpec((None, D_MODEL, tn), lambda i, j: (i, 0, j)),
            pl.BlockSpec((None, 1, tn), lambda i, j: (i, 0, j)),
        ],
        out_specs=pl.BlockSpec((None, R, tn), lambda i, j: (i, 0, j)),
        out_shape=jax.ShapeDtypeStruct((DEPTH, R, 3 * D_MODEL), F32),
        compiler_params=_cparams("parallel", "parallel"),
        name="modulation",
    )(cc, ada_w, ada_b.reshape(DEPTH, 1, 3 * D_MODEL))


def _inproj_kernel(x_ref, mod_ref, nw_ref, w_ref, o_ref):
    x = x_ref[...]
    xn = x * lax.rsqrt(jnp.mean(x * x, axis=-1, keepdims=True) + EPS)
    shift = mod_ref[:, 0:D_MODEL]
    scale = mod_ref[:, D_MODEL:2 * D_MODEL]
    h = (xn * nw_ref[...] * (1.0 + scale) + shift).astype(BF16)
    tn = 1024
    for j in range(IN_W // tn):
        o_ref[:, j * tn:(j + 1) * tn] = _dot(h, w_ref[:, j * tn:(j + 1) * tn]).astype(o_ref.dtype)


def _in_projection(xf, mod, mod_index, norm_w, w_bf, tm):
    R = xf.shape[0]
    return pl.pallas_call(
        _inproj_kernel,
        grid=(R // tm,),
        in_specs=[
            pl.BlockSpec((tm, D_MODEL), lambda i: (i, 0)),
            pl.BlockSpec((None, 1, 3 * D_MODEL), lambda i: (mod_index(i), 0, 0)),
            _resident((1, D_MODEL), lambda i: (0, 0)),
            _resident((D_MODEL, IN_W), lambda i: (0, 0)),
        ],
        out_specs=pl.BlockSpec((tm, IN_W), lambda i: (i, 0)),
        out_shape=jax.ShapeDtypeStruct((R, IN_W), BF16),
        compiler_params=_cparams("parallel"),
        name="in_projection",
    )(xf, mod, norm_w, w_bf)


def _outproj_kernel(za_ref, zy_ref, zr_ref, x_ref, mod_ref, w_ref, o_ref):
    acc = _dot(za_ref[...], w_ref[0:NA_W, :])
    acc += _dot(zy_ref[...], w_ref[NA_W:NA_W + HY_W, :])
    acc += _dot(zr_ref[...], w_ref[NA_W + HY_W:MIX_W, :])
    gate = mod_ref[:, 2 * D_MODEL:3 * D_MODEL]
    o_ref[...] = x_ref[...] + gate * acc


def _out_projection(za, zy, zr, xf, mod, mod_index, w_bf, tm):
    R = xf.shape[0]
    return pl.pallas_call(
        _outproj_kernel,
        grid=(R // tm,),
        in_specs=[
            pl.BlockSpec((tm, NA_W), lambda i: (i, 0)),
            pl.BlockSpec((tm, HY_W), lambda i: (i, 0)),
            pl.BlockSpec((tm, RET_W), lambda i: (i, 0)),
            pl.BlockSpec((tm, D_MODEL), lambda i: (i, 0)),
            pl.BlockSpec((None, 1, 3 * D_MODEL), lambda i: (mod_index(i), 0, 0)),
            _resident((MIX_W, D_MODEL), lambda i: (0, 0)),
        ],
        out_specs=pl.BlockSpec((tm, D_MODEL), lambda i: (i, 0)),
        out_shape=jax.ShapeDtypeStruct((R, D_MODEL), F32),
        compiler_params=_cparams("parallel"),
        name="out_projection",
    )(za, zy, zr, xf, mod, w_bf)


def _head_rms(x, ones_bf, gain):
    ss = _dot((x * x).astype(BF16), ones_bf)
    return x * lax.rsqrt(ss * (1.0 / NA_HEAD_DIM) + EPS) * gain


def _pair_masks():
    lane = lax.broadcasted_iota(jnp.int32, (1, 2 * NA_HEAD_DIM), 1)
    return lane < NA_HEAD_DIM


def _attend_heads(q, key_parts, val_parts, bias_fn):
    first = _pair_masks()
    outs = []
    for pair in range(NA_HEADS // 2):
        lo = pair * 2 * NA_HEAD_DIM
        hi = lo + 2 * NA_HEAD_DIM
        qp = q[:, lo:hi]
        o_pair = None
        for sub in range(2):
            h = 2 * pair + sub
            sel = first if sub == 0 else jnp.logical_not(first)
            qm = jnp.where(sel, qp, 0.0).astype(BF16)
            scores = []
            for part, kp in enumerate(key_parts):
                s = _dot_nt(qm, kp[:, lo:hi])
                b = bias_fn(h, part)
                if b is not None:
                    s = s + b
                scores.append(s)
            m = scores[0].max(axis=-1, keepdims=True)
            for s in scores[1:]:
                m = jnp.maximum(m, s.max(axis=-1, keepdims=True))
            denom = None
            acc = None
            for s, vp in zip(scores, val_parts):
                p = jnp.exp(s - m)
                ps = p.sum(axis=-1, keepdims=True)
                denom = ps if denom is None else denom + ps
                pv = _dot(p.astype(BF16), vp[:, lo:hi])
                acc = pv if acc is None else acc + pv
            o_h = acc * (1.0 / denom)
            o_pair = o_h if o_pair is None else jnp.where(first, o_pair, o_h)
        outs.append(o_pair)
    return jnp.concatenate(outs, axis=-1)


def _na_kernel(q_ref, k_ref, v_ref, g_ref, kc_ref, vc_ref, qg_ref, kg_ref, ones_ref, rpb_ref, o_ref,
               kn_scr, kcn_scr, vat_scr, vbt_scr, vcat_scr, vcbt_scr, bias_scr, mask_scr, s0_scr, s1_scr,
               p0_scr, p1_scr, qm_scr, ot_scr, *, rows):
    batch = pl.program_id(0)
    grp = pl.program_id(1)
    ones_bf = ones_ref[...]
    tq = NA_G * GRID_W
    nwin = NA_WIN * GRID_W
    Lc = kc_ref.shape[0]
    pair_w = 2 * NA_HEAD_DIM
    kt = NA_KT
    first = lax.broadcasted_iota(jnp.int32, (1, pair_w), 1) < NA_HEAD_DIM

    @pl.when((batch == 0) & (grp == 0))
    def _():
        ck = lax.broadcasted_iota(jnp.int32, (GRID_W, pair_w), 0)
        lane = lax.broadcasted_iota(jnp.int32, (GRID_W, pair_w), 1)
        cq = lane % GRID_W
        col_start = jnp.clip(cq - NA_KW // 2, 0, GRID_W - NA_KW)
        col_ok = (ck >= col_start) & (ck < col_start + NA_KW)
        left = lane < GRID_W

        def body(i, carry):
            for h in range(NA_HEADS):
                v0 = jnp.broadcast_to(rpb_ref[i + 1, h:h + 1, :], (GRID_W, pair_w))
                v1 = jnp.broadcast_to(rpb_ref[i, h:h + 1, :], (GRID_W, pair_w))
                t0 = pltpu.roll(v0, pair_w - (NA_KW - 1), 1, stride=1, stride_axis=0)
                t1 = pltpu.roll(v1, GRID_W - (NA_KW - 1), 1, stride=1, stride_axis=0)
                bias_scr[i, h] = jnp.where(col_ok, jnp.where(left, t0, t1), NEG_INF)
            return carry

        lax.fori_loop(0, NA_DR_NUM, body, 0)

    @pl.when(grp == 0)
    def _():
        kgain = kg_ref[...]
        own_a = (lax.broadcasted_iota(jnp.int32, (NA_W, 1), 0) % pair_w) < NA_HEAD_DIM
        for src, dst, vat, vbt, vsrc in ((k_ref, kn_scr, vat_scr, vbt_scr, v_ref),
                                         (kc_ref, kcn_scr, vcat_scr, vcbt_scr, vc_ref)):
            for i in range(src.shape[0] // kt):
                sl = slice(i * kt, (i + 1) * kt)
                dst[sl, :] = _head_rms(src[sl, :].astype(F32), ones_bf, kgain).astype(BF16)
                vt = vsrc[sl, :].astype(F32).T
                vat[i] = jnp.where(own_a, vt, 1.0).astype(BF16)
                vbt[i] = jnp.where(own_a, 1.0, vt).astype(BF16)

    n_kt = (nwin + Lc) // kt
    rows_per_tile = kt // GRID_W
    n_ch = (nwin + Lc) // GRID_W
    n_items = NA_GPS * NA_HEADS

    s_bufs = (s0_scr, s1_scr)
    p_bufs = (p0_scr, p1_scr)
    row_zero = jnp.minimum(grp, 0)

    def rows_of(start, size):
        return pl.ds(pl.multiple_of(row_zero + start, GRID_W), size)

    def head_lanes(h):
        return slice((h // 2) * pair_w, (h // 2 + 1) * pair_w)

    key0, tile0, dr_base = [], [], []
    for gi in range(NA_GPS):
        r0 = (grp * NA_GPS + gi) * NA_G
        ws = jnp.clip(r0 - NA_KH // 2, 0, rows - NA_WIN)
        key0.append(pl.multiple_of(ws * GRID_W, kt))
        tile0.append(ws // rows_per_tile)
        dr_base.append(ws - r0 + (NA_KH - 1) - NA_DR_MIN)
        w_id = lax.broadcasted_iota(jnp.int32, (16, tq), 0)
        g_id = lax.broadcasted_iota(jnp.int32, (16, tq), 1) // GRID_W
        lo = jnp.clip(r0 + g_id - NA_KH // 2, 0, rows - NA_KH) - ws
        mask_scr[gi] = jnp.where((w_id >= lo) & (w_id < lo + NA_KH), 0.0, NEG_INF).astype(F32)
        q = _head_rms(q_ref[gi * tq:(gi + 1) * tq, :].astype(F32), ones_bf,
                      qg_ref[...] * (NA_HEAD_DIM ** -0.5 * LOG2E))
        for h in range(NA_HEADS):
            sel = first if h % 2 == 0 else jnp.logical_not(first)
            qm_scr[gi * NA_HEADS + h, rows_of(0, tq), :] = jnp.where(sel, q[:, head_lanes(h)], 0.0).astype(BF16)

    col_max = {}

    def qk_piece(n, j):
        gi, h = divmod(n, NA_HEADS)
        last = j == n_kt - 1
        keys = kcn_scr[:, head_lanes(h)] if last else kn_scr[pl.ds(key0[gi] + j * kt, kt), head_lanes(h)]
        sv = _dot_nt(keys, qm_scr[n, rows_of(0, tq), :])
        for dw in range(rows_per_tile):
            sw = sv[dw * GRID_W:(dw + 1) * GRID_W, :]
            if not last:
                w = j * rows_per_tile + dw
                bias = jnp.concatenate([bias_scr[dr_base[gi] + (w - g), h] for g in range(0, NA_G, 2)], axis=-1)
                sw = sw + bias + mask_scr[gi, w:w + 1, :]
            s_bufs[n % 2][rows_of(j * kt + dw * GRID_W, GRID_W), :] = sw
            part = sw.reshape(GRID_W // 8, 8, tq).max(axis=0)
            col_max[n] = part if (j == 0 and dw == 0) else jnp.maximum(col_max[n], part)

    def exp_chunk(n, c):
        if c == 0:
            col_max[n] = col_max[n].max(axis=0, keepdims=True)
        rows_c = rows_of(c * GRID_W, GRID_W)
        p_bufs[n % 2][rows_c, :] = jnp.exp2(s_bufs[n % 2][rows_c, :] - col_max[n]).astype(BF16)
        if c == n_ch - 1:
            del col_max[n]

    def pv(n):
        gi, h = divmod(n, NA_HEADS)
        vt, vct = (vat_scr, vcat_scr) if h % 2 == 0 else (vbt_scr, vcbt_scr)
        ot = _dot(vct[0, head_lanes(h), :], p_bufs[n % 2][rows_of(nwin, Lc), :])
        for j in range(n_kt - 1):
            ot += _dot(vt[tile0[gi] + j, head_lanes(h), :], p_bufs[n % 2][rows_of(j * kt, kt), :])
        own, den = (ot[0:NA_HEAD_DIM], ot[NA_HEAD_DIM:]) if h % 2 == 0 else (ot[NA_HEAD_DIM:], ot[0:NA_HEAD_DIM])
        ot_scr[rows_of(n * NA_HEAD_DIM, NA_HEAD_DIM), :] = own * (1.0 / den)
        if h % 2 == 1:
            o_pair = ot_scr[rows_of((n - 1) * NA_HEAD_DIM, pair_w), :].T
            q_rows = slice(gi * tq, (gi + 1) * tq)
            gate = _silu(g_ref[q_rows, head_lanes(h)].astype(F32))
            o_ref[q_rows, head_lanes(h)] = (o_pair * gate).astype(o_ref.dtype)

    for t in range(n_items + 2):
        for i in range(n_ch):
            if 0 <= t - 1 < n_items:
                exp_chunk(t - 1, i)
            if i % 4 == 0 and t < n_items and i // 4 < n_kt:
                qk_piece(t, i // 4)
            if i == 2 and 0 <= t - 2:
                pv(t - 2)


def _neighborhood_attention(u, uc, q_gain, k_gain, rpb, B, L, Lc):
    rows = L // GRID_W
    ngrp = rows // (NA_G * NA_GPS)
    tq = NA_G * GRID_W
    tstep = tq * NA_GPS
    ones = jnp.asarray(_head_block_ones(NA_W)).astype(BF16)
    qg = jnp.tile(q_gain, NA_HEADS)[None, :]
    kg = jnp.tile(k_gain, NA_HEADS)[None, :]
    n_dr = 2 * NA_KH - 1
    rpb_t = jnp.transpose(rpb.astype(F32)[:, :, ::-1] * LOG2E, (1, 0, 2))
    lo_pad = 1 - NA_DR_MIN
    rpb_t = jnp.pad(rpb_t, ((lo_pad, NA_DR_NUM + 1 - lo_pad - n_dr), (0, 0), (0, 2 * GRID_W - (2 * NA_KW - 1))))
    return pl.pallas_call(
        functools.partial(_na_kernel, rows=rows),
        grid=(B, ngrp),
        in_specs=[
            pl.BlockSpec((tstep, NA_W), lambda b, g: (b * ngrp + g, 0)),
            pl.BlockSpec((L, NA_W), lambda b, g: (b, 1)),
            pl.BlockSpec((L, NA_W), lambda b, g: (b, 2)),
            pl.BlockSpec((tstep, NA_W), lambda b, g: (b * ngrp + g, 3)),
            pl.BlockSpec((Lc, NA_W), lambda b, g: (b, 1)),
            pl.BlockSpec((Lc, NA_W), lambda b, g: (b, 2)),
            _resident((1, NA_W), lambda b, g: (0, 0)),
            _resident((1, NA_W), lambda b, g: (0, 0)),
            _resident((NA_W, NA_W), lambda b, g: (0, 0)),
            _resident((NA_DR_NUM + 1, NA_HEADS, 2 * GRID_W), lambda b, g: (0, 0, 0)),
        ],
        out_specs=pl.BlockSpec((tstep, NA_W), lambda b, g: (b * ngrp + g, 0)),
        out_shape=jax.ShapeDtypeStruct((B * L, NA_W), BF16),
        scratch_shapes=[
            pltpu.VMEM((L, NA_W), BF16), pltpu.VMEM((Lc, NA_W), BF16),
            pltpu.VMEM((L // NA_KT, NA_W, NA_KT), BF16), pltpu.VMEM((L // NA_KT, NA_W, NA_KT), BF16),
            pltpu.VMEM((Lc // NA_KT, NA_W, NA_KT), BF16), pltpu.VMEM((Lc // NA_KT, NA_W, NA_KT), BF16),
            pltpu.VMEM((NA_DR_NUM, NA_HEADS, GRID_W, 2 * GRID_W), F32),
            pltpu.VMEM((NA_GPS, 16, tq), F32),
            pltpu.VMEM((NA_WIN * GRID_W + Lc, tq), F32), pltpu.VMEM((NA_WIN * GRID_W + Lc, tq), F32),
            pltpu.VMEM((NA_WIN * GRID_W + Lc, tq), BF16), pltpu.VMEM((NA_WIN * GRID_W + Lc, tq), BF16),
            pltpu.VMEM((NA_GPS * NA_HEADS, tq, 2 * NA_HEAD_DIM), BF16), pltpu.VMEM((NA_GPS * NA_W, tq), F32),
        ],
        compiler_params=_cparams("arbitrary", "arbitrary"),
        name="neighborhood_attention",
    )(u, u, u, u, uc, uc, qg, kg, ones, rpb_t)


def _ctx_attn_kernel(q_ref, k_ref, v_ref, g_ref, qg_ref, kg_ref, ones_ref, o_ref):
    ones_bf = ones_ref[...]
    q = _head_rms(q_ref[...].astype(F32), ones_bf, qg_ref[...] * (NA_HEAD_DIM ** -0.5))
    k = _head_rms(k_ref[...].astype(F32), ones_bf, kg_ref[...]).astype(BF16)
    o = _attend_heads(q, [k], [v_ref[...]], lambda h, part: None)
    o_ref[...] = (o * _silu(g_ref[...].astype(F32))).astype(o_ref.dtype)


def _context_attention(uc, q_gain, k_gain, B, Lc):
    ones = jnp.asarray(_head_block_ones(NA_W)).astype(BF16)
    qg = jnp.tile(q_gain, NA_HEADS)[None, :]
    kg = jnp.tile(k_gain, NA_HEADS)[None, :]
    return pl.pallas_call(
        _ctx_attn_kernel,
        grid=(B,),
        in_specs=[
            pl.BlockSpec((Lc, NA_W), lambda b: (b, 0)),
            pl.BlockSpec((Lc, NA_W), lambda b: (b, 1)),
            pl.BlockSpec((Lc, NA_W), lambda b: (b, 2)),
            pl.BlockSpec((Lc, NA_W), lambda b: (b, 3)),
            _resident((1, NA_W), lambda b: (0, 0)),
            _resident((1, NA_W), lambda b: (0, 0)),
            _resident((NA_W, NA_W), lambda b: (0, 0)),
        ],
        out_specs=pl.BlockSpec((Lc, NA_W), lambda b: (b, 0)),
        out_shape=jax.ShapeDtypeStruct((B * Lc, NA_W), BF16),
        compiler_params=_cparams("parallel"),
        name="context_attention",
    )(uc, uc, uc, uc, qg, kg, ones)


def _alt_sign(shape):
    row = lax.broadcasted_iota(jnp.int32, shape, 0)
    return (1 - 2 * (row & 1)).astype(F32)


def _filter_kernel(z_ref, win_ref, w1_ref, b1_ref, w2_ref, b2_ref, w3_ref, sf_ref, c_ref, s_ref,
                   hr_ref, hi_ref, hn_ref, *, L):
    hp = lax.Precision.HIGHEST
    h = jnp.sin(sf_ref[0:1, :] * (jnp.dot(z_ref[...], w1_ref[...], preferred_element_type=F32, precision=hp) + b1_ref[...]))
    h = jnp.sin(sf_ref[1:2, :] * (jnp.dot(h, w2_ref[...], preferred_element_type=F32, precision=hp) + b2_ref[...]))
    win = win_ref[...]
    row = lax.broadcasted_iota(jnp.int32, (L, HY_W), 0)
    sign = _alt_sign((L, HY_W))
    inv_n = 1.0 / (2 * L)
    hf = jnp.dot(h, w3_ref[:, 0:HY_W], preferred_element_type=F32, precision=hp) * win
    hb = jnp.dot(h, w3_ref[:, HY_W:2 * HY_W], preferred_element_type=F32, precision=hp) * win
    hb = jnp.where(row == 0, 0.0, hb)
    norm = jnp.sum(jnp.abs(hf), axis=0, keepdims=True) + jnp.sum(jnp.abs(hb), axis=0, keepdims=True)
    inv = 1.0 / norm
    even = (hf + hb) * inv
    odd = (hf - hb) * inv
    e_hi = even.astype(BF16)
    e_lo = (even - e_hi.astype(F32)).astype(BF16)
    o_hi = odd.astype(BF16)
    o_lo = (odd - o_hi.astype(F32)).astype(BF16)
    wgt = jnp.where(row == 0, inv_n, 2.0 * inv_n)
    hr_ref[...] = (_dot(c_ref[...], e_hi) + _dot(c_ref[...], e_lo)) * wgt
    hi_ref[...] = -(_dot(s_ref[...], o_hi) + _dot(s_ref[...], o_lo)) * wgt
    hn_ref[...] = jnp.sum(even * sign, axis=0, keepdims=True) * inv_n


def _hyena_filter(L, w1, b1, w2, b2, w3, sin_freq, c_bf, s_bf):
    z, window = _filter_features(L)
    w1p = jnp.pad(w1, ((0, HY_EMB_PAD - HY_EMB), (0, 0)))
    full = lambda *shape: _resident(shape, lambda o: (0,) * len(shape))
    per_order = lambda rows: pl.BlockSpec((None, rows, HY_W), lambda o: (o, 0, 0))
    return pl.pallas_call(
        functools.partial(_filter_kernel, L=L),
        grid=(2,),
        in_specs=[
            full(L, HY_EMB_PAD), full(L, HY_W), full(HY_EMB_PAD, HY_FFN), full(1, HY_FFN),
            full(HY_FFN, HY_FFN), full(1, HY_FFN), pl.BlockSpec((HY_FFN, 2 * HY_W), lambda o: (0, o)),
            full(2, HY_FFN), full(L, L), full(L, L),
        ],
        out_specs=[per_order(L), per_order(L), per_order(1)],
        out_shape=[jax.ShapeDtypeStruct((2, L, HY_W), F32), jax.ShapeDtypeStruct((2, L, HY_W), F32),
                   jax.ShapeDtypeStruct((2, 1, HY_W), F32)],
        compiler_params=_cparams("parallel"),
        name="hyena_filter",
    )(jnp.asarray(z), jnp.asarray(window), w1p, b1[None, :], w2, b2[None, :], w3, sin_freq, c_bf, s_bf)


def _hyena_kernel(v_ref, x1_ref, x2_ref, g_ref, cw_ref, cb_ref, skip_ref, c_ref, s_ref, hr_ref, hi_ref, hn_ref,
                  o_ref, a_scr, abf_scr, x_scr, zr_scr, zi_scr, *, L):
    ck = min(HY_ROWS, L)
    halo = 16
    chunks = [(t0, t0 + ck) for t0 in range(0, L, ck)]
    sign = _alt_sign((ck, HY_W))

    def short_conv(ref, j, t0, t1):
        lo, hi = max(t0 - halo, 0), min(t1 + halo, L)
        u = ref[lo:hi, :].astype(F32)
        row = lo + lax.broadcasted_iota(jnp.int32, (hi - lo, HY_W), 0)
        prev = jnp.where(row == 0, 0.0, pltpu.roll(u, 1, 0))
        nxt = jnp.where(row == L - 1, 0.0, pltpu.roll(u, hi - lo - 1, 0))
        w = cw_ref[:, j * HY_W:(j + 1) * HY_W]
        z = prev * w[0:1, :] + u * w[1:2, :] + nxt * w[2:3, :] + cb_ref[:, j * HY_W:(j + 1) * HY_W]
        return z[t0 - lo:t1 - lo, :]

    def spectrum_product(order):
        for k0, k1 in chunks:
            ur = _dot(c_ref[k0:k1, :], abf_scr[...])
            us = _dot(s_ref[k0:k1, :], abf_scr[...])
            hr = hr_ref[order, k0:k1, :]
            hi = hi_ref[order, k0:k1, :]
            zr_scr[k0:k1, :] = (ur * hr + us * hi).astype(BF16)
            zi_scr[k0:k1, :] = (us * hr - ur * hi).astype(BF16)

    def long_conv_chunk(order, t0, t1, nyq):
        a = a_scr[t0:t1, :]
        y = _dot(c_ref[t0:t1, :], zr_scr[...]) + _dot(s_ref[t0:t1, :], zi_scr[...])
        return y + sign * nyq + a * skip_ref[order:order + 1, :]

    nyq = jnp.zeros((1, HY_W), F32)
    for t0, t1 in chunks:
        v = short_conv(v_ref, 0, t0, t1)
        a_scr[t0:t1, :] = v
        abf_scr[t0:t1, :] = v.astype(BF16)
        x_scr[t0:t1, :] = short_conv(x1_ref, 1, t0, t1)
        nyq += jnp.sum(v * sign, axis=0, keepdims=True)
    spectrum_product(0)
    nyq0 = nyq * hn_ref[0]
    nyq = jnp.zeros((1, HY_W), F32)
    for t0, t1 in chunks:
        y = x_scr[t0:t1, :] * long_conv_chunk(0, t0, t1, nyq0)
        a_scr[t0:t1, :] = y
        abf_scr[t0:t1, :] = y.astype(BF16)
        nyq += jnp.sum(y * sign, axis=0, keepdims=True)
    spectrum_product(1)
    nyq1 = nyq * hn_ref[1]
    for t0, t1 in chunks:
        y = short_conv(x2_ref, 2, t0, t1) * long_conv_chunk(1, t0, t1, nyq1)
        o_ref[t0:t1, :] = (y * _silu(g_ref[t0:t1, :].astype(F32))).astype(o_ref.dtype)


def _hyena(u, conv_w, conv_b, skip, c_bf, s_bf, hr, hi, hn, B, L):
    col = lambda j: pl.BlockSpec((L, CB), lambda b: (b, j))
    return pl.pallas_call(
        functools.partial(_hyena_kernel, L=L),
        grid=(B,),
        in_specs=[
            col(CB_HY_V), col(CB_HY_X1), col(CB_HY_X2), col(CB_HY_G),
            _resident((3, 3 * HY_W), lambda b: (0, 0)),
            _resident((1, 3 * HY_W), lambda b: (0, 0)),
            _resident((2, HY_W), lambda b: (0, 0)),
            _resident((L, L), lambda b: (0, 0)),
            _resident((L, L), lambda b: (0, 0)),
            _resident((2, L, HY_W), lambda b: (0, 0, 0)),
            _resident((2, L, HY_W), lambda b: (0, 0, 0)),
            _resident((2, 1, HY_W), lambda b: (0, 0, 0)),
        ],
        out_specs=pl.BlockSpec((L, HY_W), lambda b: (b, 0)),
        out_shape=jax.ShapeDtypeStruct((B * L, HY_W), BF16),
        scratch_shapes=[
            pltpu.VMEM((L, HY_W), F32), pltpu.VMEM((L, HY_W), BF16), pltpu.VMEM((L, HY_W), F32),
            pltpu.VMEM((L, HY_W), BF16), pltpu.VMEM((L, HY_W), BF16),
        ],
        compiler_params=_cparams("parallel"),
        name="hyena",
    )(u, u, u, u, conv_w, conv_b[None, :], skip, c_bf, s_bf, hr, hi, hn)


def _ret_kernel(*refs, L, Lc, has_init):
    if has_init:
        (q_ref, k_ref, v_ref, g_ref, kc_ref, vc_ref, cos_ref, sin_ref, rate_ref, rrow_ref, ones_ref,
         o_ref, q_scr, k_scr, sf_scr, sb_scr, r_scr) = refs
    else:
        (q_ref, k_ref, v_ref, g_ref, cos_ref, sin_ref, rate_ref, rrow_ref, ones_ref,
         o_ref, q_scr, k_scr, sf_scr, sb_scr, r_scr) = refs
    C = min(RET_CHUNK, L)
    nch = L // C
    W = RET_W
    quarter = RET_HEAD_DIM // 4
    block = ones_ref[...]
    lg = -jnp.exp(rate_ref[...])
    lg_f, lg_b = lg[0:1, :], lg[1:2, :]

    lane = lax.broadcasted_iota(jnp.int32, (L, W), 1)
    first_quarter = (lane % (2 * quarter)) < quarter

    def rope(a):
        swapped = jnp.where(first_quarter, pltpu.roll(a, W - quarter, 1), pltpu.roll(a, quarter, 1))
        return a * cos_ref[...] + swapped * sin_ref[...]

    if has_init:
        q_scr[...] = rope(q_ref[...].astype(F32)).astype(BF16)
        k_scr[...] = rope(k_ref[...].astype(F32) * (RET_HEAD_DIM ** -0.5)).astype(BF16)
    else:
        q_scr[...] = q_ref[...]
        k_scr[...] = (k_ref[...].astype(F32) * (RET_HEAD_DIM ** -0.5)).astype(BF16)

    def decays(n_rows):
        pos = lax.broadcasted_iota(jnp.int32, (n_rows, W), 0).astype(F32)
        return (jnp.exp(lg_f * (n_rows - 1.0 - pos)),
                jnp.exp(lg_b * pos))

    def chunk_states(k_bf, v_bf, zf, zb):
        kf = (k_bf.astype(F32) * zf).astype(BF16)
        kb = (k_bf.astype(F32) * zb).astype(BF16)
        return _dot_tn(kf, v_bf) * block, _dot_tn(kb, v_bf) * block

    zeta_f, zeta_b = decays(C)
    if has_init:
        zc_f, zc_b = decays(Lc)
        kc = (kc_ref[...].astype(F32) * (RET_HEAD_DIM ** -0.5)).astype(BF16)
        s0_f, s0_b = chunk_states(kc, vc_ref[...], zc_f, zc_b)
    else:
        s0_f = jnp.zeros((W, W), F32)
        s0_b = jnp.zeros((W, W), F32)

    for n in range(nch):
        kv_f, kv_b = chunk_states(k_scr[n * C:(n + 1) * C, :], v_ref[n * C:(n + 1) * C, :], zeta_f, zeta_b)
        sf_scr[n] = kv_f
        sb_scr[n] = kv_b
    dec_f = jnp.exp(lg_f * float(C))
    dec_b = jnp.exp(lg_b * float(C))
    state = s0_f
    for n in range(nch):
        kv = sf_scr[n]
        sf_scr[n] = state
        state = dec_f * state + kv
    state = s0_b
    for n in range(nch - 1, -1, -1):
        kv = sb_scr[n]
        sb_scr[n] = state
        state = dec_b * state + kv

    posc = lax.broadcasted_iota(jnp.int32, (C, W), 0).astype(F32)
    xi_f = jnp.exp(lg_f * (posc + 1.0))
    xi_b = jnp.exp(lg_b * (float(C) - posc))
    diff = (lax.broadcasted_iota(jnp.int32, (C, C), 0) - lax.broadcasted_iota(jnp.int32, (C, C), 1)).astype(F32)
    lane_c = lax.broadcasted_iota(jnp.int32, (1, W), 1)
    dmask = []
    for h in range(RET_HEADS):
        rf = -jnp.exp(rrow_ref[h:h + 1, 0:C])
        rb = -jnp.exp(rrow_ref[RET_HEADS + h:RET_HEADS + h + 1, 0:C])
        dmask.append(jnp.where(diff >= 0, jnp.exp(rf * jnp.maximum(diff, 0.0)), 0.0)
                     + jnp.where(diff <= 0, jnp.exp(rb * jnp.maximum(-diff, 0.0)), 0.0))

    for n in range(nch):
        qn = q_scr[n * C:(n + 1) * C, :]
        kn = k_scr[n * C:(n + 1) * C, :]
        vn = v_ref[n * C:(n + 1) * C, :]
        qf = qn.astype(F32)
        o = _dot((qf * xi_f).astype(BF16), sf_scr[n].astype(BF16))
        o += _dot((qf * xi_b).astype(BF16), sb_scr[n].astype(BF16))
        for h in range(RET_HEADS):
            sel = (lane_c // RET_HEAD_DIM) == h
            s = _dot_nt(jnp.where(sel, qn, jnp.zeros_like(qn)), kn) * dmask[h]
            o += jnp.where(sel, _dot(s.astype(BF16), vn), 0.0)
        r_scr[n * C:(n + 1) * C, :] = o

    r = r_scr[...]
    ss = _dot((r * r).astype(BF16), block.astype(BF16))
    rn = r * lax.rsqrt(ss * (1.0 / RET_HEAD_DIM) + EPS)
    o_ref[...] = (rn * _silu(g_ref[...].astype(F32))).astype(o_ref.dtype)


def _retention(u, uc, ret_log_rate, B, L, Lc, has_init):
    C = min(RET_CHUNK, L)
    nch = L // C
    col = lambda j: pl.BlockSpec((L, CB), lambda b: (b, j))
    ccol = lambda j: pl.BlockSpec((Lc, CB), lambda b: (b, j))
    cos, sin = _rope_tables(L)
    rate_lane = jnp.repeat(ret_log_rate, RET_HEAD_DIM, axis=1)
    rate_row = jnp.broadcast_to(ret_log_rate.reshape(2 * RET_HEADS, 1), (2 * RET_HEADS, RET_W))
    ones = jnp.asarray(_head_block_ones(RET_W))
    in_specs = [col(CB_RE_Q), col(CB_RE_K), col(CB_RE_V), col(CB_RE_G)]
    args = [u, u, u, u]
    if has_init:
        in_specs += [ccol(CB_RE_K), ccol(CB_RE_V)]
        args += [uc, uc]
    in_specs += [
        _resident((L, RET_W), lambda b: (0, 0)),
        _resident((L, RET_W), lambda b: (0, 0)),
        _resident((2, RET_W), lambda b: (0, 0)),
        _resident((2 * RET_HEADS, RET_W), lambda b: (0, 0)),
        _resident((RET_W, RET_W), lambda b: (0, 0)),
    ]
    args += [jnp.asarray(cos), jnp.asarray(sin), rate_lane, rate_row, ones]
    return pl.pallas_call(
        functools.partial(_ret_kernel, L=L, Lc=Lc, has_init=has_init),
        grid=(B,),
        in_specs=in_specs,
        out_specs=pl.BlockSpec((L, RET_W), lambda b: (b, 0)),
        out_shape=jax.ShapeDtypeStruct((B * L, RET_W), BF16),
        scratch_shapes=[
            pltpu.VMEM((L, RET_W), BF16), pltpu.VMEM((L, RET_W), BF16),
            pltpu.VMEM((nch, RET_W, RET_W), F32), pltpu.VMEM((nch, RET_W, RET_W), F32),
            pltpu.VMEM((L, RET_W), F32),
        ],
        compiler_params=_cparams("parallel"),
        name="retention",
    )(*args)


def kernel(x, c, ctx, c_ctx, norm_w, ada_w, ada_b, w_in, w_out, na_q_gain, na_k_gain, na_rpb, hy_conv_w, hy_conv_b,
           hy_w1, hy_b1, hy_w2, hy_b2, hy_w3, hy_sin_freq, hy_skip, ret_log_rate):
    B, L, D = x.shape
    Lc = ctx.shape[1]
    assert D == D_MODEL and L % (GRID_W * NA_G * NA_GPS) == 0 and L // GRID_W >= NA_WIN
    assert NA_G == NA_KT // GRID_W and (L // GRID_W - NA_WIN) % NA_G == 0 and Lc == NA_KT

    n_cond = 16
    cc = jnp.concatenate([c, c_ctx[None, :], jnp.zeros((n_cond - B - 1, D), F32)], axis=0)
    mods = _modulation(cc, ada_w, ada_b)

    cx, sx = (jnp.asarray(a).astype(BF16) for a in _dft_matrices(L))
    ccx, scx = (jnp.asarray(a).astype(BF16) for a in _dft_matrices(Lc))

    tm_x = 512
    xf = x.reshape(B * L, D)
    cf = ctx.reshape(B * Lc, D)
    x_mod = lambda i: i // (L // tm_x)
    c_mod = lambda i: B

    for i in range(DEPTH):
        mod = mods[i].reshape(n_cond, 1, 3 * D)
        w_in_bf = w_in[i].astype(BF16)
        w_out_bf = w_out[i].astype(BF16)
        nw = norm_w[i][None, :]
        filt = (hy_w1[i], hy_b1[i], hy_w2[i], hy_b2[i], hy_w3[i], hy_sin_freq[i])

        u = _in_projection(xf, mod, x_mod, nw, w_in_bf, tm_x)
        uc = _in_projection(cf, mod, c_mod, nw, w_in_bf, Lc)

        za = _neighborhood_attention(u, uc, na_q_gain[i], na_k_gain[i], na_rpb[i], B, L, Lc)
        hr, hi, hn = _hyena_filter(L, *filt, cx, sx)
        zy = _hyena(u, hy_conv_w[i], hy_conv_b[i], hy_skip[i], cx, sx, hr, hi, hn, B, L)
        zr = _retention(u, uc, ret_log_rate[i], B, L, Lc, True)
        x_new = _out_projection(za, zy, zr, xf, mod, x_mod, w_out_bf, tm_x)

        if i < DEPTH - 1:
            zac = _context_attention(uc, na_q_gain[i], na_k_gain[i], B, Lc)
            hrc, hic, hnc = _hyena_filter(Lc, *filt, ccx, scx)
            zyc = _hyena(uc, hy_conv_w[i], hy_conv_b[i], hy_skip[i], ccx, scx, hrc, hic, hnc, B, Lc)
            zrc = _retention(uc, uc, ret_log_rate[i], B, Lc, Lc, False)
            cf = _out_projection(zac, zyc, zrc, cf, mod, c_mod, w_out_bf, Lc)
        xf = x_new

    return xf.reshape(B, L, D)
```

```python
import functools
import math

import numpy as np
import jax
import jax.numpy as jnp
from jax import lax
from jax.experimental import pallas as pl
from jax.experimental.pallas import tpu as pltpu

F32 = jnp.float32
BF16 = jnp.bfloat16

D_MODEL = 1024
DEPTH = 2
GRID_W = 64
NA_HEADS = 8
NA_HEAD_DIM = 64
NA_W = NA_HEADS * NA_HEAD_DIM
NA_KH = 8
NA_KW = 16
HY_W = 256
HY_BANDS = 8
HY_EMB = 1 + 2 * HY_BANDS
HY_EMB_PAD = 32
HY_FFN = 64
HY_FAST_DECAY = 0.3
HY_SLOW_DECAY = 1.5
HY_TARGET = 1e-2
RET_HEADS = 4
RET_HEAD_DIM = 64
RET_W = RET_HEADS * RET_HEAD_DIM
ROPE_BASE = 10000.0
EPS = 1e-6
NEG_INF = -1e30
LOG2E = 1.4426950408889634
IN_W = 4 * NA_W + 4 * HY_W + 4 * RET_W
MIX_W = NA_W + HY_W + RET_W

CB = 256
CB_HY_V, CB_HY_X1, CB_HY_X2, CB_HY_G = 8, 9, 10, 11
CB_RE_Q, CB_RE_K, CB_RE_V, CB_RE_G = 12, 13, 14, 15

NA_G = 4
NA_WIN = NA_G + NA_KH
NA_GPS = 4
NA_KT = 256
NA_DR_MIN = -4
NA_DR_NUM = 23
RET_CHUNK = 256
HY_ROWS = 512
VMEM_LIMIT = 56 * 1024 * 1024


def _cparams(*sem):
    return pltpu.CompilerParams(dimension_semantics=sem, vmem_limit_bytes=VMEM_LIMIT)


def _resident(shape, index_map):
    return pl.BlockSpec(shape, index_map, pipeline_mode=pl.Buffered(1))


def _silu(x):
    return x * (1.0 / (1.0 + jnp.exp(-x)))


def _dot(a, b):
    return jnp.dot(a, b, preferred_element_type=F32)


def _dot_nt(a, b):
    return lax.dot_general(a, b, (((1,), (1,)), ((), ())), preferred_element_type=F32)


def _dot_tn(a, b):
    return lax.dot_general(a, b, (((0,), (0,)), ((), ())), preferred_element_type=F32)


@functools.lru_cache(maxsize=None)
def _dft_matrices(L):
    kt = (np.arange(L, dtype=np.int64)[:, None] * np.arange(L, dtype=np.int64)[None, :]) % (2 * L)
    ang = kt.astype(np.float64) * (math.pi / L)
    return np.cos(ang).astype(np.float32), np.sin(ang).astype(np.float32)


@functools.lru_cache(maxsize=None)
def _filter_features(L):
    t = np.linspace(0.0, 1.0, L)[:, None]
    omega = 2.0 * math.pi * np.arange(L)[:, None] / L
    bands = np.linspace(1e-4, HY_BANDS - 1, HY_BANDS)[None, :]
    z = np.concatenate([t, np.cos(bands * omega), -np.sin(bands * omega)], axis=-1)
    z = np.pad(z, ((0, 0), (0, HY_EMB_PAD - HY_EMB)))
    deltas = np.abs(np.linspace(math.log(HY_TARGET) / HY_SLOW_DECAY, math.log(HY_TARGET) / HY_FAST_DECAY, HY_W))
    window = np.exp(-t * deltas[None, :])
    return z.astype(np.float32), window.astype(np.float32)


@functools.lru_cache(maxsize=None)
def _rope_tables(L):
    half = RET_HEAD_DIM // 2
    quarter = half // 2
    t = np.arange(L)
    pos = np.stack([t // GRID_W, t % GRID_W], axis=0).astype(np.float64)
    freqs = ROPE_BASE ** (-np.arange(quarter, dtype=np.float64) / quarter)
    cos = np.zeros((L, RET_HEAD_DIM))
    sin = np.zeros((L, RET_HEAD_DIM))
    for a in range(2):
        ang = pos[a][:, None] * freqs[None, :]
        base = a * half
        cos[:, base:base + quarter] = np.cos(ang)
        cos[:, base + quarter:base + half] = np.cos(ang)
        sin[:, base:base + quarter] = -np.sin(ang)
        sin[:, base + quarter:base + half] = np.sin(ang)
    return (np.tile(cos, (1, RET_HEADS)).astype(np.float32), np.tile(sin, (1, RET_HEADS)).astype(np.float32))


@functools.lru_cache(maxsize=None)
def _head_block_ones(width):
    i = np.arange(width) // NA_HEAD_DIM
    return (i[:, None] == i[None, :]).astype(np.float32)


def _mod_kernel(c_ref, w_ref, b_ref, o_ref):
    s = _silu(c_ref[...])
    o_ref[...] = jnp.dot(s, w_ref[...], preferred_element_type=F32, precision=lax.Precision.HIGHEST) + b_ref[...]


def _modulation(cc, ada_w, ada_b):
    R = cc.shape[0]
    tn = 1024
    return pl.pallas_call(
        _mod_kernel,
        grid=(DEPTH, 3 * D_MODEL // tn),
        in_specs=[
            pl.BlockSpec((R, D_MODEL), lambda i, j: (0, 0)),
            pl.BlockSpec((None, D_MODEL, tn), lambda i, j: (i, 0, j)),
            pl.BlockSpec((None, 1, tn), lambda i, j: (i, 0, j)),
        ],
        out_specs=pl.BlockSpec((None, R, tn), lambda i, j: (i, 0, j)),
        out_shape=jax.ShapeDtypeStruct((DEPTH, R, 3 * D_MODEL), F32),
        compiler_params=_cparams("parallel", "parallel"),
        name="modulation",
    )(cc, ada_w, ada_b.reshape(DEPTH, 1, 3 * D_MODEL))


def _inproj_kernel(x_ref, mod_ref, nw_ref, w_ref, o_ref):
    x = x_ref[...]
    xn = x * lax.rsqrt(jnp.mean(x * x, axis=-1, keepdims=True) + EPS)
    shift = mod_ref[:, 0:D_MODEL]
    scale = mod_ref[:, D_MODEL:2 * D_MODEL]
    h = (xn * nw_ref[...] * (1.0 + scale) + shift).astype(BF16)
    tn = 1024
    for j in range(IN_W // tn):
        o_ref[:, j * tn:(j + 1) * tn] = _dot(h, w_ref[:, j * tn:(j + 1) * tn]).astype(o_ref.dtype)


def _in_projection(xf, mod, mod_index, norm_w, w_bf, tm):
    R = xf.shape[0]
    return pl.pallas_call(
        _inproj_kernel,
        grid=(R // tm,),
        in_specs=[
            pl.BlockSpec((tm, D_MODEL), lambda i: (i, 0)),
            pl.BlockSpec((None, 1, 3 * D_MODEL), lambda i: (mod_index(i), 0, 0)),
            _resident((1, D_MODEL), lambda i: (0, 0)),
            _resident((D_MODEL, IN_W), lambda i: (0, 0)),
        ],
        out_specs=pl.BlockSpec((tm, IN_W), lambda i: (i, 0)),
        out_shape=jax.ShapeDtypeStruct((R, IN_W), BF16),
        compiler_params=_cparams("parallel"),
        name="in_projection",
    )(xf, mod, norm_w, w_bf)


def _outproj_kernel(za_ref, zy_ref, zr_ref, x_ref, mod_ref, w_ref, o_ref):
    acc = _dot(za_ref[...], w_ref[0:NA_W, :])
    acc += _dot(zy_ref[...], w_ref[NA_W:NA_W + HY_W, :])
    acc += _dot(zr_ref[...], w_ref[NA_W + HY_W:MIX_W, :])
    gate = mod_ref[:, 2 * D_MODEL:3 * D_MODEL]
    o_ref[...] = x_ref[...] + gate * acc


def _out_projection(za, zy, zr, xf, mod, mod_index, w_bf, tm):
    R = xf.shape[0]
    return pl.pallas_call(
        _outproj_kernel,
        grid=(R // tm,),
        in_specs=[
            pl.BlockSpec((tm, NA_W), lambda i: (i, 0)),
            pl.BlockSpec((tm, HY_W), lambda i: (i, 0)),
            pl.BlockSpec((tm, RET_W), lambda i: (i, 0)),
            pl.BlockSpec((tm, D_MODEL), lambda i: (i, 0)),
            pl.BlockSpec((None, 1, 3 * D_MODEL), lambda i: (mod_index(i), 0, 0)),
            _resident((MIX_W, D_MODEL), lambda i: (0, 0)),
        ],
        out_specs=pl.BlockSpec((tm, D_MODEL), lambda i: (i, 0)),
        out_shape=jax.ShapeDtypeStruct((R, D_MODEL), F32),
        compiler_params=_cparams("parallel"),
        name="out_projection",
    )(za, zy, zr, xf, mod, w_bf)


def _head_rms(x, ones_bf, gain):
    ss = _dot((x * x).astype(BF16), ones_bf)
    return x * lax.rsqrt(ss * (1.0 / NA_HEAD_DIM) + EPS) * gain


def _pair_masks():
    lane = lax.broadcasted_iota(jnp.int32, (1, 2 * NA_HEAD_DIM), 1)
    return lane < NA_HEAD_DIM


def _attend_heads(q, key_parts, val_parts, bias_fn):
    first = _pair_masks()
    outs = []
    for pair in range(NA_HEADS // 2):
        lo = pair * 2 * NA_HEAD_DIM
        hi = lo + 2 * NA_HEAD_DIM
        qp = q[:, lo:hi]
        o_pair = None
        for sub in range(2):
            h = 2 * pair + sub
            sel = first if sub == 0 else jnp.logical_not(first)
            qm = jnp.where(sel, qp, 0.0).astype(BF16)
            scores = []
            for part, kp in enumerate(key_parts):
                s = _dot_nt(qm, kp[:, lo:hi])
                b = bias_fn(h, part)
                if b is not None:
                    s = s + b
                scores.append(s)
            m = scores[0].max(axis=-1, keepdims=True)
            for s in scores[1:]:
                m = jnp.maximum(m, s.max(axis=-1, keepdims=True))
            denom = None
            acc = None
            for s, vp in zip(scores, val_parts):
                p = jnp.exp(s - m)
                ps = p.sum(axis=-1, keepdims=True)
                denom = ps if denom is None else denom + ps
                pv = _dot(p.astype(BF16), vp[:, lo:hi])
                acc = pv if acc is None else acc + pv
            o_h = acc * (1.0 / denom)
            o_pair = o_h if o_pair is None else jnp.where(first, o_pair, o_h)
        outs.append(o_pair)
    return jnp.concatenate(outs, axis=-1)


def _na_kernel(q_ref, k_ref, v_ref, g_ref, kc_ref, vc_ref, qg_ref, kg_ref, ones_ref, rpb_ref, o_ref,
               kn_scr, kcn_scr, vat_scr, vbt_scr, vcat_scr, vcbt_scr, bias_scr, mask_scr, s0_scr, s1_scr,
               p0_scr, p1_scr, qm_scr, ot_scr, *, rows):
    batch = pl.program_id(0)
    grp = pl.program_id(1)
    ones_bf = ones_ref[...]
    tq = NA_G * GRID_W
    nwin = NA_WIN * GRID_W
    Lc = kc_ref.shape[0]
    pair_w = 2 * NA_HEAD_DIM
    kt = NA_KT
    first = lax.broadcasted_iota(jnp.int32, (1, pair_w), 1) < NA_HEAD_DIM

    @pl.when((batch == 0) & (grp == 0))
    def _():
        ck = lax.broadcasted_iota(jnp.int32, (GRID_W, pair_w), 0)
        lane = lax.broadcasted_iota(jnp.int32, (GRID_W, pair_w), 1)
        cq = lane % GRID_W
        col_start = jnp.clip(cq - NA_KW // 2, 0, GRID_W - NA_KW)
        col_ok = (ck >= col_start) & (ck < col_start + NA_KW)
        left = lane < GRID_W

        def body(i, carry):
            for h in range(NA_HEADS):
                v0 = jnp.broadcast_to(rpb_ref[i + 1, h:h + 1, :], (GRID_W, pair_w))
                v1 = jnp.broadcast_to(rpb_ref[i, h:h + 1, :], (GRID_W, pair_w))
                t0 = pltpu.roll(v0, pair_w - (NA_KW - 1), 1, stride=1, stride_axis=0)
                t1 = pltpu.roll(v1, GRID_W - (NA_KW - 1), 1, stride=1, stride_axis=0)
                bias_scr[i, h] = jnp.where(col_ok, jnp.where(left, t0, t1), NEG_INF)
            return carry

        lax.fori_loop(0, NA_DR_NUM, body, 0)

    @pl.when(grp == 0)
    def _():
        kgain = kg_ref[...]
        own_a = (lax.broadcasted_iota(jnp.int32, (NA_W, 1), 0) % pair_w) < NA_HEAD_DIM
        for src, dst, vat, vbt, vsrc in ((k_ref, kn_scr, vat_scr, vbt_scr, v_ref),
                                         (kc_ref, kcn_scr, vcat_scr, vcbt_scr, vc_ref)):
            for i in range(src.shape[0] // kt):
                sl = slice(i * kt, (i + 1) * kt)
                dst[sl, :] = _head_rms(src[sl, :].astype(F32), ones_bf, kgain).astype(BF16)
                vt = vsrc[sl, :].astype(F32).T
                vat[i] = jnp.where(own_a, vt, 1.0).astype(BF16)
                vbt[i] = jnp.where(own_a, 1.0, vt).astype(BF16)

    n_kt = (nwin + Lc) // kt
    rows_per_tile = kt // GRID_W
    n_ch = (nwin + Lc) // GRID_W
    n_items = NA_GPS * NA_HEADS

    s_bufs = (s0_scr, s1_scr)
    p_bufs = (p0_scr, p1_scr)
    row_zero = jnp.minimum(grp, 0)

    def rows_of(start, size):
        return pl.ds(pl.multiple_of(row_zero + start, GRID_W), size)

    def head_lanes(h):
        return slice((h // 2) * pair_w, (h // 2 + 1) * pair_w)

    key0, tile0, dr_base = [], [], []
    for gi in range(NA_GPS):
        r0 = (grp * NA_GPS + gi) * NA_G
        ws = jnp.clip(r0 - NA_KH // 2, 0, rows - NA_WIN)
        key0.append(pl.multiple_of(ws * GRID_W, kt))
        tile0.append(ws // rows_per_tile)
        dr_base.append(ws - r0 + (NA_KH - 1) - NA_DR_MIN)
        w_id = lax.broadcasted_iota(jnp.int32, (16, tq), 0)
        g_id = lax.broadcasted_iota(jnp.int32, (16, tq), 1) // GRID_W
        lo = jnp.clip(r0 + g_id - NA_KH // 2, 0, rows - NA_KH) - ws
        mask_scr[gi] = jnp.where((w_id >= lo) & (w_id < lo + NA_KH), 0.0, NEG_INF).astype(F32)
        q = _head_rms(q_ref[gi * tq:(gi + 1) * tq, :].astype(F32), ones_bf,
                      qg_ref[...] * (NA_HEAD_DIM ** -0.5 * LOG2E))
        for h in range(NA_HEADS):
            sel = first if h % 2 == 0 else jnp.logical_not(first)
            qm_scr[gi * NA_HEADS + h, rows_of(0, tq), :] = jnp.where(sel, q[:, head_lanes(h)], 0.0).astype(BF16)

    col_max = {}

    def qk_piece(n, j):
        gi, h = divmod(n, NA_HEADS)
        last = j == n_kt - 1
        keys = kcn_scr[:, head_lanes(h)] if last else kn_scr[pl.ds(key0[gi] + j * kt, kt), head_lanes(h)]
        sv = _dot_nt(keys, qm_scr[n, rows_of(0, tq), :])
        for dw in range(rows_per_tile):
            sw = sv[dw * GRID_W:(dw + 1) * GRID_W, :]
            if not last:
                w = j * rows_per_tile + dw
                bias = jnp.concatenate([bias_scr[dr_base[gi] + (w - g), h] for g in range(0, NA_G, 2)], axis=-1)
                sw = sw + bias + mask_scr[gi, w:w + 1, :]
            s_bufs[n % 2][rows_of(j * kt + dw * GRID_W, GRID_W), :] = sw
            part = sw.reshape(GRID_W // 8, 8, tq).max(axis=0)
            col_max[n] = part if (j == 0 and dw == 0) else jnp.maximum(col_max[n], part)

    def exp_chunk(n, c):
        if c == 0:
            col_max[n] = col_max[n].max(axis=0, keepdims=True)
        rows_c = rows_of(c * GRID_W, GRID_W)
        p_bufs[n % 2][rows_c, :] = jnp.exp2(s_bufs[n % 2][rows_c, :] - col_max[n]).astype(BF16)
        if c == n_ch - 1:
            del col_max[n]

    def pv(n):
        gi, h = divmod(n, NA_HEADS)
        vt, vct = (vat_scr, vcat_scr) if h % 2 == 0 else (vbt_scr, vcbt_scr)
        ot = _dot(vct[0, head_lanes(h), :], p_bufs[n % 2][rows_of(nwin, Lc), :])
        for j in range(n_kt - 1):
            ot += _dot(vt[tile0[gi] + j, head_lanes(h), :], p_bufs[n % 2][rows_of(j * kt, kt), :])
        own, den = (ot[0:NA_HEAD_DIM], ot[NA_HEAD_DIM:]) if h % 2 == 0 else (ot[NA_HEAD_DIM:], ot[0:NA_HEAD_DIM])
        ot_scr[rows_of(n * NA_HEAD_DIM, NA_HEAD_DIM), :] = own * (1.0 / den)
        if h % 2 == 1:
            o_pair = ot_scr[rows_of((n - 1) * NA_HEAD_DIM, pair_w), :].T
            q_rows = slice(gi * tq, (gi + 1) * tq)
            gate = _silu(g_ref[q_rows, head_lanes(h)].astype(F32))
            o_ref[q_rows, head_lanes(h)] = (o_pair * gate).astype(o_ref.dtype)

    for t in range(n_items + 2):
        for i in range(n_ch):
            if 0 <= t - 1 < n_items:
                exp_chunk(t - 1, i)
            if i % 4 == 0 and t < n_items and i // 4 < n_kt:
                qk_piece(t, i // 4)
            if i == 2 and 0 <= t - 2:
                pv(t - 2)


def _neighborhood_attention(u, uc, q_gain, k_gain, rpb, B, L, Lc):
    rows = L // GRID_W
    ngrp = rows // (NA_G * NA_GPS)
    tq = NA_G * GRID_W
    tstep = tq * NA_GPS
    ones = jnp.asarray(_head_block_ones(NA_W)).astype(BF16)
    qg = jnp.tile(q_gain, NA_HEADS)[None, :]
    kg = jnp.tile(k_gain, NA_HEADS)[None, :]
    n_dr = 2 * NA_KH - 1
    rpb_t = jnp.transpose(rpb.astype(F32)[:, :, ::-1] * LOG2E, (1, 0, 2))
    lo_pad = 1 - NA_DR_MIN
    rpb_t = jnp.pad(rpb_t, ((lo_pad, NA_DR_NUM + 1 - lo_pad - n_dr), (0, 0), (0, 2 * GRID_W - (2 * NA_KW - 1))))
    return pl.pallas_call(
        functools.partial(_na_kernel, rows=rows),
        grid=(B, ngrp),
        in_specs=[
            pl.BlockSpec((tstep, NA_W), lambda b, g: (b * ngrp + g, 0)),
            pl.BlockSpec((L, NA_W), lambda b, g: (b, 1)),
            pl.BlockSpec((L, NA_W), lambda b, g: (b, 2)),
            pl.BlockSpec((tstep, NA_W), lambda b, g: (b * ngrp + g, 3)),
            pl.BlockSpec((Lc, NA_W), lambda b, g: (b, 1)),
            pl.BlockSpec((Lc, NA_W), lambda b, g: (b, 2)),
            _resident((1, NA_W), lambda b, g: (0, 0)),
            _resident((1, NA_W), lambda b, g: (0, 0)),
            _resident((NA_W, NA_W), lambda b, g: (0, 0)),
            _resident((NA_DR_NUM + 1, NA_HEADS, 2 * GRID_W), lambda b, g: (0, 0, 0)),
        ],
        out_specs=pl.BlockSpec((tstep, NA_W), lambda b, g: (b * ngrp + g, 0)),
        out_shape=jax.ShapeDtypeStruct((B * L, NA_W), BF16),
        scratch_shapes=[
            pltpu.VMEM((L, NA_W), BF16), pltpu.VMEM((Lc, NA_W), BF16),
            pltpu.VMEM((L // NA_KT, NA_W, NA_KT), BF16), pltpu.VMEM((L // NA_KT, NA_W, NA_KT), BF16),
            pltpu.VMEM((Lc // NA_KT, NA_W, NA_KT), BF16), pltpu.VMEM((Lc // NA_KT, NA_W, NA_KT), BF16),
            pltpu.VMEM((NA_DR_NUM, NA_HEADS, GRID_W, 2 * GRID_W), F32),
            pltpu.VMEM((NA_GPS, 16, tq), F32),
            pltpu.VMEM((NA_WIN * GRID_W + Lc, tq), F32), pltpu.VMEM((NA_WIN * GRID_W + Lc, tq), F32),
            pltpu.VMEM((NA_WIN * GRID_W + Lc, tq), BF16), pltpu.VMEM((NA_WIN * GRID_W + Lc, tq), BF16),
            pltpu.VMEM((NA_GPS * NA_HEADS, tq, 2 * NA_HEAD_DIM), BF16), pltpu.VMEM((NA_GPS * NA_W, tq), F32),
        ],
        compiler_params=_cparams("arbitrary", "arbitrary"),
        name="neighborhood_attention",
    )(u, u, u, u, uc, uc, qg, kg, ones, rpb_t)


def _ctx_attn_kernel(q_ref, k_ref, v_ref, g_ref, qg_ref, kg_ref, ones_ref, o_ref):
    ones_bf = ones_ref[...]
    q = _head_rms(q_ref[...].astype(F32), ones_bf, qg_ref[...] * (NA_HEAD_DIM ** -0.5))
    k = _head_rms(k_ref[...].astype(F32), ones_bf, kg_ref[...]).astype(BF16)
    o = _attend_heads(q, [k], [v_ref[...]], lambda h, part: None)
    o_ref[...] = (o * _silu(g_ref[...].astype(F32))).astype(o_ref.dtype)


def _context_attention(uc, q_gain, k_gain, B, Lc):
    ones = jnp.asarray(_head_block_ones(NA_W)).astype(BF16)
    qg = jnp.tile(q_gain, NA_HEADS)[None, :]
    kg = jnp.tile(k_gain, NA_HEADS)[None, :]
    return pl.pallas_call(
        _ctx_attn_kernel,
        grid=(B,),
        in_specs=[
            pl.BlockSpec((Lc, NA_W), lambda b: (b, 0)),
            pl.BlockSpec((Lc, NA_W), lambda b: (b, 1)),
            pl.BlockSpec((Lc, NA_W), lambda b: (b, 2)),
            pl.BlockSpec((Lc, NA_W), lambda b: (b, 3)),
            _resident((1, NA_W), lambda b: (0, 0)),
            _resident((1, NA_W), lambda b: (0, 0)),
            _resident((NA_W, NA_W), lambda b: (0, 0)),
        ],
        out_specs=pl.BlockSpec((Lc, NA_W), lambda b: (b, 0)),
        out_shape=jax.ShapeDtypeStruct((B * Lc, NA_W), BF16),
        compiler_params=_cparams("parallel"),
        name="context_attention",
    )(uc, uc, uc, uc, qg, kg, ones)


def _alt_sign(shape):
    row = lax.broadcasted_iota(jnp.int32, shape, 0)
    return (1 - 2 * (row & 1)).astype(F32)


def _filter_kernel(z_ref, win_ref, w1_ref, b1_ref, w2_ref, b2_ref, w3_ref, sf_ref, c_ref, s_ref,
                   hr_ref, hi_ref, hn_ref, h_scr, *, L):
    hp = lax.Precision.HIGHEST
    dot = lambda a, b: jnp.dot(a, b, preferred_element_type=F32, precision=hp)

    @pl.when(pl.program_id(0) == 0)
    def _():
        h = jnp.sin(sf_ref[0:1, :] * (dot(z_ref[...], w1_ref[...]) + b1_ref[...]))
        h_scr[...] = jnp.sin(sf_ref[1:2, :] * (dot(h, w2_ref[...]) + b2_ref[...]))

    taps = dot(h_scr[...], w3_ref[...])
    win = win_ref[...]
    row = lax.broadcasted_iota(jnp.int32, (L, HY_W), 0)
    sign = _alt_sign((L, HY_W))
    inv_n = 1.0 / (2 * L)
    hf = jnp.concatenate([taps[:, 0:HY_W], taps[:, 2 * HY_W:3 * HY_W]], axis=0) * win
    hb = jnp.concatenate([taps[:, HY_W:2 * HY_W], taps[:, 3 * HY_W:4 * HY_W]], axis=0) * win
    hb = jnp.where(row == 0, 0.0, hb)
    norm = jnp.sum(jnp.abs(hf), axis=0, keepdims=True) + jnp.sum(jnp.abs(hb), axis=0, keepdims=True)
    inv = 1.0 / norm
    even = (hf + hb) * inv
    odd = (hf - hb) * inv
    e_hi = even.astype(BF16)
    e_lo = (even - e_hi.astype(F32)).astype(BF16)
    o_hi = odd.astype(BF16)
    o_lo = (odd - o_hi.astype(F32)).astype(BF16)
    wgt = jnp.where(row == 0, inv_n, 2.0 * inv_n)
    hr_ref[...] = (_dot(c_ref[...], e_hi) + _dot(c_ref[...], e_lo)) * wgt
    hi_ref[...] = -(_dot(s_ref[...], o_hi) + _dot(s_ref[...], o_lo)) * wgt
    hn_ref[...] = jnp.sum(even * sign, axis=0, keepdims=True) * inv_n


def _block_diag2(w):
    z = jnp.zeros_like(w)
    return jnp.concatenate([jnp.concatenate([w, z], axis=1), jnp.concatenate([z, w], axis=1)], axis=0)


def _hyena_filter(L, w1, b1, w2, b2, w3, sin_freq, c_bf, s_bf):
    z, window = _filter_features(L)
    z2 = np.concatenate([z[:L // 2], z[L // 2:]], axis=1)
    w1d = _block_diag2(jnp.pad(w1, ((0, HY_EMB_PAD - HY_EMB), (0, 0))))
    w2d = _block_diag2(w2)
    w3d = jnp.stack([_block_diag2(w3[:, o * 2 * HY_W:(o + 1) * 2 * HY_W]) for o in range(2)])
    pair = lambda v: jnp.tile(v, (1, 2))
    full = lambda *shape: _resident(shape, lambda o: (0,) * len(shape))
    per_order = lambda rows: pl.BlockSpec((None, rows, HY_W), lambda o: (o, 0, 0))
    return pl.pallas_call(
        functools.partial(_filter_kernel, L=L),
        grid=(2,),
        in_specs=[
            full(L // 2, 2 * HY_EMB_PAD), full(L, HY_W), full(2 * HY_EMB_PAD, 2 * HY_FFN), full(1, 2 * HY_FFN),
            full(2 * HY_FFN, 2 * HY_FFN), full(1, 2 * HY_FFN),
            pl.BlockSpec((None, 2 * HY_FFN, 4 * HY_W), lambda o: (o, 0, 0)),
            full(2, 2 * HY_FFN), full(L, L), full(L, L),
        ],
        out_specs=[per_order(L), per_order(L), per_order(1)],
        out_shape=[jax.ShapeDtypeStruct((2, L, HY_W), F32), jax.ShapeDtypeStruct((2, L, HY_W), F32),
                   jax.ShapeDtypeStruct((2, 1, HY_W), F32)],
        scratch_shapes=[pltpu.VMEM((L // 2, 2 * HY_FFN), F32)],
        compiler_params=_cparams("arbitrary"),
        name="hyena_filter",
    )(jnp.asarray(z2), jnp.asarray(window), w1d, pair(b1[None, :]), w2d, pair(b2[None, :]), w3d, pair(sin_freq),
      c_bf, s_bf)


def _hyena_kernel(v_ref, x1_ref, x2_ref, g_ref, cw_ref, cb_ref, skip_ref, c_ref, s_ref, hr_ref, hi_ref, hn_ref,
                  o_ref, a_scr, abf_scr, x_scr, zr_scr, zi_scr, *, L):
    ck = min(HY_ROWS, L)
    halo = 16
    chunks = [(t0, t0 + ck) for t0 in range(0, L, ck)]
    sign = _alt_sign((ck, HY_W))

    def short_conv(ref, j, t0, t1):
        lo, hi = max(t0 - halo, 0), min(t1 + halo, L)
        u = ref[lo:hi, :].astype(F32)
        row = lo + lax.broadcasted_iota(jnp.int32, (hi - lo, HY_W), 0)
        prev = jnp.where(row == 0, 0.0, pltpu.roll(u, 1, 0))
        nxt = jnp.where(row == L - 1, 0.0, pltpu.roll(u, hi - lo - 1, 0))
        w = cw_ref[:, j * HY_W:(j + 1) * HY_W]
        z = prev * w[0:1, :] + u * w[1:2, :] + nxt * w[2:3, :] + cb_ref[:, j * HY_W:(j + 1) * HY_W]
        return z[t0 - lo:t1 - lo, :]

    def spectrum_product(order):
        for k0, k1 in chunks:
            ur = _dot(c_ref[k0:k1, :], abf_scr[...])
            us = _dot(s_ref[k0:k1, :], abf_scr[...])
            hr = hr_ref[order, k0:k1, :]
            hi = hi_ref[order, k0:k1, :]
            zr_scr[k0:k1, :] = (ur * hr + us * hi).astype(BF16)
            zi_scr[k0:k1, :] = (us * hr - ur * hi).astype(BF16)

    def long_conv_chunk(order, t0, t1, nyq):
        a = a_scr[t0:t1, :]
        y = _dot(c_ref[t0:t1, :], zr_scr[...]) + _dot(s_ref[t0:t1, :], zi_scr[...])
        return y + sign * nyq + a * skip_ref[order:order + 1, :]

    nyq = jnp.zeros((1, HY_W), F32)
    for t0, t1 in chunks:
        v = short_conv(v_ref, 0, t0, t1)
        a_scr[t0:t1, :] = v
        abf_scr[t0:t1, :] = v.astype(BF16)
        x_scr[t0:t1, :] = short_conv(x1_ref, 1, t0, t1)
        nyq += jnp.sum(v * sign, axis=0, keepdims=True)
    spectrum_product(0)
    nyq0 = nyq * hn_ref[0]
    nyq = jnp.zeros((1, HY_W), F32)
    for t0, t1 in chunks:
        y = x_scr[t0:t1, :] * long_conv_chunk(0, t0, t1, nyq0)
        a_scr[t0:t1, :] = y
        abf_scr[t0:t1, :] = y.astype(BF16)
        nyq += jnp.sum(y * sign, axis=0, keepdims=True)
    spectrum_product(1)
    nyq1 = nyq * hn_ref[1]
    for t0, t1 in chunks:
        y = short_conv(x2_ref, 2, t0, t1) * long_conv_chunk(1, t0, t1, nyq1)
        o_ref[t0:t1, :] = (y * _silu(g_ref[t0:t1, :].astype(F32))).astype(o_ref.dtype)


def _hyena(u, conv_w, conv_b, skip, c_bf, s_bf, hr, hi, hn, B, L):
    col = lambda j: pl.BlockSpec((L, CB), lambda b: (b, j))
    return pl.pallas_call(
        functools.partial(_hyena_kernel, L=L),
        grid=(B,),
        in_specs=[
            col(CB_HY_V), col(CB_HY_X1), col(CB_HY_X2), col(CB_HY_G),
            _resident((3, 3 * HY_W), lambda b: (0, 0)),
            _resident((1, 3 * HY_W), lambda b: (0, 0)),
            _resident((2, HY_W), lambda b: (0, 0)),
            _resident((L, L), lambda b: (0, 0)),
            _resident((L, L), lambda b: (0, 0)),
            _resident((2, L, HY_W), lambda b: (0, 0, 0)),
            _resident((2, L, HY_W), lambda b: (0, 0, 0)),
            _resident((2, 1, HY_W), lambda b: (0, 0, 0)),
        ],
        out_specs=pl.BlockSpec((L, HY_W), lambda b: (b, 0)),
        out_shape=jax.ShapeDtypeStruct((B * L, HY_W), BF16),
        scratch_shapes=[
            pltpu.VMEM((L, HY_W), F32), pltpu.VMEM((L, HY_W), BF16), pltpu.VMEM((L, HY_W), F32),
            pltpu.VMEM((L, HY_W), BF16), pltpu.VMEM((L, HY_W), BF16),
        ],
        compiler_params=_cparams("parallel"),
        name="hyena",
    )(u, u, u, u, conv_w, conv_b[None, :], skip, c_bf, s_bf, hr, hi, hn)


def _ret_kernel(*refs, L, Lc, has_init):
    if has_init:
        (q_ref, k_ref, v_ref, g_ref, kc_ref, vc_ref, cos_ref, sin_ref, rate_ref, rrow_ref, ones_ref,
         o_ref, q_scr, k_scr, sf_scr, sb_scr, r_scr) = refs
    else:
        (q_ref, k_ref, v_ref, g_ref, cos_ref, sin_ref, rate_ref, rrow_ref, ones_ref,
         o_ref, q_scr, k_scr, sf_scr, sb_scr, r_scr) = refs
    C = min(RET_CHUNK, L)
    nch = L // C
    W = RET_W
    quarter = RET_HEAD_DIM // 4
    block = ones_ref[...]
    lg = -jnp.exp(rate_ref[...])
    lg_f, lg_b = lg[0:1, :], lg[1:2, :]

    lane = lax.broadcasted_iota(jnp.int32, (L, W), 1)
    first_quarter = (lane % (2 * quarter)) < quarter

    def rope(a):
        swapped = jnp.where(first_quarter, pltpu.roll(a, W - quarter, 1), pltpu.roll(a, quarter, 1))
        return a * cos_ref[...] + swapped * sin_ref[...]

    if has_init:
        q_scr[...] = rope(q_ref[...].astype(F32)).astype(BF16)
        k_scr[...] = rope(k_ref[...].astype(F32) * (RET_HEAD_DIM ** -0.5)).astype(BF16)
    else:
        q_scr[...] = q_ref[...]
        k_scr[...] = (k_ref[...].astype(F32) * (RET_HEAD_DIM ** -0.5)).astype(BF16)

    def decays(n_rows):
        pos = lax.broadcasted_iota(jnp.int32, (n_rows, W), 0).astype(F32)
        return (jnp.exp(lg_f * (n_rows - 1.0 - pos)),
                jnp.exp(lg_b * pos))

    def chunk_states(k_bf, v_bf, zf, zb):
        kf = (k_bf.astype(F32) * zf).astype(BF16)
        kb = (k_bf.astype(F32) * zb).astype(BF16)
        return _dot_tn(kf, v_bf) * block, _dot_tn(kb, v_bf) * block

    zeta_f, zeta_b = decays(C)
    if has_init:
        zc_f, zc_b = decays(Lc)
        kc = (kc_ref[...].astype(F32) * (RET_HEAD_DIM ** -0.5)).astype(BF16)
        s0_f, s0_b = chunk_states(kc, vc_ref[...], zc_f, zc_b)
    else:
        s0_f = jnp.zeros((W, W), F32)
        s0_b = jnp.zeros((W, W), F32)

    for n in range(nch):
        kv_f, kv_b = chunk_states(k_scr[n * C:(n + 1) * C, :], v_ref[n * C:(n + 1) * C, :], zeta_f, zeta_b)
        sf_scr[n] = kv_f
        sb_scr[n] = kv_b
    dec_f = jnp.exp(lg_f * float(C))
    dec_b = jnp.exp(lg_b * float(C))
    state = s0_f
    for n in range(nch):
        kv = sf_scr[n]
        sf_scr[n] = state
        state = dec_f * state + kv
    state = s0_b
    for n in range(nch - 1, -1, -1):
        kv = sb_scr[n]
        sb_scr[n] = state
        state = dec_b * state + kv

    posc = lax.broadcasted_iota(jnp.int32, (C, W), 0).astype(F32)
    xi_f = jnp.exp(lg_f * (posc + 1.0))
    xi_b = jnp.exp(lg_b * (float(C) - posc))
    diff = (lax.broadcasted_iota(jnp.int32, (C, C), 0) - lax.broadcasted_iota(jnp.int32, (C, C), 1)).astype(F32)
    lane_c = lax.broadcasted_iota(jnp.int32, (1, W), 1)
    dmask = []
    for h in range(RET_HEADS):
        rf = -jnp.exp(rrow_ref[h:h + 1, 0:C])
        rb = -jnp.exp(rrow_ref[RET_HEADS + h:RET_HEADS + h + 1, 0:C])
        dmask.append(jnp.where(diff >= 0, jnp.exp(rf * jnp.maximum(diff, 0.0)), 0.0)
                     + jnp.where(diff <= 0, jnp.exp(rb * jnp.maximum(-diff, 0.0)), 0.0))

    for n in range(nch):
        qn = q_scr[n * C:(n + 1) * C, :]
        kn = k_scr[n * C:(n + 1) * C, :]
        vn = v_ref[n * C:(n + 1) * C, :]
        qf = qn.astype(F32)
        o = _dot((qf * xi_f).astype(BF16), sf_scr[n].astype(BF16))
        o += _dot((qf * xi_b).astype(BF16), sb_scr[n].astype(BF16))
        for h in range(RET_HEADS):
            sel = (lane_c // RET_HEAD_DIM) == h
            s = _dot_nt(jnp.where(sel, qn, jnp.zeros_like(qn)), kn) * dmask[h]
            o += jnp.where(sel, _dot(s.astype(BF16), vn), 0.0)
        r_scr[n * C:(n + 1) * C, :] = o

    r = r_scr[...]
    ss = _dot((r * r).astype(BF16), block.astype(BF16))
    rn = r * lax.rsqrt(ss * (1.0 / RET_HEAD_DIM) + EPS)
    o_ref[...] = (rn * _silu(g_ref[...].astype(F32))).astype(o_ref.dtype)


def _retention(u, uc, ret_log_rate, B, L, Lc, has_init):
    C = min(RET_CHUNK, L)
    nch = L // C
    col = lambda j: pl.BlockSpec((L, CB), lambda b: (b, j))
    ccol = lambda j: pl.BlockSpec((Lc, CB), lambda b: (b, j))
    cos, sin = _rope_tables(L)
    rate_lane = jnp.repeat(ret_log_rate, RET_HEAD_DIM, axis=1)
    rate_row = jnp.broadcast_to(ret_log_rate.reshape(2 * RET_HEADS, 1), (2 * RET_HEADS, RET_W))
    ones = jnp.asarray(_head_block_ones(RET_W))
    in_specs = [col(CB_RE_Q), col(CB_RE_K), col(CB_RE_V), col(CB_RE_G)]
    args = [u, u, u, u]
    if has_init:
        in_specs += [ccol(CB_RE_K), ccol(CB_RE_V)]
        args += [uc, uc]
    in_specs += [
        _resident((L, RET_W), lambda b: (0, 0)),
        _resident((L, RET_W), lambda b: (0, 0)),
        _resident((2, RET_W), lambda b: (0, 0)),
        _resident((2 * RET_HEADS, RET_W), lambda b: (0, 0)),
        _resident((RET_W, RET_W), lambda b: (0, 0)),
    ]
    args += [jnp.asarray(cos), jnp.asarray(sin), rate_lane, rate_row, ones]
    return pl.pallas_call(
        functools.partial(_ret_kernel, L=L, Lc=Lc, has_init=has_init),
        grid=(B,),
        in_specs=in_specs,
        out_specs=pl.BlockSpec((L, RET_W), lambda b: (b, 0)),
        out_shape=jax.ShapeDtypeStruct((B * L, RET_W), BF16),
        scratch_shapes=[
            pltpu.VMEM((L, RET_W), BF16), pltpu.VMEM((L, RET_W), BF16),
            pltpu.VMEM((nch, RET_W, RET_W), F32), pltpu.VMEM((nch, RET_W, RET_W), F32),
            pltpu.VMEM((L, RET_W), F32),
        ],
        compiler_params=_cparams("parallel"),
        name="retention",
    )(*args)


def kernel(x, c, ctx, c_ctx, norm_w, ada_w, ada_b, w_in, w_out, na_q_gain, na_k_gain, na_rpb, hy_conv_w, hy_conv_b,
           hy_w1, hy_b1, hy_w2, hy_b2, hy_w3, hy_sin_freq, hy_skip, ret_log_rate):
    B, L, D = x.shape
    Lc = ctx.shape[1]
    assert D == D_MODEL and L % (GRID_W * NA_G * NA_GPS) == 0 and L // GRID_W >= NA_WIN
    assert NA_G == NA_KT // GRID_W and (L // GRID_W - NA_WIN) % NA_G == 0 and Lc == NA_KT

    n_cond = 16
    cc = jnp.concatenate([c, c_ctx[None, :], jnp.zeros((n_cond - B - 1, D), F32)], axis=0)
    mods = _modulation(cc, ada_w, ada_b)

    cx, sx = (jnp.asarray(a).astype(BF16) for a in _dft_matrices(L))
    ccx, scx = (jnp.asarray(a).astype(BF16) for a in _dft_matrices(Lc))

    tm_x = 512
    xf = x.reshape(B * L, D)
    cf = ctx.reshape(B * Lc, D)
    x_mod = lambda i: i // (L // tm_x)
    c_mod = lambda i: B

    for i in range(DEPTH):
        mod = mods[i].reshape(n_cond, 1, 3 * D)
        w_in_bf = w_in[i].astype(BF16)
        w_out_bf = w_out[i].astype(BF16)
        nw = norm_w[i][None, :]
        filt = (hy_w1[i], hy_b1[i], hy_w2[i], hy_b2[i], hy_w3[i], hy_sin_freq[i])

        u = _in_projection(xf, mod, x_mod, nw, w_in_bf, tm_x)
        uc = _in_projection(cf, mod, c_mod, nw, w_in_bf, Lc)

        za = _neighborhood_attention(u, uc, na_q_gain[i], na_k_gain[i], na_rpb[i], B, L, Lc)
        hr, hi, hn = _hyena_filter(L, *filt, cx, sx)
        zy = _hyena(u, hy_conv_w[i], hy_conv_b[i], hy_skip[i], cx, sx, hr, hi, hn, B, L)
        zr = _retention(u, uc, ret_log_rate[i], B, L, Lc, True)
        x_new = _out_projection(za, zy, zr, xf, mod, x_mod, w_out_bf, tm_x)

        if i < DEPTH - 1:
            zac = _context_attention(uc, na_q_gain[i], na_k_gain[i], B, Lc)
            hrc, hic, hnc = _hyena_filter(Lc, *filt, ccx, scx)
            zyc = _hyena(uc, hy_conv_w[i], hy_conv_b[i], hy_skip[i], ccx, scx, hrc, hic, hnc, B, Lc)
            zrc = _retention(uc, uc, ret_log_rate[i], B, Lc, Lc, False)
            cf = _out_projection(zac, zyc, zrc, cf, mod, c_mod, w_out_bf, Lc)
        xf = x_new

    return xf.reshape(B, L, D)
```

```python
import functools
import math

import numpy as np
import jax
import jax.numpy as jnp
from jax import lax
from jax.experimental import pallas as pl
from jax.experimental.pallas import tpu as pltpu

F32 = jnp.float32
BF16 = jnp.bfloat16

D_MODEL = 1024
DEPTH = 2
GRID_W = 64
NA_HEADS = 8
NA_HEAD_DIM = 64
NA_W = NA_HEADS * NA_HEAD_DIM
NA_KH = 8
NA_KW = 16
HY_W = 256
HY_BANDS = 8
HY_EMB = 1 + 2 * HY_BANDS
HY_EMB_PAD = 32
HY_FFN = 64
HY_FAST_DECAY = 0.3
HY_SLOW_DECAY = 1.5
HY_TARGET = 1e-2
RET_HEADS = 4
RET_HEAD_DIM = 64
RET_W = RET_HEADS * RET_HEAD_DIM
ROPE_BASE = 10000.0
EPS = 1e-6
NEG_INF = -1e30
LOG2E = 1.4426950408889634
IN_W = 4 * NA_W + 4 * HY_W + 4 * RET_W
MIX_W = NA_W + HY_W + RET_W

CB = 256
CB_HY_V, CB_HY_X1, CB_HY_X2, CB_HY_G = 8, 9, 10, 11
CB_RE_Q, CB_RE_K, CB_RE_V, CB_RE_G = 12, 13, 14, 15

NA_G = 4
NA_WIN = NA_G + NA_KH
NA_GPS = 4
NA_KT = 256
NA_DR_MIN = -4
NA_DR_NUM = 23
RET_CHUNK = 256
HY_ROWS = 512
VMEM_LIMIT = 56 * 1024 * 1024


def _cparams(*sem):
    return pltpu.CompilerParams(dimension_semantics=sem, vmem_limit_bytes=VMEM_LIMIT)


def _resident(shape, index_map):
    return pl.BlockSpec(shape, index_map, pipeline_mode=pl.Buffered(1))


def _silu(x):
    return x * (1.0 / (1.0 + jnp.exp(-x)))


def _dot(a, b):
    return jnp.dot(a, b, preferred_element_type=F32)


def _dot_nt(a, b):
    return lax.dot_general(a, b, (((1,), (1,)), ((), ())), preferred_element_type=F32)


def _dot_tn(a, b):
    return lax.dot_general(a, b, (((0,), (0,)), ((), ())), preferred_element_type=F32)


@functools.lru_cache(maxsize=None)
def _dft_matrices(L):
    kt = (np.arange(L, dtype=np.int64)[:, None] * np.arange(L, dtype=np.int64)[None, :]) % (2 * L)
    ang = kt.astype(np.float64) * (math.pi / L)
    return np.cos(ang).astype(np.float32), np.sin(ang).astype(np.float32)


@functools.lru_cache(maxsize=None)
def _filter_features(L):
    t = np.linspace(0.0, 1.0, L)[:, None]
    omega = 2.0 * math.pi * np.arange(L)[:, None] / L
    bands = np.linspace(1e-4, HY_BANDS - 1, HY_BANDS)[None, :]
    z = np.concatenate([t, np.cos(bands * omega), -np.sin(bands * omega)], axis=-1)
    z = np.pad(z, ((0, 0), (0, HY_EMB_PAD - HY_EMB)))
    deltas = np.abs(np.linspace(math.log(HY_TARGET) / HY_SLOW_DECAY, math.log(HY_TARGET) / HY_FAST_DECAY, HY_W))
    window = np.exp(-t * deltas[None, :])
    return z.astype(np.float32), window.astype(np.float32)


@functools.lru_cache(maxsize=None)
def _rope_tables(L):
    half = RET_HEAD_DIM // 2
    quarter = half // 2
    t = np.arange(L)
    pos = np.stack([t // GRID_W, t % GRID_W], axis=0).astype(np.float64)
    freqs = ROPE_BASE ** (-np.arange(quarter, dtype=np.float64) / quarter)
    cos = np.zeros((L, RET_HEAD_DIM))
    sin = np.zeros((L, RET_HEAD_DIM))
    for a in range(2):
        ang = pos[a][:, None] * freqs[None, :]
        base = a * half
        cos[:, base:base + quarter] = np.cos(ang)
        cos[:, base + quarter:base + half] = np.cos(ang)
        sin[:, base:base + quarter] = -np.sin(ang)
        sin[:, base + quarter:base + half] = np.sin(ang)
    return (np.tile(cos, (1, RET_HEADS)).astype(np.float32), np.tile(sin, (1, RET_HEADS)).astype(np.float32))


@functools.lru_cache(maxsize=None)
def _head_block_ones(width):
    i = np.arange(width) // NA_HEAD_DIM
    return (i[:, None] == i[None, :]).astype(np.float32)


def _mod_kernel(c_ref, w_ref, b_ref, o_ref):
    s = _silu(c_ref[...])
    o_ref[...] = jnp.dot(s, w_ref[...], preferred_element_type=F32, precision=lax.Precision.HIGHEST) + b_ref[...]


def _modulation(cc, ada_w, ada_b):
    R = cc.shape[0]
    tn = 1024
    return pl.pallas_call(
        _mod_kernel,
        grid=(DEPTH, 3 * D_MODEL // tn),
        in_specs=[
            pl.BlockSpec((R, D_MODEL), lambda i, j: (0, 0)),
            pl.BlockSpec((None, D_MODEL, tn), lambda i, j: (i, 0, j)),
            pl.BlockSpec((None, 1, tn), lambda i, j: (i, 0, j)),
        ],
        out_specs=pl.BlockSpec((None, R, tn), lambda i, j: (i, 0, j)),
        out_shape=jax.ShapeDtypeStruct((DEPTH, R, 3 * D_MODEL), F32),
        compiler_params=_cparams("parallel", "parallel"),
        name="modulation",
    )(cc, ada_w, ada_b.reshape(DEPTH, 1, 3 * D_MODEL))


def _proj_kernel(*refs, with_out, with_in):
    refs = list(refs)
    if with_out:
        za_ref, zy_ref, zr_ref, x_ref, mod_o_ref, w_out_ref = refs[:6]
        del refs[:6]
    else:
        x_ref = refs.pop(0)
    if with_in:
        mod_i_ref, nw_ref, w_in_ref = refs[:3]
        del refs[:3]
    x = x_ref[...]
    if with_out:
        acc = _dot(za_ref[...], w_out_ref[0:NA_W, :])
        acc += _dot(zy_ref[...], w_out_ref[NA_W:NA_W + HY_W, :])
        acc += _dot(zr_ref[...], w_out_ref[NA_W + HY_W:MIX_W, :])
        x = x + mod_o_ref[:, 2 * D_MODEL:3 * D_MODEL] * acc
        refs.pop(0)[...] = x
    if with_in:
        u_ref = refs.pop(0)
        xn = x * lax.rsqrt(jnp.mean(x * x, axis=-1, keepdims=True) + EPS)
        shift = mod_i_ref[:, 0:D_MODEL]
        scale = mod_i_ref[:, D_MODEL:2 * D_MODEL]
        h = (xn * nw_ref[...] * (1.0 + scale) + shift).astype(BF16)
        tn = 1024
        for j in range(IN_W // tn):
            u_ref[:, j * tn:(j + 1) * tn] = _dot(h, w_in_ref[:, j * tn:(j + 1) * tn]).astype(u_ref.dtype)


def _projection(xf, mod_index, tm, out_args=None, in_args=None):
    R = xf.shape[0]
    row = lambda w: pl.BlockSpec((tm, w), lambda i: (i, 0))
    mod_spec = pl.BlockSpec((None, 1, 3 * D_MODEL), lambda i: (mod_index(i), 0, 0))
    args, in_specs, out_specs, out_shape = [], [], [], []
    if out_args is not None:
        za, zy, zr, mod_o, w_out_bf = out_args
        args += [za, zy, zr, xf, mod_o, w_out_bf]
        in_specs += [row(NA_W), row(HY_W), row(RET_W), row(D_MODEL), mod_spec,
                     _resident((MIX_W, D_MODEL), lambda i: (0, 0))]
        out_specs.append(row(D_MODEL))
        out_shape.append(jax.ShapeDtypeStruct((R, D_MODEL), F32))
    else:
        args.append(xf)
        in_specs.append(row(D_MODEL))
    if in_args is not None:
        mod_i, norm_w, w_in_bf = in_args
        args += [mod_i, norm_w, w_in_bf]
        in_specs += [mod_spec, _resident((1, D_MODEL), lambda i: (0, 0)), _resident((D_MODEL, IN_W), lambda i: (0, 0))]
        out_specs.append(row(IN_W))
        out_shape.append(jax.ShapeDtypeStruct((R, IN_W), BF16))
    outs = pl.pallas_call(
        functools.partial(_proj_kernel, with_out=out_args is not None, with_in=in_args is not None),
        grid=(R // tm,),
        in_specs=in_specs,
        out_specs=out_specs,
        out_shape=out_shape,
        compiler_params=_cparams("parallel"),
        name="projection",
    )(*args)
    return outs if len(outs) > 1 else outs[0]


def _head_rms(x, ones_bf, gain):
    ss = _dot((x * x).astype(BF16), ones_bf)
    return x * lax.rsqrt(ss * (1.0 / NA_HEAD_DIM) + EPS) * gain


def _pair_masks():
    lane = lax.broadcasted_iota(jnp.int32, (1, 2 * NA_HEAD_DIM), 1)
    return lane < NA_HEAD_DIM


def _attend_heads(q, key_parts, val_parts, bias_fn):
    first = _pair_masks()
    outs = []
    for pair in range(NA_HEADS // 2):
        lo = pair * 2 * NA_HEAD_DIM
        hi = lo + 2 * NA_HEAD_DIM
        qp = q[:, lo:hi]
        o_pair = None
        for sub in range(2):
            h = 2 * pair + sub
            sel = first if sub == 0 else jnp.logical_not(first)
            qm = jnp.where(sel, qp, 0.0).astype(BF16)
            scores = []
            for part, kp in enumerate(key_parts):
                s = _dot_nt(qm, kp[:, lo:hi])
                b = bias_fn(h, part)
                if b is not None:
                    s = s + b
                scores.append(s)
            m = scores[0].max(axis=-1, keepdims=True)
            for s in scores[1:]:
                m = jnp.maximum(m, s.max(axis=-1, keepdims=True))
            denom = None
            acc = None
            for s, vp in zip(scores, val_parts):
                p = jnp.exp(s - m)
                ps = p.sum(axis=-1, keepdims=True)
                denom = ps if denom is None else denom + ps
                pv = _dot(p.astype(BF16), vp[:, lo:hi])
                acc = pv if acc is None else acc + pv
            o_h = acc * (1.0 / denom)
            o_pair = o_h if o_pair is None else jnp.where(first, o_pair, o_h)
        outs.append(o_pair)
    return jnp.concatenate(outs, axis=-1)


def _na_kernel(q_ref, k_ref, v_ref, g_ref, kc_ref, vc_ref, qg_ref, kg_ref, ones_ref, rpb_ref, o_ref,
               kn_scr, kcn_scr, vat_scr, vbt_scr, vcat_scr, vcbt_scr, bias_scr, mask_scr, s0_scr, s1_scr,
               p0_scr, p1_scr, qm_scr, ot_scr, *, rows):
    batch = pl.program_id(0)
    grp = pl.program_id(1)
    ones_bf = ones_ref[...]
    tq = NA_G * GRID_W
    nwin = NA_WIN * GRID_W
    Lc = kc_ref.shape[0]
    pair_w = 2 * NA_HEAD_DIM
    kt = NA_KT
    first = lax.broadcasted_iota(jnp.int32, (1, pair_w), 1) < NA_HEAD_DIM

    @pl.when((batch == 0) & (grp == 0))
    def _():
        ck = lax.broadcasted_iota(jnp.int32, (GRID_W, pair_w), 0)
        lane = lax.broadcasted_iota(jnp.int32, (GRID_W, pair_w), 1)
        cq = lane % GRID_W
        col_start = jnp.clip(cq - NA_KW // 2, 0, GRID_W - NA_KW)
        col_ok = (ck >= col_start) & (ck < col_start + NA_KW)
        left = lane < GRID_W

        def body(i, carry):
            for h in range(NA_HEADS):
                v0 = jnp.broadcast_to(rpb_ref[i + 1, h:h + 1, :], (GRID_W, pair_w))
                v1 = jnp.broadcast_to(rpb_ref[i, h:h + 1, :], (GRID_W, pair_w))
                t0 = pltpu.roll(v0, pair_w - (NA_KW - 1), 1, stride=1, stride_axis=0)
                t1 = pltpu.roll(v1, GRID_W - (NA_KW - 1), 1, stride=1, stride_axis=0)
                bias_scr[i, h] = jnp.where(col_ok, jnp.where(left, t0, t1), NEG_INF)
            return carry

        lax.fori_loop(0, NA_DR_NUM, body, 0)

    @pl.when(grp == 0)
    def _():
        kgain = kg_ref[...]
        own_a = (lax.broadcasted_iota(jnp.int32, (NA_W, 1), 0) % pair_w) < NA_HEAD_DIM
        for src, dst, vat, vbt, vsrc in ((k_ref, kn_scr, vat_scr, vbt_scr, v_ref),
                                         (kc_ref, kcn_scr, vcat_scr, vcbt_scr, vc_ref)):
            for i in range(src.shape[0] // kt):
                sl = slice(i * kt, (i + 1) * kt)
                dst[sl, :] = _head_rms(src[sl, :].astype(F32), ones_bf, kgain).astype(BF16)
                vt = vsrc[sl, :].astype(F32).T
                vat[i] = jnp.where(own_a, vt, 1.0).astype(BF16)
                vbt[i] = jnp.where(own_a, 1.0, vt).astype(BF16)

    n_kt = (nwin + Lc) // kt
    rows_per_tile = kt // GRID_W
    n_ch = (nwin + Lc) // GRID_W
    n_items = NA_GPS * NA_HEADS

    s_bufs = (s0_scr, s1_scr)
    p_bufs = (p0_scr, p1_scr)
    row_zero = jnp.minimum(grp, 0)

    def rows_of(start, size):
        return pl.ds(pl.multiple_of(row_zero + start, GRID_W), size)

    def head_lanes(h):
        return slice((h // 2) * pair_w, (h // 2 + 1) * pair_w)

    key0, tile0, dr_base = [], [], []
    for gi in range(NA_GPS):
        r0 = (grp * NA_GPS + gi) * NA_G
        ws = jnp.clip(r0 - NA_KH // 2, 0, rows - NA_WIN)
        key0.append(pl.multiple_of(ws * GRID_W, kt))
        tile0.append(ws // rows_per_tile)
        dr_base.append(ws - r0 + (NA_KH - 1) - NA_DR_MIN)
        w_id = lax.broadcasted_iota(jnp.int32, (16, tq), 0)
        g_id = lax.broadcasted_iota(jnp.int32, (16, tq), 1) // GRID_W
        lo = jnp.clip(r0 + g_id - NA_KH // 2, 0, rows - NA_KH) - ws
        mask_scr[gi] = jnp.where((w_id >= lo) & (w_id < lo + NA_KH), 0.0, NEG_INF).astype(F32)
        q = _head_rms(q_ref[gi * tq:(gi + 1) * tq, :].astype(F32), ones_bf,
                      qg_ref[...] * (NA_HEAD_DIM ** -0.5 * LOG2E))
        for h in range(NA_HEADS):
            sel = first if h % 2 == 0 else jnp.logical_not(first)
            qm_scr[gi * NA_HEADS + h, rows_of(0, tq), :] = jnp.where(sel, q[:, head_lanes(h)], 0.0).astype(BF16)

    col_max = {}

    def qk_piece(n, j):
        gi, h = divmod(n, NA_HEADS)
        last = j == n_kt - 1
        keys = kcn_scr[:, head_lanes(h)] if last else kn_scr[pl.ds(key0[gi] + j * kt, kt), head_lanes(h)]
        sv = _dot_nt(keys, qm_scr[n, rows_of(0, tq), :])
        for dw in range(rows_per_tile):
            sw = sv[dw * GRID_W:(dw + 1) * GRID_W, :]
            if not last:
                w = j * rows_per_tile + dw
                bias = jnp.concatenate([bias_scr[dr_base[gi] + (w - g), h] for g in range(0, NA_G, 2)], axis=-1)
                sw = sw + bias + mask_scr[gi, w:w + 1, :]
            s_bufs[n % 2][rows_of(j * kt + dw * GRID_W, GRID_W), :] = sw
            part = sw.reshape(GRID_W // 8, 8, tq).max(axis=0)
            col_max[n] = part if (j == 0 and dw == 0) else jnp.maximum(col_max[n], part)

    def exp_chunk(n, c):
        if c == 0:
            col_max[n] = col_max[n].max(axis=0, keepdims=True)
        rows_c = rows_of(c * GRID_W, GRID_W)
        p_bufs[n % 2][rows_c, :] = jnp.exp2(s_bufs[n % 2][rows_c, :] - col_max[n]).astype(BF16)
        if c == n_ch - 1:
            del col_max[n]

    def pv(n):
        gi, h = divmod(n, NA_HEADS)
        vt, vct = (vat_scr, vcat_scr) if h % 2 == 0 else (vbt_scr, vcbt_scr)
        ot = _dot(vct[0, head_lanes(h), :], p_bufs[n % 2][rows_of(nwin, Lc), :])
        for j in range(n_kt - 1):
            ot += _dot(vt[tile0[gi] + j, head_lanes(h), :], p_bufs[n % 2][rows_of(j * kt, kt), :])
        own, den = (ot[0:NA_HEAD_DIM], ot[NA_HEAD_DIM:]) if h % 2 == 0 else (ot[NA_HEAD_DIM:], ot[0:NA_HEAD_DIM])
        ot_scr[rows_of(n * NA_HEAD_DIM, NA_HEAD_DIM), :] = own * (1.0 / den)
        if h % 2 == 1:
            o_pair = ot_scr[rows_of((n - 1) * NA_HEAD_DIM, pair_w), :].T
            q_rows = slice(gi * tq, (gi + 1) * tq)
            gate = _silu(g_ref[q_rows, head_lanes(h)].astype(F32))
            o_ref[q_rows, head_lanes(h)] = (o_pair * gate).astype(o_ref.dtype)

    for t in range(n_items + 2):
        for i in range(n_ch):
            if 0 <= t - 1 < n_items:
                exp_chunk(t - 1, i)
            if i % 4 == 0 and t < n_items and i // 4 < n_kt:
                qk_piece(t, i // 4)
            if i == 2 and 0 <= t - 2:
                pv(t - 2)


def _neighborhood_attention(u, uc, q_gain, k_gain, rpb, B, L, Lc):
    rows = L // GRID_W
    ngrp = rows // (NA_G * NA_GPS)
    tq = NA_G * GRID_W
    tstep = tq * NA_GPS
    ones = jnp.asarray(_head_block_ones(NA_W)).astype(BF16)
    qg = jnp.tile(q_gain, NA_HEADS)[None, :]
    kg = jnp.tile(k_gain, NA_HEADS)[None, :]
    n_dr = 2 * NA_KH - 1
    rpb_t = jnp.transpose(rpb.astype(F32)[:, :, ::-1] * LOG2E, (1, 0, 2))
    lo_pad = 1 - NA_DR_MIN
    rpb_t = jnp.pad(rpb_t, ((lo_pad, NA_DR_NUM + 1 - lo_pad - n_dr), (0, 0), (0, 2 * GRID_W - (2 * NA_KW - 1))))
    return pl.pallas_call(
        functools.partial(_na_kernel, rows=rows),
        grid=(B, ngrp),
        in_specs=[
            pl.BlockSpec((tstep, NA_W), lambda b, g: (b * ngrp + g, 0)),
            pl.BlockSpec((L, NA_W), lambda b, g: (b, 1)),
            pl.BlockSpec((L, NA_W), lambda b, g: (b, 2)),
            pl.BlockSpec((tstep, NA_W), lambda b, g: (b * ngrp + g, 3)),
            pl.BlockSpec((Lc, NA_W), lambda b, g: (b, 1)),
            pl.BlockSpec((Lc, NA_W), lambda b, g: (b, 2)),
            _resident((1, NA_W), lambda b, g: (0, 0)),
            _resident((1, NA_W), lambda b, g: (0, 0)),
            _resident((NA_W, NA_W), lambda b, g: (0, 0)),
            _resident((NA_DR_NUM + 1, NA_HEADS, 2 * GRID_W), lambda b, g: (0, 0, 0)),
        ],
        out_specs=pl.BlockSpec((tstep, NA_W), lambda b, g: (b * ngrp + g, 0)),
        out_shape=jax.ShapeDtypeStruct((B * L, NA_W), BF16),
        scratch_shapes=[
            pltpu.VMEM((L, NA_W), BF16), pltpu.VMEM((Lc, NA_W), BF16),
            pltpu.VMEM((L // NA_KT, NA_W, NA_KT), BF16), pltpu.VMEM((L // NA_KT, NA_W, NA_KT), BF16),
            pltpu.VMEM((Lc // NA_KT, NA_W, NA_KT), BF16), pltpu.VMEM((Lc // NA_KT, NA_W, NA_KT), BF16),
            pltpu.VMEM((NA_DR_NUM, NA_HEADS, GRID_W, 2 * GRID_W), F32),
            pltpu.VMEM((NA_GPS, 16, tq), F32),
            pltpu.VMEM((NA_WIN * GRID_W + Lc, tq), F32), pltpu.VMEM((NA_WIN * GRID_W + Lc, tq), F32),
            pltpu.VMEM((NA_WIN * GRID_W + Lc, tq), BF16), pltpu.VMEM((NA_WIN * GRID_W + Lc, tq), BF16),
            pltpu.VMEM((NA_GPS * NA_HEADS, tq, 2 * NA_HEAD_DIM), BF16), pltpu.VMEM((NA_GPS * NA_W, tq), F32),
        ],
        compiler_params=_cparams("arbitrary", "arbitrary"),
        name="neighborhood_attention",
    )(u, u, u, u, uc, uc, qg, kg, ones, rpb_t)


def _ctx_attn_kernel(q_ref, k_ref, v_ref, g_ref, qg_ref, kg_ref, ones_ref, o_ref):
    ones_bf = ones_ref[...]
    q = _head_rms(q_ref[...].astype(F32), ones_bf, qg_ref[...] * (NA_HEAD_DIM ** -0.5))
    k = _head_rms(k_ref[...].astype(F32), ones_bf, kg_ref[...]).astype(BF16)
    o = _attend_heads(q, [k], [v_ref[...]], lambda h, part: None)
    o_ref[...] = (o * _silu(g_ref[...].astype(F32))).astype(o_ref.dtype)


def _context_attention(uc, q_gain, k_gain, B, Lc):
    ones = jnp.asarray(_head_block_ones(NA_W)).astype(BF16)
    qg = jnp.tile(q_gain, NA_HEADS)[None, :]
    kg = jnp.tile(k_gain, NA_HEADS)[None, :]
    return pl.pallas_call(
        _ctx_attn_kernel,
        grid=(B,),
        in_specs=[
            pl.BlockSpec((Lc, NA_W), lambda b: (b, 0)),
            pl.BlockSpec((Lc, NA_W), lambda b: (b, 1)),
            pl.BlockSpec((Lc, NA_W), lambda b: (b, 2)),
            pl.BlockSpec((Lc, NA_W), lambda b: (b, 3)),
            _resident((1, NA_W), lambda b: (0, 0)),
            _resident((1, NA_W), lambda b: (0, 0)),
            _resident((NA_W, NA_W), lambda b: (0, 0)),
        ],
        out_specs=pl.BlockSpec((Lc, NA_W), lambda b: (b, 0)),
        out_shape=jax.ShapeDtypeStruct((B * Lc, NA_W), BF16),
        compiler_params=_cparams("parallel"),
        name="context_attention",
    )(uc, uc, uc, uc, qg, kg, ones)


def _alt_sign(shape):
    row = lax.broadcasted_iota(jnp.int32, shape, 0)
    return (1 - 2 * (row & 1)).astype(F32)


def _filter_kernel(z_ref, win_ref, w1_ref, b1_ref, w2_ref, b2_ref, w3_ref, sf_ref, c_ref, s_ref,
                   hr_ref, hi_ref, hn_ref, h_scr, *, L):
    hp = lax.Precision.HIGHEST
    dot = lambda a, b: jnp.dot(a, b, preferred_element_type=F32, precision=hp)

    @pl.when(pl.program_id(0) == 0)
    def _():
        h = jnp.sin(sf_ref[0:1, :] * (dot(z_ref[...], w1_ref[...]) + b1_ref[...]))
        h_scr[...] = jnp.sin(sf_ref[1:2, :] * (dot(h, w2_ref[...]) + b2_ref[...]))

    taps = dot(h_scr[...], w3_ref[...])
    win = win_ref[...]
    row = lax.broadcasted_iota(jnp.int32, (L, HY_W), 0)
    sign = _alt_sign((L, HY_W))
    inv_n = 1.0 / (2 * L)
    hf = jnp.concatenate([taps[:, 0:HY_W], taps[:, 2 * HY_W:3 * HY_W]], axis=0) * win
    hb = jnp.concatenate([taps[:, HY_W:2 * HY_W], taps[:, 3 * HY_W:4 * HY_W]], axis=0) * win
    hb = jnp.where(row == 0, 0.0, hb)
    norm = jnp.sum(jnp.abs(hf), axis=0, keepdims=True) + jnp.sum(jnp.abs(hb), axis=0, keepdims=True)
    inv = 1.0 / norm
    even = (hf + hb) * inv
    odd = (hf - hb) * inv
    wgt = jnp.where(row == 0, inv_n, 2.0 * inv_n)
    hr_ref[...] = _dot(c_ref[...], even.astype(BF16)) * wgt
    hi_ref[...] = -_dot(s_ref[...], odd.astype(BF16)) * wgt
    hn_ref[...] = jnp.sum(even * sign, axis=0, keepdims=True) * inv_n


def _block_diag2(w):
    z = jnp.zeros_like(w)
    return jnp.concatenate([jnp.concatenate([w, z], axis=1), jnp.concatenate([z, w], axis=1)], axis=0)


def _hyena_filter(L, w1, b1, w2, b2, w3, sin_freq, c_bf, s_bf):
    z, window = _filter_features(L)
    z2 = np.concatenate([z[:L // 2], z[L // 2:]], axis=1)
    w1d = _block_diag2(jnp.pad(w1, ((0, HY_EMB_PAD - HY_EMB), (0, 0))))
    w2d = _block_diag2(w2)
    w3d = jnp.stack([_block_diag2(w3[:, o * 2 * HY_W:(o + 1) * 2 * HY_W]) for o in range(2)])
    pair = lambda v: jnp.tile(v, (1, 2))
    full = lambda *shape: _resident(shape, lambda o: (0,) * len(shape))
    per_order = lambda rows: pl.BlockSpec((None, rows, HY_W), lambda o: (o, 0, 0))
    return pl.pallas_call(
        functools.partial(_filter_kernel, L=L),
        grid=(2,),
        in_specs=[
            full(L // 2, 2 * HY_EMB_PAD), full(L, HY_W), full(2 * HY_EMB_PAD, 2 * HY_FFN), full(1, 2 * HY_FFN),
            full(2 * HY_FFN, 2 * HY_FFN), full(1, 2 * HY_FFN),
            pl.BlockSpec((None, 2 * HY_FFN, 4 * HY_W), lambda o: (o, 0, 0)),
            full(2, 2 * HY_FFN), full(L, L), full(L, L),
        ],
        out_specs=[per_order(L), per_order(L), per_order(1)],
        out_shape=[jax.ShapeDtypeStruct((2, L, HY_W), F32), jax.ShapeDtypeStruct((2, L, HY_W), F32),
                   jax.ShapeDtypeStruct((2, 1, HY_W), F32)],
        scratch_shapes=[pltpu.VMEM((L // 2, 2 * HY_FFN), F32)],
        compiler_params=_cparams("arbitrary"),
        name="hyena_filter",
    )(jnp.asarray(z2), jnp.asarray(window), w1d, pair(b1[None, :]), w2d, pair(b2[None, :]), w3d, pair(sin_freq),
      c_bf, s_bf)


def _hyena_kernel(v_ref, x1_ref, x2_ref, g_ref, cw_ref, cb_ref, skip_ref, c_ref, s_ref, hr_ref, hi_ref, hn_ref,
                  o_ref, a_scr, abf_scr, x_scr, zr_scr, zi_scr, *, L):
    ck = min(HY_ROWS, L)
    halo = 16
    chunks = [(t0, t0 + ck) for t0 in range(0, L, ck)]
    sign = _alt_sign((ck, HY_W))

    def short_conv(ref, j, t0, t1):
        lo, hi = max(t0 - halo, 0), min(t1 + halo, L)
        u = ref[lo:hi, :].astype(F32)
        row = lo + lax.broadcasted_iota(jnp.int32, (hi - lo, HY_W), 0)
        prev = jnp.where(row == 0, 0.0, pltpu.roll(u, 1, 0))
        nxt = jnp.where(row == L - 1, 0.0, pltpu.roll(u, hi - lo - 1, 0))
        w = cw_ref[:, j * HY_W:(j + 1) * HY_W]
        z = prev * w[0:1, :] + u * w[1:2, :] + nxt * w[2:3, :] + cb_ref[:, j * HY_W:(j + 1) * HY_W]
        return z[t0 - lo:t1 - lo, :]

    def spectrum_product(order):
        for k0, k1 in chunks:
            ur = _dot(c_ref[k0:k1, :], abf_scr[...])
            us = _dot(s_ref[k0:k1, :], abf_scr[...])
            hr = hr_ref[order, k0:k1, :]
            hi = hi_ref[order, k0:k1, :]
            zr_scr[k0:k1, :] = (ur * hr + us * hi).astype(BF16)
            zi_scr[k0:k1, :] = (us * hr - ur * hi).astype(BF16)

    def long_conv_chunk(order, t0, t1, nyq):
        a = a_scr[t0:t1, :]
        y = _dot(c_ref[t0:t1, :], zr_scr[...]) + _dot(s_ref[t0:t1, :], zi_scr[...])
        return y + sign * nyq + a * skip_ref[order:order + 1, :]

    nyq = jnp.zeros((1, HY_W), F32)
    for t0, t1 in chunks:
        v = short_conv(v_ref, 0, t0, t1)
        a_scr[t0:t1, :] = v
        abf_scr[t0:t1, :] = v.astype(BF16)
        x_scr[t0:t1, :] = short_conv(x1_ref, 1, t0, t1)
        nyq += jnp.sum(v * sign, axis=0, keepdims=True)
    spectrum_product(0)
    nyq0 = nyq * hn_ref[0]
    nyq = jnp.zeros((1, HY_W), F32)
    for t0, t1 in chunks:
        y = x_scr[t0:t1, :] * long_conv_chunk(0, t0, t1, nyq0)
        a_scr[t0:t1, :] = y
        abf_scr[t0:t1, :] = y.astype(BF16)
        nyq += jnp.sum(y * sign, axis=0, keepdims=True)
    spectrum_product(1)
    nyq1 = nyq * hn_ref[1]
    for t0, t1 in chunks:
        y = short_conv(x2_ref, 2, t0, t1) * long_conv_chunk(1, t0, t1, nyq1)
        o_ref[t0:t1, :] = (y * _silu(g_ref[t0:t1, :].astype(F32))).astype(o_ref.dtype)


def _hyena(u, conv_w, conv_b, skip, c_bf, s_bf, hr, hi, hn, B, L):
    col = lambda j: pl.BlockSpec((L, CB), lambda b: (b, j))
    return pl.pallas_call(
        functools.partial(_hyena_kernel, L=L),
        grid=(B,),
        in_specs=[
            col(CB_HY_V), col(CB_HY_X1), col(CB_HY_X2), col(CB_HY_G),
            _resident((3, 3 * HY_W), lambda b: (0, 0)),
            _resident((1, 3 * HY_W), lambda b: (0, 0)),
            _resident((2, HY_W), lambda b: (0, 0)),
            _resident((L, L), lambda b: (0, 0)),
            _resident((L, L), lambda b: (0, 0)),
            _resident((2, L, HY_W), lambda b: (0, 0, 0)),
            _resident((2, L, HY_W), lambda b: (0, 0, 0)),
            _resident((2, 1, HY_W), lambda b: (0, 0, 0)),
        ],
        out_specs=pl.BlockSpec((L, HY_W), lambda b: (b, 0)),
        out_shape=jax.ShapeDtypeStruct((B * L, HY_W), BF16),
        scratch_shapes=[
            pltpu.VMEM((L, HY_W), F32), pltpu.VMEM((L, HY_W), BF16), pltpu.VMEM((L, HY_W), F32),
            pltpu.VMEM((L, HY_W), BF16), pltpu.VMEM((L, HY_W), BF16),
        ],
        compiler_params=_cparams("parallel"),
        name="hyena",
    )(u, u, u, u, conv_w, conv_b[None, :], skip, c_bf, s_bf, hr, hi, hn)


def _ret_kernel(*refs, L, Lc, has_init):
    if has_init:
        (q_ref, k_ref, v_ref, g_ref, kc_ref, vc_ref, cos_ref, sin_ref, rate_ref, rrow_ref, ones_ref,
         o_ref, q_scr, k_scr, sf_scr, sb_scr, r_scr) = refs
    else:
        (q_ref, k_ref, v_ref, g_ref, cos_ref, sin_ref, rate_ref, rrow_ref, ones_ref,
         o_ref, q_scr, k_scr, sf_scr, sb_scr, r_scr) = refs
    C = min(RET_CHUNK, L)
    nch = L // C
    W = RET_W
    quarter = RET_HEAD_DIM // 4
    block = ones_ref[...]
    lg = -jnp.exp(rate_ref[...])
    lg_f, lg_b = lg[0:1, :], lg[1:2, :]

    lane = lax.broadcasted_iota(jnp.int32, (L, W), 1)
    first_quarter = (lane % (2 * quarter)) < quarter

    def rope(a):
        swapped = jnp.where(first_quarter, pltpu.roll(a, W - quarter, 1), pltpu.roll(a, quarter, 1))
        return a * cos_ref[...] + swapped * sin_ref[...]

    if has_init:
        q_scr[...] = rope(q_ref[...].astype(F32)).astype(BF16)
        k_scr[...] = rope(k_ref[...].astype(F32) * (RET_HEAD_DIM ** -0.5)).astype(BF16)
    else:
        q_scr[...] = q_ref[...]
        k_scr[...] = (k_ref[...].astype(F32) * (RET_HEAD_DIM ** -0.5)).astype(BF16)

    def decays(n_rows):
        pos = lax.broadcasted_iota(jnp.int32, (n_rows, W), 0).astype(F32)
        return (jnp.exp(lg_f * (n_rows - 1.0 - pos)),
                jnp.exp(lg_b * pos))

    def chunk_states(k_bf, v_bf, zf, zb):
        kf = (k_bf.astype(F32) * zf).astype(BF16)
        kb = (k_bf.astype(F32) * zb).astype(BF16)
        return _dot_tn(kf, v_bf) * block, _dot_tn(kb, v_bf) * block

    zeta_f, zeta_b = decays(C)
    if has_init:
        zc_f, zc_b = decays(Lc)
        kc = (kc_ref[...].astype(F32) * (RET_HEAD_DIM ** -0.5)).astype(BF16)
        s0_f, s0_b = chunk_states(kc, vc_ref[...], zc_f, zc_b)
    else:
        s0_f = jnp.zeros((W, W), F32)
        s0_b = jnp.zeros((W, W), F32)

    for n in range(nch):
        kv_f, kv_b = chunk_states(k_scr[n * C:(n + 1) * C, :], v_ref[n * C:(n + 1) * C, :], zeta_f, zeta_b)
        sf_scr[n] = kv_f
        sb_scr[n] = kv_b
    dec_f = jnp.exp(lg_f * float(C))
    dec_b = jnp.exp(lg_b * float(C))
    state = s0_f
    for n in range(nch):
        kv = sf_scr[n]
        sf_scr[n] = state
        state = dec_f * state + kv
    state = s0_b
    for n in range(nch - 1, -1, -1):
        kv = sb_scr[n]
        sb_scr[n] = state
        state = dec_b * state + kv

    posc = lax.broadcasted_iota(jnp.int32, (C, W), 0).astype(F32)
    xi_f = jnp.exp(lg_f * (posc + 1.0))
    xi_b = jnp.exp(lg_b * (float(C) - posc))
    diff = (lax.broadcasted_iota(jnp.int32, (C, C), 0) - lax.broadcasted_iota(jnp.int32, (C, C), 1)).astype(F32)
    lane_c = lax.broadcasted_iota(jnp.int32, (1, W), 1)
    dmask = []
    for h in range(RET_HEADS):
        rf = -jnp.exp(rrow_ref[h:h + 1, 0:C])
        rb = -jnp.exp(rrow_ref[RET_HEADS + h:RET_HEADS + h + 1, 0:C])
        dmask.append(jnp.where(diff >= 0, jnp.exp(rf * jnp.maximum(diff, 0.0)), 0.0)
                     + jnp.where(diff <= 0, jnp.exp(rb * jnp.maximum(-diff, 0.0)), 0.0))

    for n in range(nch):
        qn = q_scr[n * C:(n + 1) * C, :]
        kn = k_scr[n * C:(n + 1) * C, :]
        vn = v_ref[n * C:(n + 1) * C, :]
        qf = qn.astype(F32)
        lhs = [(qf * xi_f).astype(BF16), (qf * xi_b).astype(BF16)]
        rhs = [sf_scr[n].astype(BF16), sb_scr[n].astype(BF16)]
        zero = jnp.zeros_like(qn)
        for h in range(RET_HEADS):
            sel = (lane_c // RET_HEAD_DIM) == h
            s = _dot_nt(jnp.where(sel, qn, zero), kn) * dmask[h]
            lhs.append(s.astype(BF16))
            rhs.append(jnp.where(sel, vn, zero))
        r_scr[n * C:(n + 1) * C, :] = _dot(jnp.concatenate(lhs, axis=1), jnp.concatenate(rhs, axis=0))

    r = r_scr[...]
    ss = _dot((r * r).astype(BF16), block.astype(BF16))
    rn = r * lax.rsqrt(ss * (1.0 / RET_HEAD_DIM) + EPS)
    o_ref[...] = (rn * _silu(g_ref[...].astype(F32))).astype(o_ref.dtype)


def _retention(u, uc, ret_log_rate, B, L, Lc, has_init):
    C = min(RET_CHUNK, L)
    nch = L // C
    col = lambda j: pl.BlockSpec((L, CB), lambda b: (b, j))
    ccol = lambda j: pl.BlockSpec((Lc, CB), lambda b: (b, j))
    cos, sin = _rope_tables(L)
    rate_lane = jnp.repeat(ret_log_rate, RET_HEAD_DIM, axis=1)
    rate_row = jnp.broadcast_to(ret_log_rate.reshape(2 * RET_HEADS, 1), (2 * RET_HEADS, RET_W))
    ones = jnp.asarray(_head_block_ones(RET_W))
    in_specs = [col(CB_RE_Q), col(CB_RE_K), col(CB_RE_V), col(CB_RE_G)]
    args = [u, u, u, u]
    if has_init:
        in_specs += [ccol(CB_RE_K), ccol(CB_RE_V)]
        args += [uc, uc]
    in_specs += [
        _resident((L, RET_W), lambda b: (0, 0)),
        _resident((L, RET_W), lambda b: (0, 0)),
        _resident((2, RET_W), lambda b: (0, 0)),
        _resident((2 * RET_HEADS, RET_W), lambda b: (0, 0)),
        _resident((RET_W, RET_W), lambda b: (0, 0)),
    ]
    args += [jnp.asarray(cos), jnp.asarray(sin), rate_lane, rate_row, ones]
    return pl.pallas_call(
        functools.partial(_ret_kernel, L=L, Lc=Lc, has_init=has_init),
        grid=(B,),
        in_specs=in_specs,
        out_specs=pl.BlockSpec((L, RET_W), lambda b: (b, 0)),
        out_shape=jax.ShapeDtypeStruct((B * L, RET_W), BF16),
        scratch_shapes=[
            pltpu.VMEM((L, RET_W), BF16), pltpu.VMEM((L, RET_W), BF16),
            pltpu.VMEM((nch, RET_W, RET_W), F32), pltpu.VMEM((nch, RET_W, RET_W), F32),
            pltpu.VMEM((L, RET_W), F32),
        ],
        compiler_params=_cparams("parallel"),
        name="retention",
    )(*args)


def kernel(x, c, ctx, c_ctx, norm_w, ada_w, ada_b, w_in, w_out, na_q_gain, na_k_gain, na_rpb, hy_conv_w, hy_conv_b,
           hy_w1, hy_b1, hy_w2, hy_b2, hy_w3, hy_sin_freq, hy_skip, ret_log_rate):
    B, L, D = x.shape
    Lc = ctx.shape[1]
    assert D == D_MODEL and L % (GRID_W * NA_G * NA_GPS) == 0 and L // GRID_W >= NA_WIN
    assert NA_G == NA_KT // GRID_W and (L // GRID_W - NA_WIN) % NA_G == 0 and Lc == NA_KT

    n_cond = 16
    cc = jnp.concatenate([c, c_ctx[None, :], jnp.zeros((n_cond - B - 1, D), F32)], axis=0)
    mods = _modulation(cc, ada_w, ada_b)

    cx, sx = (jnp.asarray(a).astype(BF16) for a in _dft_matrices(L))
    ccx, scx = (jnp.asarray(a).astype(BF16) for a in _dft_matrices(Lc))

    tm_x = 512
    xf = x.reshape(B * L, D)
    cf = ctx.reshape(B * Lc, D)
    x_mod = lambda i: i // (L // tm_x)
    c_mod = lambda i: B

    def layer_params(i):
        return mods[i].reshape(n_cond, 1, 3 * D), norm_w[i][None, :], w_in[i].astype(BF16)

    mod, nw, w_in_bf = layer_params(0)
    u = _projection(xf, x_mod, tm_x, in_args=(mod, nw, w_in_bf))
    uc = _projection(cf, c_mod, Lc, in_args=(mod, nw, w_in_bf))
    for i in range(DEPTH):
        w_out_bf = w_out[i].astype(BF16)
        filt = (hy_w1[i], hy_b1[i], hy_w2[i], hy_b2[i], hy_w3[i], hy_sin_freq[i])
        last = i == DEPTH - 1
        nxt = None if last else layer_params(i + 1)

        za = _neighborhood_attention(u, uc, na_q_gain[i], na_k_gain[i], na_rpb[i], B, L, Lc)
        hr, hi, hn = _hyena_filter(L, *filt, cx, sx)
        zy = _hyena(u, hy_conv_w[i], hy_conv_b[i], hy_skip[i], cx, sx, hr, hi, hn, B, L)
        zr = _retention(u, uc, ret_log_rate[i], B, L, Lc, True)
        if not last:
            zac = _context_attention(uc, na_q_gain[i], na_k_gain[i], B, Lc)
            hrc, hic, hnc = _hyena_filter(Lc, *filt, ccx, scx)
            zyc = _hyena(uc, hy_conv_w[i], hy_conv_b[i], hy_skip[i], ccx, scx, hrc, hic, hnc, B, Lc)
            zrc = _retention(uc, uc, ret_log_rate[i], B, Lc, Lc, False)
            cf, uc = _projection(cf, c_mod, Lc, out_args=(zac, zyc, zrc, mod, w_out_bf), in_args=nxt)
            xf, u = _projection(xf, x_mod, tm_x, out_args=(za, zy, zr, mod, w_out_bf), in_args=nxt)
            mod = nxt[0]
        else:
            xf = _projection(xf, x_mod, tm_x, out_args=(za, zy, zr, mod, w_out_bf))

    return xf.reshape(B, L, D)
```

```python
import functools
import math

import numpy as np
import jax
import jax.numpy as jnp
from jax import lax
from jax.experimental import pallas as pl
from jax.experimental.pallas import tpu as pltpu

F32 = jnp.float32
BF16 = jnp.bfloat16

D_MODEL = 1024
DEPTH = 2
GRID_W = 64
NA_HEADS = 8
NA_HEAD_DIM = 64
NA_W = NA_HEADS * NA_HEAD_DIM
NA_KH = 8
NA_KW = 16
HY_W = 256
HY_BANDS = 8
HY_EMB = 1 + 2 * HY_BANDS
HY_EMB_PAD = 32
HY_FFN = 64
HY_FAST_DECAY = 0.3
HY_SLOW_DECAY = 1.5
HY_TARGET = 1e-2
RET_HEADS = 4
RET_HEAD_DIM = 64
RET_W = RET_HEADS * RET_HEAD_DIM
ROPE_BASE = 10000.0
EPS = 1e-6
NEG_INF = -1e30
LOG2E = 1.4426950408889634
IN_W = 4 * NA_W + 4 * HY_W + 4 * RET_W
MIX_W = NA_W + HY_W + RET_W

CB = 256
CB_HY_V, CB_HY_X1, CB_HY_X2, CB_HY_G = 8, 9, 10, 11
CB_RE_Q, CB_RE_K, CB_RE_V, CB_RE_G = 12, 13, 14, 15

NA_G = 4
NA_WIN = NA_G + NA_KH
NA_GPS = 4
NA_KT = 256
NA_DR_MIN = -4
NA_DR_NUM = 23
RET_CHUNK = 256
HY_ROWS = 512
VMEM_LIMIT = 56 * 1024 * 1024


def _cparams(*sem):
    return pltpu.CompilerParams(dimension_semantics=sem, vmem_limit_bytes=VMEM_LIMIT)


def _resident(shape, index_map):
    return pl.BlockSpec(shape, index_map, pipeline_mode=pl.Buffered(1))


def _silu(x):
    return x * (1.0 / (1.0 + jnp.exp(-x)))


def _dot(a, b):
    return jnp.dot(a, b, preferred_element_type=F32)


def _dot_nt(a, b):
    return lax.dot_general(a, b, (((1,), (1,)), ((), ())), preferred_element_type=F32)


def _dot_tn(a, b):
    return lax.dot_general(a, b, (((0,), (0,)), ((), ())), preferred_element_type=F32)


@functools.lru_cache(maxsize=None)
def _dft_matrices(L):
    m = L // 2
    k = np.arange(m, dtype=np.int64)[:, None]
    j = np.arange(m, dtype=np.int64)[None, :]
    even = ((k * 2 * j) % (2 * L)).astype(np.float64) * (math.pi / L)
    odd = ((k * (2 * j + 1)) % (2 * L)).astype(np.float64) * (math.pi / L)
    mats = (np.cos(even), np.sin(even), np.cos(odd), np.sin(odd), np.cos(odd).T, np.sin(odd).T)
    return tuple(np.ascontiguousarray(a).astype(np.float32) for a in mats)


@functools.lru_cache(maxsize=None)
def _filter_features(L):
    t = np.linspace(0.0, 1.0, L)[:, None]
    omega = 2.0 * math.pi * np.arange(L)[:, None] / L
    bands = np.linspace(1e-4, HY_BANDS - 1, HY_BANDS)[None, :]
    z = np.concatenate([t, np.cos(bands * omega), -np.sin(bands * omega)], axis=-1)
    z = np.pad(z, ((0, 0), (0, HY_EMB_PAD - HY_EMB)))
    deltas = np.abs(np.linspace(math.log(HY_TARGET) / HY_SLOW_DECAY, math.log(HY_TARGET) / HY_FAST_DECAY, HY_W))
    window = np.exp(-t * deltas[None, :])
    return z.astype(np.float32), window.astype(np.float32)


@functools.lru_cache(maxsize=None)
def _rope_tables(L):
    half = RET_HEAD_DIM // 2
    quarter = half // 2
    t = np.arange(L)
    pos = np.stack([t // GRID_W, t % GRID_W], axis=0).astype(np.float64)
    freqs = ROPE_BASE ** (-np.arange(quarter, dtype=np.float64) / quarter)
    cos = np.zeros((L, RET_HEAD_DIM))
    sin = np.zeros((L, RET_HEAD_DIM))
    for a in range(2):
        ang = pos[a][:, None] * freqs[None, :]
        base = a * half
        cos[:, base:base + quarter] = np.cos(ang)
        cos[:, base + quarter:base + half] = np.cos(ang)
        sin[:, base:base + quarter] = -np.sin(ang)
        sin[:, base + quarter:base + half] = np.sin(ang)
    return (np.tile(cos, (1, RET_HEADS)).astype(np.float32), np.tile(sin, (1, RET_HEADS)).astype(np.float32))


@functools.lru_cache(maxsize=None)
def _head_block_ones(width):
    i = np.arange(width) // NA_HEAD_DIM
    return (i[:, None] == i[None, :]).astype(np.float32)


def _mod_kernel(c_ref, w_ref, b_ref, o_ref):
    s = _silu(c_ref[...])
    o_ref[...] = jnp.dot(s, w_ref[...], preferred_element_type=F32, precision=lax.Precision.HIGHEST) + b_ref[...]


def _modulation(cc, ada_w, ada_b):
    R = cc.shape[0]
    tn = 1024
    return pl.pallas_call(
        _mod_kernel,
        grid=(DEPTH, 3 * D_MODEL // tn),
        in_specs=[
            pl.BlockSpec((R, D_MODEL), lambda i, j: (0, 0)),
            pl.BlockSpec((None, D_MODEL, tn), lambda i, j: (i, 0, j)),
            pl.BlockSpec((None, 1, tn), lambda i, j: (i, 0, j)),
        ],
        out_specs=pl.BlockSpec((None, R, tn), lambda i, j: (i, 0, j)),
        out_shape=jax.ShapeDtypeStruct((DEPTH, R, 3 * D_MODEL), F32),
        compiler_params=_cparams("parallel", "parallel"),
        name="modulation",
    )(cc, ada_w, ada_b.reshape(DEPTH, 1, 3 * D_MODEL))


def _proj_kernel(*refs, with_out, with_in):
    refs = list(refs)
    if with_out:
        za_ref, zy_ref, zr_ref, x_ref, mod_o_ref, w_out_ref = refs[:6]
        del refs[:6]
    else:
        x_ref = refs.pop(0)
    if with_in:
        mod_i_ref, nw_ref, w_in_ref = refs[:3]
        del refs[:3]
    x = x_ref[...]
    if with_out:
        acc = _dot(za_ref[...], w_out_ref[0:NA_W, :])
        acc += _dot(zy_ref[...], w_out_ref[NA_W:NA_W + HY_W, :])
        acc += _dot(zr_ref[...], w_out_ref[NA_W + HY_W:MIX_W, :])
        x = x + mod_o_ref[:, 2 * D_MODEL:3 * D_MODEL] * acc
        refs.pop(0)[...] = x
    if with_in:
        u_ref = refs.pop(0)
        xn = x * lax.rsqrt(jnp.mean(x * x, axis=-1, keepdims=True) + EPS)
        shift = mod_i_ref[:, 0:D_MODEL]
        scale = mod_i_ref[:, D_MODEL:2 * D_MODEL]
        h = (xn * nw_ref[...] * (1.0 + scale) + shift).astype(BF16)
        tn = 1024
        for j in range(IN_W // tn):
            u_ref[:, j * tn:(j + 1) * tn] = _dot(h, w_in_ref[:, j * tn:(j + 1) * tn]).astype(u_ref.dtype)


def _projection(xf, mod_index, tm, out_args=None, in_args=None):
    R = xf.shape[0]
    row = lambda w: pl.BlockSpec((tm, w), lambda i: (i, 0))
    mod_spec = pl.BlockSpec((None, 1, 3 * D_MODEL), lambda i: (mod_index(i), 0, 0))
    args, in_specs, out_specs, out_shape = [], [], [], []
    if out_args is not None:
        za, zy, zr, mod_o, w_out_bf = out_args
        args += [za, zy, zr, xf, mod_o, w_out_bf]
        in_specs += [row(NA_W), row(HY_W), row(RET_W), row(D_MODEL), mod_spec,
                     _resident((MIX_W, D_MODEL), lambda i: (0, 0))]
        out_specs.append(row(D_MODEL))
        out_shape.append(jax.ShapeDtypeStruct((R, D_MODEL), F32))
    else:
        args.append(xf)
        in_specs.append(row(D_MODEL))
    if in_args is not None:
        mod_i, norm_w, w_in_bf = in_args
        args += [mod_i, norm_w, w_in_bf]
        in_specs += [mod_spec, _resident((1, D_MODEL), lambda i: (0, 0)), _resident((D_MODEL, IN_W), lambda i: (0, 0))]
        out_specs.append(row(IN_W))
        out_shape.append(jax.ShapeDtypeStruct((R, IN_W), BF16))
    outs = pl.pallas_call(
        functools.partial(_proj_kernel, with_out=out_args is not None, with_in=in_args is not None),
        grid=(R // tm,),
        in_specs=in_specs,
        out_specs=out_specs,
        out_shape=out_shape,
        compiler_params=_cparams("parallel"),
        name="projection",
    )(*args)
    return outs if len(outs) > 1 else outs[0]


def _head_rms(x, ones_bf, gain):
    ss = _dot((x * x).astype(BF16), ones_bf)
    return x * lax.rsqrt(ss * (1.0 / NA_HEAD_DIM) + EPS) * gain


def _pair_masks():
    lane = lax.broadcasted_iota(jnp.int32, (1, 2 * NA_HEAD_DIM), 1)
    return lane < NA_HEAD_DIM


def _attend_heads(q, key_parts, val_parts, bias_fn):
    first = _pair_masks()
    outs = []
    for pair in range(NA_HEADS // 2):
        lo = pair * 2 * NA_HEAD_DIM
        hi = lo + 2 * NA_HEAD_DIM
        qp = q[:, lo:hi]
        o_pair = None
        for sub in range(2):
            h = 2 * pair + sub
            sel = first if sub == 0 else jnp.logical_not(first)
            qm = jnp.where(sel, qp, 0.0).astype(BF16)
            scores = []
            for part, kp in enumerate(key_parts):
                s = _dot_nt(qm, kp[:, lo:hi])
                b = bias_fn(h, part)
                if b is not None:
                    s = s + b
                scores.append(s)
            m = scores[0].max(axis=-1, keepdims=True)
            for s in scores[1:]:
                m = jnp.maximum(m, s.max(axis=-1, keepdims=True))
            denom = None
            acc = None
            for s, vp in zip(scores, val_parts):
                p = jnp.exp(s - m)
                ps = p.sum(axis=-1, keepdims=True)
                denom = ps if denom is None else denom + ps
                pv = _dot(p.astype(BF16), vp[:, lo:hi])
                acc = pv if acc is None else acc + pv
            o_h = acc * (1.0 / denom)
            o_pair = o_h if o_pair is None else jnp.where(first, o_pair, o_h)
        outs.append(o_pair)
    return jnp.concatenate(outs, axis=-1)


def _na_kernel(q_ref, k_ref, v_ref, g_ref, kc_ref, vc_ref, qg_ref, kg_ref, ones_ref, rpb_ref, o_ref,
               kn_scr, kcn_scr, vat_scr, vbt_scr, vcat_scr, vcbt_scr, bias_scr, mask_scr, s0_scr, s1_scr,
               p0_scr, p1_scr, qm_scr, ot_scr, *, rows):
    batch = pl.program_id(0)
    grp = pl.program_id(1)
    ones_bf = ones_ref[...]
    tq = NA_G * GRID_W
    nwin = NA_WIN * GRID_W
    Lc = kc_ref.shape[0]
    pair_w = 2 * NA_HEAD_DIM
    kt = NA_KT
    first = lax.broadcasted_iota(jnp.int32, (1, pair_w), 1) < NA_HEAD_DIM

    @pl.when((batch == 0) & (grp == 0))
    def _():
        ck = lax.broadcasted_iota(jnp.int32, (GRID_W, pair_w), 0)
        lane = lax.broadcasted_iota(jnp.int32, (GRID_W, pair_w), 1)
        cq = lane % GRID_W
        col_start = jnp.clip(cq - NA_KW // 2, 0, GRID_W - NA_KW)
        col_ok = (ck >= col_start) & (ck < col_start + NA_KW)
        left = lane < GRID_W

        def body(i, carry):
            for h in range(NA_HEADS):
                v0 = jnp.broadcast_to(rpb_ref[i + 1, h:h + 1, :], (GRID_W, pair_w))
                v1 = jnp.broadcast_to(rpb_ref[i, h:h + 1, :], (GRID_W, pair_w))
                t0 = pltpu.roll(v0, pair_w - (NA_KW - 1), 1, stride=1, stride_axis=0)
                t1 = pltpu.roll(v1, GRID_W - (NA_KW - 1), 1, stride=1, stride_axis=0)
                bias_scr[i, h] = jnp.where(col_ok, jnp.where(left, t0, t1), NEG_INF)
            return carry

        lax.fori_loop(0, NA_DR_NUM, body, 0)

    @pl.when(grp == 0)
    def _():
        kgain = kg_ref[...]
        own_a = (lax.broadcasted_iota(jnp.int32, (NA_W, 1), 0) % pair_w) < NA_HEAD_DIM
        for src, dst, vat, vbt, vsrc in ((k_ref, kn_scr, vat_scr, vbt_scr, v_ref),
                                         (kc_ref, kcn_scr, vcat_scr, vcbt_scr, vc_ref)):
            for i in range(src.shape[0] // kt):
                sl = slice(i * kt, (i + 1) * kt)
                dst[sl, :] = _head_rms(src[sl, :].astype(F32), ones_bf, kgain).astype(BF16)
                vt = vsrc[sl, :].astype(F32).T
                vat[i] = jnp.where(own_a, vt, 1.0).astype(BF16)
                vbt[i] = jnp.where(own_a, 1.0, vt).astype(BF16)

    n_kt = (nwin + Lc) // kt
    rows_per_tile = kt // GRID_W
    n_ch = (nwin + Lc) // GRID_W
    n_items = NA_GPS * NA_HEADS

    s_bufs = (s0_scr, s1_scr)
    p_bufs = (p0_scr, p1_scr)
    row_zero = jnp.minimum(grp, 0)

    def rows_of(start, size):
        return pl.ds(pl.multiple_of(row_zero + start, GRID_W), size)

    def head_lanes(h):
        return slice((h // 2) * pair_w, (h // 2 + 1) * pair_w)

    key0, tile0, dr_base = [], [], []
    for gi in range(NA_GPS):
        r0 = (grp * NA_GPS + gi) * NA_G
        ws = jnp.clip(r0 - NA_KH // 2, 0, rows - NA_WIN)
        key0.append(pl.multiple_of(ws * GRID_W, kt))
        tile0.append(ws // rows_per_tile)
        dr_base.append(ws - r0 + (NA_KH - 1) - NA_DR_MIN)
        w_id = lax.broadcasted_iota(jnp.int32, (16, tq), 0)
        g_id = lax.broadcasted_iota(jnp.int32, (16, tq), 1) // GRID_W
        lo = jnp.clip(r0 + g_id - NA_KH // 2, 0, rows - NA_KH) - ws
        mask_scr[gi] = jnp.where((w_id >= lo) & (w_id < lo + NA_KH), 0.0, NEG_INF).astype(F32)
        q = _head_rms(q_ref[gi * tq:(gi + 1) * tq, :].astype(F32), ones_bf,
                      qg_ref[...] * (NA_HEAD_DIM ** -0.5 * LOG2E))
        for h in range(NA_HEADS):
            sel = first if h % 2 == 0 else jnp.logical_not(first)
            qm_scr[gi * NA_HEADS + h, rows_of(0, tq), :] = jnp.where(sel, q[:, head_lanes(h)], 0.0).astype(BF16)

    col_max = {}

    def qk_piece(n, j):
        gi, h = divmod(n, NA_HEADS)
        last = j == n_kt - 1
        keys = kcn_scr[:, head_lanes(h)] if last else kn_scr[pl.ds(key0[gi] + j * kt, kt), head_lanes(h)]
        sv = _dot_nt(keys, qm_scr[n, rows_of(0, tq), :])
        for dw in range(rows_per_tile):
            sw = sv[dw * GRID_W:(dw + 1) * GRID_W, :]
            if not last:
                w = j * rows_per_tile + dw
                bias = jnp.concatenate([bias_scr[dr_base[gi] + (w - g), h] for g in range(0, NA_G, 2)], axis=-1)
                sw = sw + bias + mask_scr[gi, w:w + 1, :]
            s_bufs[n % 2][rows_of(j * kt + dw * GRID_W, GRID_W), :] = sw
            part = sw.reshape(GRID_W // 8, 8, tq).max(axis=0)
            col_max[n] = part if (j == 0 and dw == 0) else jnp.maximum(col_max[n], part)

    def exp_chunk(n, c):
        if c == 0:
            col_max[n] = col_max[n].max(axis=0, keepdims=True)
        rows_c = rows_of(c * GRID_W, GRID_W)
        p_bufs[n % 2][rows_c, :] = jnp.exp2(s_bufs[n % 2][rows_c, :] - col_max[n]).astype(BF16)
        if c == n_ch - 1:
            del col_max[n]

    def pv(n):
        gi, h = divmod(n, NA_HEADS)
        vt, vct = (vat_scr, vcat_scr) if h % 2 == 0 else (vbt_scr, vcbt_scr)
        ot = _dot(vct[0, head_lanes(h), :], p_bufs[n % 2][rows_of(nwin, Lc), :])
        for j in range(n_kt - 1):
            ot += _dot(vt[tile0[gi] + j, head_lanes(h), :], p_bufs[n % 2][rows_of(j * kt, kt), :])
        own, den = (ot[0:NA_HEAD_DIM], ot[NA_HEAD_DIM:]) if h % 2 == 0 else (ot[NA_HEAD_DIM:], ot[0:NA_HEAD_DIM])
        ot_scr[rows_of(n * NA_HEAD_DIM, NA_HEAD_DIM), :] = own * (1.0 / den)
        if h % 2 == 1:
            o_pair = ot_scr[rows_of((n - 1) * NA_HEAD_DIM, pair_w), :].T
            q_rows = slice(gi * tq, (gi + 1) * tq)
            gate = _silu(g_ref[q_rows, head_lanes(h)].astype(F32))
            o_ref[q_rows, head_lanes(h)] = (o_pair * gate).astype(o_ref.dtype)

    for t in range(n_items + 2):
        for i in range(n_ch):
            if 0 <= t - 1 < n_items:
                exp_chunk(t - 1, i)
            if i % 4 == 0 and t < n_items and i // 4 < n_kt:
                qk_piece(t, i // 4)
            if i == 2 and 0 <= t - 2:
                pv(t - 2)


def _neighborhood_attention(u, uc, q_gain, k_gain, rpb, B, L, Lc):
    rows = L // GRID_W
    ngrp = rows // (NA_G * NA_GPS)
    tq = NA_G * GRID_W
    tstep = tq * NA_GPS
    ones = jnp.asarray(_head_block_ones(NA_W)).astype(BF16)
    qg = jnp.tile(q_gain, NA_HEADS)[None, :]
    kg = jnp.tile(k_gain, NA_HEADS)[None, :]
    n_dr = 2 * NA_KH - 1
    rpb_t = jnp.transpose(rpb.astype(F32)[:, :, ::-1] * LOG2E, (1, 0, 2))
    lo_pad = 1 - NA_DR_MIN
    rpb_t = jnp.pad(rpb_t, ((lo_pad, NA_DR_NUM + 1 - lo_pad - n_dr), (0, 0), (0, 2 * GRID_W - (2 * NA_KW - 1))))
    return pl.pallas_call(
        functools.partial(_na_kernel, rows=rows),
        grid=(B, ngrp),
        in_specs=[
            pl.BlockSpec((tstep, NA_W), lambda b, g: (b * ngrp + g, 0)),
            pl.BlockSpec((L, NA_W), lambda b, g: (b, 1)),
            pl.BlockSpec((L, NA_W), lambda b, g: (b, 2)),
            pl.BlockSpec((tstep, NA_W), lambda b, g: (b * ngrp + g, 3)),
            pl.BlockSpec((Lc, NA_W), lambda b, g: (b, 1)),
            pl.BlockSpec((Lc, NA_W), lambda b, g: (b, 2)),
            _resident((1, NA_W), lambda b, g: (0, 0)),
            _resident((1, NA_W), lambda b, g: (0, 0)),
            _resident((NA_W, NA_W), lambda b, g: (0, 0)),
            _resident((NA_DR_NUM + 1, NA_HEADS, 2 * GRID_W), lambda b, g: (0, 0, 0)),
        ],
        out_specs=pl.BlockSpec((tstep, NA_W), lambda b, g: (b * ngrp + g, 0)),
        out_shape=jax.ShapeDtypeStruct((B * L, NA_W), BF16),
        scratch_shapes=[
            pltpu.VMEM((L, NA_W), BF16), pltpu.VMEM((Lc, NA_W), BF16),
            pltpu.VMEM((L // NA_KT, NA_W, NA_KT), BF16), pltpu.VMEM((L // NA_KT, NA_W, NA_KT), BF16),
            pltpu.VMEM((Lc // NA_KT, NA_W, NA_KT), BF16), pltpu.VMEM((Lc // NA_KT, NA_W, NA_KT), BF16),
            pltpu.VMEM((NA_DR_NUM, NA_HEADS, GRID_W, 2 * GRID_W), F32),
            pltpu.VMEM((NA_GPS, 16, tq), F32),
            pltpu.VMEM((NA_WIN * GRID_W + Lc, tq), F32), pltpu.VMEM((NA_WIN * GRID_W + Lc, tq), F32),
            pltpu.VMEM((NA_WIN * GRID_W + Lc, tq), BF16), pltpu.VMEM((NA_WIN * GRID_W + Lc, tq), BF16),
            pltpu.VMEM((NA_GPS * NA_HEADS, tq, 2 * NA_HEAD_DIM), BF16), pltpu.VMEM((NA_GPS * NA_W, tq), F32),
        ],
        compiler_params=_cparams("arbitrary", "arbitrary"),
        name="neighborhood_attention",
    )(u, u, u, u, uc, uc, qg, kg, ones, rpb_t)


def _ctx_attn_kernel(q_ref, k_ref, v_ref, g_ref, qg_ref, kg_ref, ones_ref, o_ref):
    ones_bf = ones_ref[...]
    q = _head_rms(q_ref[...].astype(F32), ones_bf, qg_ref[...] * (NA_HEAD_DIM ** -0.5))
    k = _head_rms(k_ref[...].astype(F32), ones_bf, kg_ref[...]).astype(BF16)
    o = _attend_heads(q, [k], [v_ref[...]], lambda h, part: None)
    o_ref[...] = (o * _silu(g_ref[...].astype(F32))).astype(o_ref.dtype)


def _context_attention(uc, q_gain, k_gain, B, Lc):
    ones = jnp.asarray(_head_block_ones(NA_W)).astype(BF16)
    qg = jnp.tile(q_gain, NA_HEADS)[None, :]
    kg = jnp.tile(k_gain, NA_HEADS)[None, :]
    return pl.pallas_call(
        _ctx_attn_kernel,
        grid=(B,),
        in_specs=[
            pl.BlockSpec((Lc, NA_W), lambda b: (b, 0)),
            pl.BlockSpec((Lc, NA_W), lambda b: (b, 1)),
            pl.BlockSpec((Lc, NA_W), lambda b: (b, 2)),
            pl.BlockSpec((Lc, NA_W), lambda b: (b, 3)),
            _resident((1, NA_W), lambda b: (0, 0)),
            _resident((1, NA_W), lambda b: (0, 0)),
            _resident((NA_W, NA_W), lambda b: (0, 0)),
        ],
        out_specs=pl.BlockSpec((Lc, NA_W), lambda b: (b, 0)),
        out_shape=jax.ShapeDtypeStruct((B * Lc, NA_W), BF16),
        compiler_params=_cparams("parallel"),
        name="context_attention",
    )(uc, uc, uc, uc, qg, kg, ones)


def _alt_sign(shape):
    row = lax.broadcasted_iota(jnp.int32, shape, 0)
    return (1 - 2 * (row & 1)).astype(F32)


def _lane_halves(x):
    return [x[:, h * 128:(h + 1) * 128] for h in range(HY_W // 128)]


def _even_odd_rows(scr, j0, n):
    halves = range(HY_W // 128)
    even = jnp.concatenate([scr[h, pl.ds(2 * j0, n, stride=2), :] for h in halves], axis=1)
    odd = jnp.concatenate([scr[h, pl.ds(2 * j0 + 1, n, stride=2), :] for h in halves], axis=1)
    return even, odd


def _filter_kernel(z_ref, win_ref, w1_ref, b1_ref, w2_ref, b2_ref, w3_ref, sf_ref, ce_ref, se_ref, co_ref, so_ref,
                   hr_ref, hi_ref, hm_ref, h_scr, a_scr, d_scr, *, L):
    hp = lax.Precision.HIGHEST
    dot = lambda a, b: jnp.dot(a, b, preferred_element_type=F32, precision=hp)

    @pl.when(pl.program_id(0) == 0)
    def _():
        h = jnp.sin(sf_ref[0:1, :] * (dot(z_ref[...], w1_ref[...]) + b1_ref[...]))
        h_scr[...] = jnp.sin(sf_ref[1:2, :] * (dot(h, w2_ref[...]) + b2_ref[...]))

    taps = dot(h_scr[...], w3_ref[...])
    win = win_ref[...]
    M = L // 2
    row = lax.broadcasted_iota(jnp.int32, (L, HY_W), 0)
    inv_n = 1.0 / (2 * L)
    hf = jnp.concatenate([taps[:, 0:HY_W], taps[:, 2 * HY_W:3 * HY_W]], axis=0) * win
    hb = jnp.concatenate([taps[:, HY_W:2 * HY_W], taps[:, 3 * HY_W:4 * HY_W]], axis=0) * win
    hb = jnp.where(row == 0, 0.0, hb)
    norm = jnp.sum(jnp.abs(hf), axis=0, keepdims=True) + jnp.sum(jnp.abs(hb), axis=0, keepdims=True)
    inv = 1.0 / norm
    for h, (a_half, d_half) in enumerate(zip(_lane_halves((hf + hb) * inv), _lane_halves((hf - hb) * inv))):
        a_scr[h] = a_half
        d_scr[h] = d_half
    a_e, a_o = _even_odd_rows(a_scr, 0, M)
    d_e, d_o = _even_odd_rows(d_scr, 0, M)
    pc = _dot(ce_ref[...], a_e.astype(BF16))
    qc = _dot(co_ref[...], a_o.astype(BF16))
    ps = _dot(se_ref[...], d_e.astype(BF16))
    qs = _dot(so_ref[...], d_o.astype(BF16))
    wgt = jnp.where(lax.broadcasted_iota(jnp.int32, (M, HY_W), 0) == 0, inv_n, 2.0 * inv_n)
    hr_ref[0] = (pc + qc) * wgt
    hr_ref[1] = (pc - qc) * wgt
    hi_ref[0] = -(ps + qs) * wgt
    hi_ref[1] = (ps - qs) * wgt
    sign = _alt_sign((M, HY_W))
    mid_r = jnp.sum(a_e * sign, axis=0, keepdims=True) * (2.0 * inv_n)
    mid_i = -jnp.sum(d_o * sign, axis=0, keepdims=True) * (2.0 * inv_n)
    hm_ref[0] = mid_r
    hm_ref[1] = mid_i


def _block_diag2(w):
    z = jnp.zeros_like(w)
    return jnp.concatenate([jnp.concatenate([w, z], axis=1), jnp.concatenate([z, w], axis=1)], axis=0)


def _hyena_filter(L, w1, b1, w2, b2, w3, sin_freq, dft_bf):
    M = L // 2
    z, window = _filter_features(L)
    z2 = np.concatenate([z[:L // 2], z[L // 2:]], axis=1)
    w1d = _block_diag2(jnp.pad(w1, ((0, HY_EMB_PAD - HY_EMB), (0, 0))))
    w2d = _block_diag2(w2)
    w3d = jnp.stack([_block_diag2(w3[:, o * 2 * HY_W:(o + 1) * 2 * HY_W]) for o in range(2)])
    pair = lambda v: jnp.tile(v, (1, 2))
    full = lambda *shape: _resident(shape, lambda o: (0,) * len(shape))
    spectrum = pl.BlockSpec((None, 2, M, HY_W), lambda o: (o, 0, 0, 0))
    return pl.pallas_call(
        functools.partial(_filter_kernel, L=L),
        grid=(2,),
        in_specs=[
            full(M, 2 * HY_EMB_PAD), full(L, HY_W), full(2 * HY_EMB_PAD, 2 * HY_FFN), full(1, 2 * HY_FFN),
            full(2 * HY_FFN, 2 * HY_FFN), full(1, 2 * HY_FFN),
            pl.BlockSpec((None, 2 * HY_FFN, 4 * HY_W), lambda o: (o, 0, 0)),
            full(2, 2 * HY_FFN), full(M, M), full(M, M), full(M, M), full(M, M),
        ],
        out_specs=[spectrum, spectrum, pl.BlockSpec((None, 2, 1, HY_W), lambda o: (o, 0, 0, 0))],
        out_shape=[jax.ShapeDtypeStruct((2, 2, M, HY_W), F32), jax.ShapeDtypeStruct((2, 2, M, HY_W), F32),
                   jax.ShapeDtypeStruct((2, 2, 1, HY_W), F32)],
        scratch_shapes=[pltpu.VMEM((M, 2 * HY_FFN), F32), pltpu.VMEM((HY_W // 128, L, 128), F32),
                        pltpu.VMEM((HY_W // 128, L, 128), F32)],
        compiler_params=_cparams("arbitrary"),
        name="hyena_filter",
    )(jnp.asarray(z2), jnp.asarray(window), w1d, pair(b1[None, :]), w2d, pair(b2[None, :]), w3d, pair(sin_freq),
      *dft_bf[:4])


def _hyena_kernel(v_ref, x1_ref, x2_ref, g_ref, cw_ref, cb_ref, skip_ref, ce_ref, se_ref, co_ref, so_ref,
                  cot_ref, sot_ref, hr_ref, hi_ref, hm_ref, o_ref,
                  a_scr, x_scr, y_scr, ae_scr, ao_scr, er_scr, ei_scr, dr_scr, di_scr, *, L):
    M = L // 2
    ck = min(HY_ROWS, L)
    cm = min(HY_ROWS, M)
    halo = 16
    chunks = [(t0, t0 + ck) for t0 in range(0, L, ck)]
    mchunks = [(j0, j0 + cm) for j0 in range(0, M, cm)]
    sign = _alt_sign((cm, HY_W))

    def split_store(scr, t0, t1, val):
        for h, part in enumerate(_lane_halves(val)):
            scr[h, t0:t1, :] = part

    def joined(scr, t0, t1):
        return jnp.concatenate([scr[h, t0:t1, :] for h in range(HY_W // 128)], axis=1)

    def short_conv(ref, j, t0, t1):
        lo, hi = max(t0 - halo, 0), min(t1 + halo, L)
        u = ref[lo:hi, :].astype(F32)
        row = lo + lax.broadcasted_iota(jnp.int32, (hi - lo, HY_W), 0)
        prev = jnp.where(row == 0, 0.0, pltpu.roll(u, 1, 0))
        nxt = jnp.where(row == L - 1, 0.0, pltpu.roll(u, hi - lo - 1, 0))
        w = cw_ref[:, j * HY_W:(j + 1) * HY_W]
        z = prev * w[0:1, :] + u * w[1:2, :] + nxt * w[2:3, :] + cb_ref[:, j * HY_W:(j + 1) * HY_W]
        return z[t0 - lo:t1 - lo, :]

    def long_conv(order):
        mid_r = jnp.zeros((1, HY_W), F32)
        mid_i = jnp.zeros((1, HY_W), F32)
        for j0, j1 in mchunks:
            even, odd = _even_odd_rows(a_scr, j0, cm)
            ae_scr[j0:j1, :] = even.astype(BF16)
            ao_scr[j0:j1, :] = odd.astype(BF16)
            mid_r += jnp.sum(even * sign, axis=0, keepdims=True)
            mid_i += jnp.sum(odd * sign, axis=0, keepdims=True)
        for k0, k1 in mchunks:
            pc = _dot(ce_ref[k0:k1, :], ae_scr[...])
            qc = _dot(co_ref[k0:k1, :], ao_scr[...])
            ps = _dot(se_ref[k0:k1, :], ae_scr[...])
            qs = _dot(so_ref[k0:k1, :], ao_scr[...])
            zr, zi = [], []
            for mirror, (ur, us) in enumerate(((pc + qc, ps + qs), (pc - qc, qs - ps))):
                hr = hr_ref[order, mirror, k0:k1, :]
                hi = hi_ref[order, mirror, k0:k1, :]
                zr.append(ur * hr + us * hi)
                zi.append(us * hr - ur * hi)
            er_scr[k0:k1, :] = (zr[0] + zr[1]).astype(BF16)
            ei_scr[k0:k1, :] = (zi[0] - zi[1]).astype(BF16)
            dr_scr[k0:k1, :] = (zr[0] - zr[1]).astype(BF16)
            di_scr[k0:k1, :] = (zi[0] + zi[1]).astype(BF16)
        hm_r, hm_i = hm_ref[order, 0], hm_ref[order, 1]
        zm_r = mid_r * hm_r + mid_i * hm_i
        zm_i = mid_i * hm_r - mid_r * hm_i
        for j0, j1 in mchunks:
            y_even = _dot(ce_ref[j0:j1, :], er_scr[...]) + _dot(se_ref[j0:j1, :], ei_scr[...]) + sign * zm_r
            y_odd = _dot(cot_ref[j0:j1, :], dr_scr[...]) + _dot(sot_ref[j0:j1, :], di_scr[...]) + sign * zm_i
            for h in range(HY_W // 128):
                y_scr[h, pl.ds(2 * j0, cm, stride=2), :] = y_even[:, h * 128:(h + 1) * 128]
                y_scr[h, pl.ds(2 * j0 + 1, cm, stride=2), :] = y_odd[:, h * 128:(h + 1) * 128]

    for t0, t1 in chunks:
        split_store(a_scr, t0, t1, short_conv(v_ref, 0, t0, t1))
        x_scr[t0:t1, :] = short_conv(x1_ref, 1, t0, t1)
    long_conv(0)
    for t0, t1 in chunks:
        conv = joined(y_scr, t0, t1) + joined(a_scr, t0, t1) * skip_ref[0:1, :]
        split_store(a_scr, t0, t1, x_scr[t0:t1, :] * conv)
    long_conv(1)
    for t0, t1 in chunks:
        conv = joined(y_scr, t0, t1) + joined(a_scr, t0, t1) * skip_ref[1:2, :]
        y = short_conv(x2_ref, 2, t0, t1) * conv
        o_ref[t0:t1, :] = (y * _silu(g_ref[t0:t1, :].astype(F32))).astype(o_ref.dtype)


def _hyena(u, conv_w, conv_b, skip, dft_bf, hr, hi, hm, B, L):
    M = L // 2
    col = lambda j: pl.BlockSpec((L, CB), lambda b: (b, j))
    const = lambda *shape: _resident(shape, lambda b: (0,) * len(shape))
    halves = HY_W // 128
    return pl.pallas_call(
        functools.partial(_hyena_kernel, L=L),
        grid=(B,),
        in_specs=[
            col(CB_HY_V), col(CB_HY_X1), col(CB_HY_X2), col(CB_HY_G),
            const(3, 3 * HY_W), const(1, 3 * HY_W), const(2, HY_W),
            const(M, M), const(M, M), const(M, M), const(M, M), const(M, M), const(M, M),
            const(2, 2, M, HY_W), const(2, 2, M, HY_W), const(2, 2, 1, HY_W),
        ],
        out_specs=pl.BlockSpec((L, HY_W), lambda b: (b, 0)),
        out_shape=jax.ShapeDtypeStruct((B * L, HY_W), BF16),
        scratch_shapes=[
            pltpu.VMEM((halves, L, 128), F32), pltpu.VMEM((L, HY_W), F32), pltpu.VMEM((halves, L, 128), F32),
            pltpu.VMEM((M, HY_W), BF16), pltpu.VMEM((M, HY_W), BF16),
            pltpu.VMEM((M, HY_W), BF16), pltpu.VMEM((M, HY_W), BF16),
            pltpu.VMEM((M, HY_W), BF16), pltpu.VMEM((M, HY_W), BF16),
        ],
        compiler_params=_cparams("parallel"),
        name="hyena",
    )(u, u, u, u, conv_w, conv_b[None, :], skip, *dft_bf, hr, hi, hm)


def _ret_kernel(*refs, L, Lc, has_init):
    if has_init:
        (q_ref, k_ref, v_ref, g_ref, kc_ref, vc_ref, cos_ref, sin_ref, rate_ref, rrow_ref, ones_ref,
         o_ref, q_scr, k_scr, sf_scr, sb_scr, r_scr) = refs
    else:
        (q_ref, k_ref, v_ref, g_ref, cos_ref, sin_ref, rate_ref, rrow_ref, ones_ref,
         o_ref, q_scr, k_scr, sf_scr, sb_scr, r_scr) = refs
    C = min(RET_CHUNK, L)
    nch = L // C
    W = RET_W
    quarter = RET_HEAD_DIM // 4
    block = ones_ref[...]
    lg = -jnp.exp(rate_ref[...])
    lg_f, lg_b = lg[0:1, :], lg[1:2, :]

    lane = lax.broadcasted_iota(jnp.int32, (L, W), 1)
    first_quarter = (lane % (2 * quarter)) < quarter

    def rope(a):
        swapped = jnp.where(first_quarter, pltpu.roll(a, W - quarter, 1), pltpu.roll(a, quarter, 1))
        return a * cos_ref[...] + swapped * sin_ref[...]

    if has_init:
        q_scr[...] = rope(q_ref[...].astype(F32)).astype(BF16)
        k_scr[...] = rope(k_ref[...].astype(F32) * (RET_HEAD_DIM ** -0.5)).astype(BF16)
    else:
        q_scr[...] = q_ref[...]
        k_scr[...] = (k_ref[...].astype(F32) * (RET_HEAD_DIM ** -0.5)).astype(BF16)

    def decays(n_rows):
        pos = lax.broadcasted_iota(jnp.int32, (n_rows, W), 0).astype(F32)
        return (jnp.exp(lg_f * (n_rows - 1.0 - pos)),
                jnp.exp(lg_b * pos))

    def chunk_states(k_bf, v_bf, zf, zb):
        kf = (k_bf.astype(F32) * zf).astype(BF16)
        kb = (k_bf.astype(F32) * zb).astype(BF16)
        return _dot_tn(kf, v_bf) * block, _dot_tn(kb, v_bf) * block

    zeta_f, zeta_b = decays(C)
    if has_init:
        zc_f, zc_b = decays(Lc)
        kc = (kc_ref[...].astype(F32) * (RET_HEAD_DIM ** -0.5)).astype(BF16)
        s0_f, s0_b = chunk_states(kc, vc_ref[...], zc_f, zc_b)
    else:
        s0_f = jnp.zeros((W, W), F32)
        s0_b = jnp.zeros((W, W), F32)

    for n in range(nch):
        kv_f, kv_b = chunk_states(k_scr[n * C:(n + 1) * C, :], v_ref[n * C:(n + 1) * C, :], zeta_f, zeta_b)
        sf_scr[n] = kv_f
        sb_scr[n] = kv_b
    dec_f = jnp.exp(lg_f * float(C))
    dec_b = jnp.exp(lg_b * float(C))
    state = s0_f
    for n in range(nch):
        kv = sf_scr[n]
        sf_scr[n] = state
        state = dec_f * state + kv
    state = s0_b
    for n in range(nch - 1, -1, -1):
        kv = sb_scr[n]
        sb_scr[n] = state
        state = dec_b * state + kv

    posc = lax.broadcasted_iota(jnp.int32, (C, W), 0).astype(F32)
    xi_f = jnp.exp(lg_f * (posc + 1.0))
    xi_b = jnp.exp(lg_b * (float(C) - posc))
    diff = (lax.broadcasted_iota(jnp.int32, (C, C), 0) - lax.broadcasted_iota(jnp.int32, (C, C), 1)).astype(F32)
    lane_c = lax.broadcasted_iota(jnp.int32, (1, W), 1)
    dmask = []
    for h in range(RET_HEADS):
        rf = -jnp.exp(rrow_ref[h:h + 1, 0:C])
        rb = -jnp.exp(rrow_ref[RET_HEADS + h:RET_HEADS + h + 1, 0:C])
        dmask.append(jnp.where(diff >= 0, jnp.exp(rf * jnp.maximum(diff, 0.0)), 0.0)
                     + jnp.where(diff <= 0, jnp.exp(rb * jnp.maximum(-diff, 0.0)), 0.0))

    for n in range(nch):
        qn = q_scr[n * C:(n + 1) * C, :]
        kn = k_scr[n * C:(n + 1) * C, :]
        vn = v_ref[n * C:(n + 1) * C, :]
        qf = qn.astype(F32)
        lhs = [(qf * xi_f).astype(BF16), (qf * xi_b).astype(BF16)]
        rhs = [sf_scr[n].astype(BF16), sb_scr[n].astype(BF16)]
        zero = jnp.zeros_like(qn)
        for h in range(RET_HEADS):
            sel = (lane_c // RET_HEAD_DIM) == h
            s = _dot_nt(jnp.where(sel, qn, zero), kn) * dmask[h]
            lhs.append(s.astype(BF16))
            rhs.append(jnp.where(sel, vn, zero))
        r_scr[n * C:(n + 1) * C, :] = _dot(jnp.concatenate(lhs, axis=1), jnp.concatenate(rhs, axis=0))

    r = r_scr[...]
    ss = _dot((r * r).astype(BF16), block.astype(BF16))
    rn = r * lax.rsqrt(ss * (1.0 / RET_HEAD_DIM) + EPS)
    o_ref[...] = (rn * _silu(g_ref[...].astype(F32))).astype(o_ref.dtype)


def _retention(u, uc, ret_log_rate, B, L, Lc, has_init):
    C = min(RET_CHUNK, L)
    nch = L // C
    col = lambda j: pl.BlockSpec((L, CB), lambda b: (b, j))
    ccol = lambda j: pl.BlockSpec((Lc, CB), lambda b: (b, j))
    cos, sin = _rope_tables(L)
    rate_lane = jnp.repeat(ret_log_rate, RET_HEAD_DIM, axis=1)
    rate_row = jnp.broadcast_to(ret_log_rate.reshape(2 * RET_HEADS, 1), (2 * RET_HEADS, RET_W))
    ones = jnp.asarray(_head_block_ones(RET_W))
    in_specs = [col(CB_RE_Q), col(CB_RE_K), col(CB_RE_V), col(CB_RE_G)]
    args = [u, u, u, u]
    if has_init:
        in_specs += [ccol(CB_RE_K), ccol(CB_RE_V)]
        args += [uc, uc]
    in_specs += [
        _resident((L, RET_W), lambda b: (0, 0)),
        _resident((L, RET_W), lambda b: (0, 0)),
        _resident((2, RET_W), lambda b: (0, 0)),
        _resident((2 * RET_HEADS, RET_W), lambda b: (0, 0)),
        _resident((RET_W, RET_W), lambda b: (0, 0)),
    ]
    args += [jnp.asarray(cos), jnp.asarray(sin), rate_lane, rate_row, ones]
    return pl.pallas_call(
        functools.partial(_ret_kernel, L=L, Lc=Lc, has_init=has_init),
        grid=(B,),
        in_specs=in_specs,
        out_specs=pl.BlockSpec((L, RET_W), lambda b: (b, 0)),
        out_shape=jax.ShapeDtypeStruct((B * L, RET_W), BF16),
        scratch_shapes=[
            pltpu.VMEM((L, RET_W), BF16), pltpu.VMEM((L, RET_W), BF16),
            pltpu.VMEM((nch, RET_W, RET_W), F32), pltpu.VMEM((nch, RET_W, RET_W), F32),
            pltpu.VMEM((L, RET_W), F32),
        ],
        compiler_params=_cparams("parallel"),
        name="retention",
    )(*args)


def kernel(x, c, ctx, c_ctx, norm_w, ada_w, ada_b, w_in, w_out, na_q_gain, na_k_gain, na_rpb, hy_conv_w, hy_conv_b,
           hy_w1, hy_b1, hy_w2, hy_b2, hy_w3, hy_sin_freq, hy_skip, ret_log_rate):
    B, L, D = x.shape
    Lc = ctx.shape[1]
    assert D == D_MODEL and L % (GRID_W * NA_G * NA_GPS) == 0 and L // GRID_W >= NA_WIN
    assert NA_G == NA_KT // GRID_W and (L // GRID_W - NA_WIN) % NA_G == 0 and Lc == NA_KT

    n_cond = 16
    cc = jnp.concatenate([c, c_ctx[None, :], jnp.zeros((n_cond - B - 1, D), F32)], axis=0)
    mods = _modulation(cc, ada_w, ada_b)

    dft_x = tuple(jnp.asarray(a).astype(BF16) for a in _dft_matrices(L))
    dft_c = tuple(jnp.asarray(a).astype(BF16) for a in _dft_matrices(Lc))

    tm_x = 512
    xf = x.reshape(B * L, D)
    cf = ctx.reshape(B * Lc, D)
    x_mod = lambda i: i // (L // tm_x)
    c_mod = lambda i: B

    def layer_params(i):
        return mods[i].reshape(n_cond, 1, 3 * D), norm_w[i][None, :], w_in[i].astype(BF16)

    mod, nw, w_in_bf = layer_params(0)
    u = _projection(xf, x_mod, tm_x, in_args=(mod, nw, w_in_bf))
    uc = _projection(cf, c_mod, Lc, in_args=(mod, nw, w_in_bf))
    for i in range(DEPTH):
        w_out_bf = w_out[i].astype(BF16)
        filt = (hy_w1[i], hy_b1[i], hy_w2[i], hy_b2[i], hy_w3[i], hy_sin_freq[i])
        last = i == DEPTH - 1
        nxt = None if last else layer_params(i + 1)

        za = _neighborhood_attention(u, uc, na_q_gain[i], na_k_gain[i], na_rpb[i], B, L, Lc)
        hr, hi, hm = _hyena_filter(L, *filt, dft_x)
        zy = _hyena(u, hy_conv_w[i], hy_conv_b[i], hy_skip[i], dft_x, hr, hi, hm, B, L)
        zr = _retention(u, uc, ret_log_rate[i], B, L, Lc, True)
        if not last:
            zac = _context_attention(uc, na_q_gain[i], na_k_gain[i], B, Lc)
            hrc, hic, hmc = _hyena_filter(Lc, *filt, dft_c)
            zyc = _hyena(uc, hy_conv_w[i], hy_conv_b[i], hy_skip[i], dft_c, hrc, hic, hmc, B, Lc)
            zrc = _retention(uc, uc, ret_log_rate[i], B, Lc, Lc, False)
            cf, uc = _projection(cf, c_mod, Lc, out_args=(zac, zyc, zrc, mod, w_out_bf), in_args=nxt)
            xf, u = _projection(xf, x_mod, tm_x, out_args=(za, zy, zr, mod, w_out_bf), in_args=nxt)
            mod = nxt[0]
        else:
            xf = _projection(xf, x_mod, tm_x, out_args=(za, zy, zr, mod, w_out_bf))

    return xf.reshape(B, L, D)
```

```python
import functools
import math

import numpy as np
import jax
import jax.numpy as jnp
from jax import lax
from jax.experimental import pallas as pl
from jax.experimental.pallas import tpu as pltpu

F32 = jnp.float32
BF16 = jnp.bfloat16

D_MODEL = 1024
DEPTH = 2
GRID_W = 64
NA_HEADS = 8
NA_HEAD_DIM = 64
NA_W = NA_HEADS * NA_HEAD_DIM
NA_KH = 8
NA_KW = 16
HY_W = 256
HY_BANDS = 8
HY_EMB = 1 + 2 * HY_BANDS
HY_EMB_PAD = 32
HY_FFN = 64
HY_FAST_DECAY = 0.3
HY_SLOW_DECAY = 1.5
HY_TARGET = 1e-2
RET_HEADS = 4
RET_HEAD_DIM = 64
RET_W = RET_HEADS * RET_HEAD_DIM
ROPE_BASE = 10000.0
EPS = 1e-6
NEG_INF = -1e30
LOG2E = 1.4426950408889634
IN_W = 4 * NA_W + 4 * HY_W + 4 * RET_W
MIX_W = NA_W + HY_W + RET_W

CB = 256
CB_HY_V, CB_HY_X1, CB_HY_X2, CB_HY_G = 8, 9, 10, 11
CB_RE_Q, CB_RE_K, CB_RE_V, CB_RE_G = 12, 13, 14, 15

NA_G = 4
NA_WIN = NA_G + NA_KH
NA_GPS = 4
NA_KT = 256
NA_DR_MIN = -4
NA_DR_NUM = 23
RET_CHUNK = 256
HY_ROWS = 512
HY_BASE = 512
VMEM_LIMIT = 56 * 1024 * 1024


def _cparams(*sem):
    return pltpu.CompilerParams(dimension_semantics=sem, vmem_limit_bytes=VMEM_LIMIT)


def _resident(shape, index_map):
    return pl.BlockSpec(shape, index_map, pipeline_mode=pl.Buffered(1))


def _silu(x):
    return x * (1.0 / (1.0 + jnp.exp(-x)))


def _dot(a, b):
    return jnp.dot(a, b, preferred_element_type=F32)


def _dot_nt(a, b):
    return lax.dot_general(a, b, (((1,), (1,)), ((), ())), preferred_element_type=F32)


def _dot_tn(a, b):
    return lax.dot_general(a, b, (((0,), (0,)), ((), ())), preferred_element_type=F32)


@functools.lru_cache(maxsize=None)
def _bin_map(n, depth):
    if depth == 0:
        return [np.arange(n)], [n]
    hb, he = _bin_map(n // 2, depth - 1)
    blocks = [b for b in hb] + [n - b for b in hb]
    extras = list(he) + [n - e for e in he if e < n // 2]
    return blocks, extras


@functools.lru_cache(maxsize=None)
def _hy_plan(L):
    depth = 0
    while L >> depth > HY_BASE:
        depth += 1
    m = L >> depth
    k = np.arange(m, dtype=np.int64)[:, None] * np.arange(m, dtype=np.int64)[None, :]
    ang = (k % (2 * m)).astype(np.float64) * (math.pi / m)
    base = (np.cos(ang).astype(np.float32), np.sin(ang).astype(np.float32))
    tw_blocks, tw_extras = [], []
    for level in range(1, depth + 1):
        n = m << level
        hb, he = _bin_map(n // 2, level - 1)
        tw_blocks.append([(np.cos(math.pi * b / n), np.sin(math.pi * b / n)) for b in hb])
        tw_extras.append([(math.cos(math.pi * e / n), math.sin(math.pi * e / n), e == n // 2) for e in he])
    blocks, extras = _bin_map(L, depth)
    inv_n = 1.0 / (2 * L)
    seen = set()

    def weight(b):
        w = 0.0 if b in seen else (inv_n if b in (0, L) else 2.0 * inv_n)
        seen.add(b)
        return w

    w_blocks = [np.array([weight(int(b)) for b in blk]) for blk in blocks]
    w_extras = [weight(int(e)) for e in extras]
    tables = [np.broadcast_to(t[:, None], (m, HY_W)) for lvl in tw_blocks for pair in lvl for t in pair]
    tw_table = np.stack(tables).astype(np.float32) if tables else np.zeros((1, 8, HY_W), np.float32)
    w_table = np.stack([np.broadcast_to(w[:, None], (m, HY_W)) for w in w_blocks]).astype(np.float32)
    return dict(depth=depth, m=m, base=base, tw_extras=tw_extras, n_blocks=len(blocks), n_extras=len(extras),
                w_extras=w_extras, tw_table=tw_table, w_table=w_table,
                tw_index=[[2 * (sum(len(l) for l in tw_blocks[:lv]) + i) for i in range(len(tw_blocks[lv]))]
                          for lv in range(depth)])


@functools.lru_cache(maxsize=None)
def _filter_features(L):
    t = np.linspace(0.0, 1.0, L)[:, None]
    omega = 2.0 * math.pi * np.arange(L)[:, None] / L
    bands = np.linspace(1e-4, HY_BANDS - 1, HY_BANDS)[None, :]
    z = np.concatenate([t, np.cos(bands * omega), -np.sin(bands * omega)], axis=-1)
    z = np.pad(z, ((0, 0), (0, HY_EMB_PAD - HY_EMB)))
    deltas = np.abs(np.linspace(math.log(HY_TARGET) / HY_SLOW_DECAY, math.log(HY_TARGET) / HY_FAST_DECAY, HY_W))
    window = np.exp(-t * deltas[None, :])
    return z.astype(np.float32), window.astype(np.float32)


@functools.lru_cache(maxsize=None)
def _rope_tables(L):
    half = RET_HEAD_DIM // 2
    quarter = half // 2
    t = np.arange(L)
    pos = np.stack([t // GRID_W, t % GRID_W], axis=0).astype(np.float64)
    freqs = ROPE_BASE ** (-np.arange(quarter, dtype=np.float64) / quarter)
    cos = np.zeros((L, RET_HEAD_DIM))
    sin = np.zeros((L, RET_HEAD_DIM))
    for a in range(2):
        ang = pos[a][:, None] * freqs[None, :]
        base = a * half
        cos[:, base:base + quarter] = np.cos(ang)
        cos[:, base + quarter:base + half] = np.cos(ang)
        sin[:, base:base + quarter] = -np.sin(ang)
        sin[:, base + quarter:base + half] = np.sin(ang)
    return (np.tile(cos, (1, RET_HEADS)).astype(np.float32), np.tile(sin, (1, RET_HEADS)).astype(np.float32))


@functools.lru_cache(maxsize=None)
def _head_block_ones(width):
    i = np.arange(width) // NA_HEAD_DIM
    return (i[:, None] == i[None, :]).astype(np.float32)


def _mod_kernel(c_ref, w_ref, b_ref, o_ref):
    s = _silu(c_ref[...])
    w = w_ref[...]
    s_hi = s.astype(BF16)
    w_hi = w.astype(BF16)
    s_lo = (s - s_hi.astype(F32)).astype(BF16)
    w_lo = (w - w_hi.astype(F32)).astype(BF16)
    o_ref[...] = _dot(s_hi, w_hi) + (_dot(s_hi, w_lo) + _dot(s_lo, w_hi)) + b_ref[...]


def _modulation(cc, ada_w, ada_b):
    R = cc.shape[0]
    tn = 1024
    return pl.pallas_call(
        _mod_kernel,
        grid=(DEPTH, 3 * D_MODEL // tn),
        in_specs=[
            pl.BlockSpec((R, D_MODEL), lambda i, j: (0, 0)),
            pl.BlockSpec((None, D_MODEL, tn), lambda i, j: (i, 0, j)),
            pl.BlockSpec((None, 1, tn), lambda i, j: (i, 0, j)),
        ],
        out_specs=pl.BlockSpec((None, R, tn), lambda i, j: (i, 0, j)),
        out_shape=jax.ShapeDtypeStruct((DEPTH, R, 3 * D_MODEL), F32),
        compiler_params=_cparams("parallel", "parallel"),
        name="modulation",
    )(cc, ada_w, ada_b.reshape(DEPTH, 1, 3 * D_MODEL))


def _proj_kernel(*refs, with_out, with_in):
    refs = list(refs)
    if with_out:
        za_ref, zy_ref, zr_ref, x_ref, mod_o_ref, w_out_ref = refs[:6]
        del refs[:6]
    else:
        x_ref = refs.pop(0)
    if with_in:
        mod_i_ref, nw_ref, w_in_ref = refs[:3]
        del refs[:3]
    x = x_ref[...]
    if with_out:
        acc = _dot(za_ref[...], w_out_ref[0:NA_W, :])
        acc += _dot(zy_ref[...], w_out_ref[NA_W:NA_W + HY_W, :])
        acc += _dot(zr_ref[...], w_out_ref[NA_W + HY_W:MIX_W, :])
        x = x + mod_o_ref[:, 2 * D_MODEL:3 * D_MODEL] * acc
        refs.pop(0)[...] = x
    if with_in:
        u_ref = refs.pop(0)
        xn = x * lax.rsqrt(jnp.mean(x * x, axis=-1, keepdims=True) + EPS)
        shift = mod_i_ref[:, 0:D_MODEL]
        scale = mod_i_ref[:, D_MODEL:2 * D_MODEL]
        h = (xn * nw_ref[...] * (1.0 + scale) + shift).astype(BF16)
        tn = 1024
        for j in range(IN_W // tn):
            u_ref[:, j * tn:(j + 1) * tn] = _dot(h, w_in_ref[:, j * tn:(j + 1) * tn]).astype(u_ref.dtype)


def _projection(xf, mod_index, tm, out_args=None, in_args=None):
    R = xf.shape[0]
    row = lambda w: pl.BlockSpec((tm, w), lambda i: (i, 0))
    mod_spec = pl.BlockSpec((None, 1, 3 * D_MODEL), lambda i: (mod_index(i), 0, 0))
    args, in_specs, out_specs, out_shape = [], [], [], []
    if out_args is not None:
        za, zy, zr, mod_o, w_out_bf = out_args
        args += [za, zy, zr, xf, mod_o, w_out_bf]
        in_specs += [row(NA_W), row(HY_W), row(RET_W), row(D_MODEL), mod_spec,
                     _resident((MIX_W, D_MODEL), lambda i: (0, 0))]
        out_specs.append(row(D_MODEL))
        out_shape.append(jax.ShapeDtypeStruct((R, D_MODEL), F32))
    else:
        args.append(xf)
        in_specs.append(row(D_MODEL))
    if in_args is not None:
        mod_i, norm_w, w_in_bf = in_args
        args += [mod_i, norm_w, w_in_bf]
        in_specs += [mod_spec, _resident((1, D_MODEL), lambda i: (0, 0)), _resident((D_MODEL, IN_W), lambda i: (0, 0))]
        out_specs.append(row(IN_W))
        out_shape.append(jax.ShapeDtypeStruct((R, IN_W), BF16))
    outs = pl.pallas_call(
        functools.partial(_proj_kernel, with_out=out_args is not None, with_in=in_args is not None),
        grid=(R // tm,),
        in_specs=in_specs,
        out_specs=out_specs,
        out_shape=out_shape,
        compiler_params=_cparams("parallel"),
        name="projection",
    )(*args)
    return outs if len(outs) > 1 else outs[0]


def _head_rms(x, ones_bf, gain):
    ss = _dot((x * x).astype(BF16), ones_bf)
    return x * lax.rsqrt(ss * (1.0 / NA_HEAD_DIM) + EPS) * gain


def _pair_masks():
    lane = lax.broadcasted_iota(jnp.int32, (1, 2 * NA_HEAD_DIM), 1)
    return lane < NA_HEAD_DIM


def _attend_heads(q, key_parts, val_parts, bias_fn):
    first = _pair_masks()
    outs = []
    for pair in range(NA_HEADS // 2):
        lo = pair * 2 * NA_HEAD_DIM
        hi = lo + 2 * NA_HEAD_DIM
        qp = q[:, lo:hi]
        o_pair = None
        for sub in range(2):
            h = 2 * pair + sub
            sel = first if sub == 0 else jnp.logical_not(first)
            qm = jnp.where(sel, qp, 0.0).astype(BF16)
            scores = []
            for part, kp in enumerate(key_parts):
                s = _dot_nt(qm, kp[:, lo:hi])
                b = bias_fn(h, part)
                if b is not None:
                    s = s + b
                scores.append(s)
            m = scores[0].max(axis=-1, keepdims=True)
            for s in scores[1:]:
                m = jnp.maximum(m, s.max(axis=-1, keepdims=True))
            denom = None
            acc = None
            for s, vp in zip(scores, val_parts):
                p = jnp.exp(s - m)
                ps = p.sum(axis=-1, keepdims=True)
                denom = ps if denom is None else denom + ps
                pv = _dot(p.astype(BF16), vp[:, lo:hi])
                acc = pv if acc is None else acc + pv
            o_h = acc * (1.0 / denom)
            o_pair = o_h if o_pair is None else jnp.where(first, o_pair, o_h)
        outs.append(o_pair)
    return jnp.concatenate(outs, axis=-1)


def _na_kernel(q_ref, k_ref, v_ref, g_ref, kc_ref, vc_ref, qg_ref, kg_ref, ones_ref, rpb_ref, o_ref,
               kn_scr, kcn_scr, vat_scr, vbt_scr, vcat_scr, vcbt_scr, bias_scr, mask_scr, s0_scr, s1_scr,
               p0_scr, p1_scr, qm_scr, ot_scr, *, rows):
    batch = pl.program_id(0)
    grp = pl.program_id(1)
    ones_bf = ones_ref[...]
    tq = NA_G * GRID_W
    nwin = NA_WIN * GRID_W
    Lc = kc_ref.shape[0]
    pair_w = 2 * NA_HEAD_DIM
    kt = NA_KT
    first = lax.broadcasted_iota(jnp.int32, (1, pair_w), 1) < NA_HEAD_DIM

    @pl.when((batch == 0) & (grp == 0))
    def _():
        ck = lax.broadcasted_iota(jnp.int32, (GRID_W, pair_w), 0)
        lane = lax.broadcasted_iota(jnp.int32, (GRID_W, pair_w), 1)
        cq = lane % GRID_W
        col_start = jnp.clip(cq - NA_KW // 2, 0, GRID_W - NA_KW)
        col_ok = (ck >= col_start) & (ck < col_start + NA_KW)
        left = lane < GRID_W

        def body(i, carry):
            for h in range(NA_HEADS):
                v0 = jnp.broadcast_to(rpb_ref[i + 1, h:h + 1, :], (GRID_W, pair_w))
                v1 = jnp.broadcast_to(rpb_ref[i, h:h + 1, :], (GRID_W, pair_w))
                t0 = pltpu.roll(v0, pair_w - (NA_KW - 1), 1, stride=1, stride_axis=0)
                t1 = pltpu.roll(v1, GRID_W - (NA_KW - 1), 1, stride=1, stride_axis=0)
                bias_scr[i, h] = jnp.where(col_ok, jnp.where(left, t0, t1), NEG_INF)
            return carry

        lax.fori_loop(0, NA_DR_NUM, body, 0)

    @pl.when(grp == 0)
    def _():
        kgain = kg_ref[...]
        own_a = (lax.broadcasted_iota(jnp.int32, (NA_W, 1), 0) % pair_w) < NA_HEAD_DIM
        for src, dst, vat, vbt, vsrc in ((k_ref, kn_scr, vat_scr, vbt_scr, v_ref),
                                         (kc_ref, kcn_scr, vcat_scr, vcbt_scr, vc_ref)):
            for i in range(src.shape[0] // kt):
                sl = slice(i * kt, (i + 1) * kt)
                dst[sl, :] = _head_rms(src[sl, :].astype(F32), ones_bf, kgain).astype(BF16)
                vt = vsrc[sl, :].astype(F32).T
                vat[i] = jnp.where(own_a, vt, 1.0).astype(BF16)
                vbt[i] = jnp.where(own_a, 1.0, vt).astype(BF16)

    n_kt = (nwin + Lc) // kt
    rows_per_tile = kt // GRID_W
    n_ch = (nwin + Lc) // GRID_W
    n_items = NA_GPS * NA_HEADS

    s_bufs = (s0_scr, s1_scr)
    p_bufs = (p0_scr, p1_scr)
    row_zero = jnp.minimum(grp, 0)

    def rows_of(start, size):
        return pl.ds(pl.multiple_of(row_zero + start, GRID_W), size)

    def head_lanes(h):
        return slice((h // 2) * pair_w, (h // 2 + 1) * pair_w)

    key0, tile0, dr_base = [], [], []
    for gi in range(NA_GPS):
        r0 = (grp * NA_GPS + gi) * NA_G
        ws = jnp.clip(r0 - NA_KH // 2, 0, rows - NA_WIN)
        key0.append(pl.multiple_of(ws * GRID_W, kt))
        tile0.append(ws // rows_per_tile)
        dr_base.append(ws - r0 + (NA_KH - 1) - NA_DR_MIN)
        w_id = lax.broadcasted_iota(jnp.int32, (16, tq), 0)
        g_id = lax.broadcasted_iota(jnp.int32, (16, tq), 1) // GRID_W
        lo = jnp.clip(r0 + g_id - NA_KH // 2, 0, rows - NA_KH) - ws
        mask_scr[gi] = jnp.where((w_id >= lo) & (w_id < lo + NA_KH), 0.0, NEG_INF).astype(F32)
        q = _head_rms(q_ref[gi * tq:(gi + 1) * tq, :].astype(F32), ones_bf,
                      qg_ref[...] * (NA_HEAD_DIM ** -0.5 * LOG2E))
        for h in range(NA_HEADS):
            sel = first if h % 2 == 0 else jnp.logical_not(first)
            qm_scr[gi * NA_HEADS + h, rows_of(0, tq), :] = jnp.where(sel, q[:, head_lanes(h)], 0.0).astype(BF16)

    col_max = {}

    def qk_piece(n, j):
        gi, h = divmod(n, NA_HEADS)
        last = j == n_kt - 1
        keys = kcn_scr[:, head_lanes(h)] if last else kn_scr[pl.ds(key0[gi] + j * kt, kt), head_lanes(h)]
        sv = _dot_nt(keys, qm_scr[n, rows_of(0, tq), :])
        for dw in range(rows_per_tile):
            sw = sv[dw * GRID_W:(dw + 1) * GRID_W, :]
            if not last:
                w = j * rows_per_tile + dw
                bias = jnp.concatenate([bias_scr[dr_base[gi] + (w - g), h] for g in range(0, NA_G, 2)], axis=-1)
                sw = sw + bias + mask_scr[gi, w:w + 1, :]
            s_bufs[n % 2][rows_of(j * kt + dw * GRID_W, GRID_W), :] = sw
            part = sw.reshape(GRID_W // 8, 8, tq).max(axis=0)
            col_max[n] = part if (j == 0 and dw == 0) else jnp.maximum(col_max[n], part)

    def exp_chunk(n, c):
        if c == 0:
            col_max[n] = col_max[n].max(axis=0, keepdims=True)
        rows_c = rows_of(c * GRID_W, GRID_W)
        p_bufs[n % 2][rows_c, :] = jnp.exp2(s_bufs[n % 2][rows_c, :] - col_max[n]).astype(BF16)
        if c == n_ch - 1:
            del col_max[n]

    def pv(n):
        gi, h = divmod(n, NA_HEADS)
        vt, vct = (vat_scr, vcat_scr) if h % 2 == 0 else (vbt_scr, vcbt_scr)
        ot = _dot(vct[0, head_lanes(h), :], p_bufs[n % 2][rows_of(nwin, Lc), :])
        for j in range(n_kt - 1):
            ot += _dot(vt[tile0[gi] + j, head_lanes(h), :], p_bufs[n % 2][rows_of(j * kt, kt), :])
        own, den = (ot[0:NA_HEAD_DIM], ot[NA_HEAD_DIM:]) if h % 2 == 0 else (ot[NA_HEAD_DIM:], ot[0:NA_HEAD_DIM])
        ot_scr[rows_of(n * NA_HEAD_DIM, NA_HEAD_DIM), :] = own * (1.0 / den)
        if h % 2 == 1:
            o_pair = ot_scr[rows_of((n - 1) * NA_HEAD_DIM, pair_w), :].T
            q_rows = slice(gi * tq, (gi + 1) * tq)
            gate = _silu(g_ref[q_rows, head_lanes(h)].astype(F32))
            o_ref[q_rows, head_lanes(h)] = (o_pair * gate).astype(o_ref.dtype)

    for t in range(n_items + 2):
        for i in range(n_ch):
            if 0 <= t - 1 < n_items:
                exp_chunk(t - 1, i)
            if i % 4 == 0 and t < n_items and i // 4 < n_kt:
                qk_piece(t, i // 4)
            if i == 2 and 0 <= t - 2:
                pv(t - 2)


def _neighborhood_attention(u, uc, q_gain, k_gain, rpb, B, L, Lc):
    rows = L // GRID_W
    ngrp = rows // (NA_G * NA_GPS)
    tq = NA_G * GRID_W
    tstep = tq * NA_GPS
    ones = jnp.asarray(_head_block_ones(NA_W)).astype(BF16)
    qg = jnp.tile(q_gain, NA_HEADS)[None, :]
    kg = jnp.tile(k_gain, NA_HEADS)[None, :]
    n_dr = 2 * NA_KH - 1
    rpb_t = jnp.transpose(rpb.astype(F32)[:, :, ::-1] * LOG2E, (1, 0, 2))
    lo_pad = 1 - NA_DR_MIN
    rpb_t = jnp.pad(rpb_t, ((lo_pad, NA_DR_NUM + 1 - lo_pad - n_dr), (0, 0), (0, 2 * GRID_W - (2 * NA_KW - 1))))
    return pl.pallas_call(
        functools.partial(_na_kernel, rows=rows),
        grid=(B, ngrp),
        in_specs=[
            pl.BlockSpec((tstep, NA_W), lambda b, g: (b * ngrp + g, 0)),
            pl.BlockSpec((L, NA_W), lambda b, g: (b, 1)),
            pl.BlockSpec((L, NA_W), lambda b, g: (b, 2)),
            pl.BlockSpec((tstep, NA_W), lambda b, g: (b * ngrp + g, 3)),
            pl.BlockSpec((Lc, NA_W), lambda b, g: (b, 1)),
            pl.BlockSpec((Lc, NA_W), lambda b, g: (b, 2)),
            _resident((1, NA_W), lambda b, g: (0, 0)),
            _resident((1, NA_W), lambda b, g: (0, 0)),
            _resident((NA_W, NA_W), lambda b, g: (0, 0)),
            _resident((NA_DR_NUM + 1, NA_HEADS, 2 * GRID_W), lambda b, g: (0, 0, 0)),
        ],
        out_specs=pl.BlockSpec((tstep, NA_W), lambda b, g: (b * ngrp + g, 0)),
        out_shape=jax.ShapeDtypeStruct((B * L, NA_W), BF16),
        scratch_shapes=[
            pltpu.VMEM((L, NA_W), BF16), pltpu.VMEM((Lc, NA_W), BF16),
            pltpu.VMEM((L // NA_KT, NA_W, NA_KT), BF16), pltpu.VMEM((L // NA_KT, NA_W, NA_KT), BF16),
            pltpu.VMEM((Lc // NA_KT, NA_W, NA_KT), BF16), pltpu.VMEM((Lc // NA_KT, NA_W, NA_KT), BF16),
            pltpu.VMEM((NA_DR_NUM, NA_HEADS, GRID_W, 2 * GRID_W), F32),
            pltpu.VMEM((NA_GPS, 16, tq), F32),
            pltpu.VMEM((NA_WIN * GRID_W + Lc, tq), F32), pltpu.VMEM((NA_WIN * GRID_W + Lc, tq), F32),
            pltpu.VMEM((NA_WIN * GRID_W + Lc, tq), BF16), pltpu.VMEM((NA_WIN * GRID_W + Lc, tq), BF16),
            pltpu.VMEM((NA_GPS * NA_HEADS, tq, 2 * NA_HEAD_DIM), BF16), pltpu.VMEM((NA_GPS * NA_W, tq), F32),
        ],
        compiler_params=_cparams("arbitrary", "arbitrary"),
        name="neighborhood_attention",
    )(u, u, u, u, uc, uc, qg, kg, ones, rpb_t)


def _ctx_attn_kernel(q_ref, k_ref, v_ref, g_ref, qg_ref, kg_ref, ones_ref, o_ref):
    ones_bf = ones_ref[...]
    q = _head_rms(q_ref[...].astype(F32), ones_bf, qg_ref[...] * (NA_HEAD_DIM ** -0.5))
    k = _head_rms(k_ref[...].astype(F32), ones_bf, kg_ref[...]).astype(BF16)
    o = _attend_heads(q, [k], [v_ref[...]], lambda h, part: None)
    o_ref[...] = (o * _silu(g_ref[...].astype(F32))).astype(o_ref.dtype)


def _context_attention(uc, q_gain, k_gain, B, Lc):
    ones = jnp.asarray(_head_block_ones(NA_W)).astype(BF16)
    qg = jnp.tile(q_gain, NA_HEADS)[None, :]
    kg = jnp.tile(k_gain, NA_HEADS)[None, :]
    return pl.pallas_call(
        _ctx_attn_kernel,
        grid=(B,),
        in_specs=[
            pl.BlockSpec((Lc, NA_W), lambda b: (b, 0)),
            pl.BlockSpec((Lc, NA_W), lambda b: (b, 1)),
            pl.BlockSpec((Lc, NA_W), lambda b: (b, 2)),
            pl.BlockSpec((Lc, NA_W), lambda b: (b, 3)),
            _resident((1, NA_W), lambda b: (0, 0)),
            _resident((1, NA_W), lambda b: (0, 0)),
            _resident((NA_W, NA_W), lambda b: (0, 0)),
        ],
        out_specs=pl.BlockSpec((Lc, NA_W), lambda b: (b, 0)),
        out_shape=jax.ShapeDtypeStruct((B * Lc, NA_W), BF16),
        compiler_params=_cparams("parallel"),
        name="context_attention",
    )(uc, uc, uc, uc, qg, kg, ones)


def _alt_sign(shape):
    row = lax.broadcasted_iota(jnp.int32, shape, 0)
    return (1 - 2 * (row & 1)).astype(F32)


def _lane_halves(x):
    return [x[:, h * 128:(h + 1) * 128] for h in range(HY_W // 128)]


class _Spec:
    def __init__(self, c, s):
        self.c, self.s = list(c), list(s)


def _combine(P, R, tw):
    lo_c, lo_s, mi_c, mi_s = [], [], [], []
    for (pc, ps, rc, rs, (ct, st, single)) in zip(P.c, P.s, R.c, R.s, tw):
        if single:
            lo_c.append(pc)
            lo_s.append(rc)
            continue
        qc = ct * rc - st * rs
        qs = st * rc + ct * rs
        lo_c.append(pc + qc)
        lo_s.append(ps + qs)
        mi_c.append(pc - qc)
        mi_s.append(qs - ps)
    return _Spec(lo_c + mi_c, lo_s + mi_s)


def _combine_t(Z, tw):
    n_lo = len(tw)
    lo_c, lo_s, mi_c, mi_s = Z.c[:n_lo], Z.s[:n_lo], Z.c[n_lo:], Z.s[n_lo:]
    pc, ps, rc, rs = [], [], [], []
    mi = 0
    for i, (ct, st, single) in enumerate(tw):
        if single:
            pc.append(lo_c[i])
            ps.append(jnp.zeros_like(lo_c[i]))
            rc.append(lo_s[i])
            rs.append(jnp.zeros_like(lo_c[i]))
            continue
        g_qc = lo_c[i] - mi_c[mi]
        g_qs = lo_s[i] + mi_s[mi]
        pc.append(lo_c[i] + mi_c[mi])
        ps.append(lo_s[i] - mi_s[mi])
        rc.append(ct * g_qc + st * g_qs)
        rs.append(ct * g_qs - st * g_qc)
        mi += 1
    return _Spec(pc, ps), _Spec(rc, rs)


def _forward(level, leaves, tws):
    if level == 0:
        return leaves[0]
    half = len(leaves) // 2
    return _combine(_forward(level - 1, leaves[:half], tws), _forward(level - 1, leaves[half:], tws), tws[level - 1])


def _backward(level, Z, tws):
    if level == 0:
        return [Z]
    P, R = _combine_t(Z, tws[level - 1])
    return _backward(level - 1, P, tws) + _backward(level - 1, R, tws)


def _leaf_samples(depth, off=0, stride=1):
    if depth == 0:
        return [(off, stride)]
    return _leaf_samples(depth - 1, off, 2 * stride) + _leaf_samples(depth - 1, off + stride, 2 * stride)


def _block_twiddles(plan, tw_ref, k0, k1):
    return [[(tw_ref[i, k0:k1, :], tw_ref[i + 1, k0:k1, :], False) for i in plan["tw_index"][lv]]
            for lv in range(plan["depth"])]


def _spectrum_of(plan, src_scr, xbf_scr, c_ref, s_ref, tw_ref, consume):
    depth, m = plan["depth"], plan["m"]
    cm = min(HY_ROWS // 2, m)
    sign = _alt_sign((cm, HY_W))
    tops = []
    for i, (off, stride) in enumerate(_leaf_samples(depth)):
        top = jnp.zeros((1, HY_W), F32)
        for j0 in range(0, m, cm):
            x = jnp.concatenate([src_scr[h, pl.ds(off + stride * j0, cm, stride=stride), :]
                                 for h in range(HY_W // 128)], axis=1)
            xbf_scr[i, j0:j0 + cm, :] = x.astype(BF16)
            top += jnp.sum(x * sign, axis=0, keepdims=True)
        tops.append(top)
    for k0 in range(0, m, cm):
        k1 = k0 + cm
        base = [_Spec([_dot(c_ref[k0:k1, :], xbf_scr[i])], [_dot(s_ref[k0:k1, :], xbf_scr[i])])
                for i in range(len(tops))]
        consume(k0, k1, _forward(depth, base, _block_twiddles(plan, tw_ref, k0, k1)))
    zero = jnp.zeros((1, HY_W), F32)
    return _forward(depth, [_Spec([t], [zero]) for t in tops], plan["tw_extras"])


def _sequence_of(plan, gc_scr, gs_scr, tops, c_ref, s_ref, dst_scr):
    depth, m = plan["depth"], plan["m"]
    cm = min(HY_ROWS // 2, m)
    sign = _alt_sign((cm, HY_W))
    for i, (off, stride) in enumerate(_leaf_samples(depth)):
        for j0 in range(0, m, cm):
            y = _dot(c_ref[j0:j0 + cm, :], gc_scr[i]) + _dot(s_ref[j0:j0 + cm, :], gs_scr[i]) + sign * tops[i]
            for h in range(HY_W // 128):
                dst_scr[h, pl.ds(off + stride * j0, cm, stride=stride), :] = y[:, h * 128:(h + 1) * 128]


def _filter_kernel(z_ref, win_ref, w1_ref, b1_ref, w2_ref, b2_ref, w3_ref, sf_ref, c_ref, s_ref, tw_ref, wt_ref,
                   hr_ref, hi_ref, hx_ref, h_scr, a_scr, d_scr, xbf_scr, *, L):
    plan = _hy_plan(L)
    hp = lax.Precision.HIGHEST
    dot = lambda a, b: jnp.dot(a, b, preferred_element_type=F32, precision=hp)

    @pl.when(pl.program_id(0) == 0)
    def _():
        h = jnp.sin(sf_ref[0:1, :] * (dot(z_ref[...], w1_ref[...]) + b1_ref[...]))
        h_scr[...] = jnp.sin(sf_ref[1:2, :] * (dot(h, w2_ref[...]) + b2_ref[...]))

    taps = dot(h_scr[...], w3_ref[...])
    win = win_ref[...]
    row = lax.broadcasted_iota(jnp.int32, (L, HY_W), 0)
    hf = jnp.concatenate([taps[:, 0:HY_W], taps[:, 2 * HY_W:3 * HY_W]], axis=0) * win
    hb = jnp.concatenate([taps[:, HY_W:2 * HY_W], taps[:, 3 * HY_W:4 * HY_W]], axis=0) * win
    hb = jnp.where(row == 0, 0.0, hb)
    norm = jnp.sum(jnp.abs(hf), axis=0, keepdims=True) + jnp.sum(jnp.abs(hb), axis=0, keepdims=True)
    inv = 1.0 / norm
    for h, (a_half, d_half) in enumerate(zip(_lane_halves((hf + hb) * inv), _lane_halves((hf - hb) * inv))):
        a_scr[h] = a_half
        d_scr[h] = d_half

    def store_real(k0, k1, spec):
        for b, blk in enumerate(spec.c):
            hr_ref[b, k0:k1, :] = blk * wt_ref[b, k0:k1, :]

    def store_imag(k0, k1, spec):
        for b, blk in enumerate(spec.s):
            hi_ref[b, k0:k1, :] = -blk * wt_ref[b, k0:k1, :]

    extra_a = _spectrum_of(plan, a_scr, xbf_scr, c_ref, s_ref, tw_ref, store_real)
    extra_d = _spectrum_of(plan, d_scr, xbf_scr, c_ref, s_ref, tw_ref, store_imag)
    for e, w in enumerate(plan["w_extras"]):
        hx_ref[0, e] = extra_a.c[e] * w
        hx_ref[1, e] = -extra_d.s[e] * w


def _block_diag2(w):
    z = jnp.zeros_like(w)
    return jnp.concatenate([jnp.concatenate([w, z], axis=1), jnp.concatenate([z, w], axis=1)], axis=0)


def _hyena_consts(L):
    plan = _hy_plan(L)
    c, s = plan["base"]
    return (jnp.asarray(c).astype(BF16), jnp.asarray(s).astype(BF16), jnp.asarray(plan["tw_table"]),
            jnp.asarray(plan["w_table"]))


def _hyena_filter(L, w1, b1, w2, b2, w3, sin_freq, consts):
    plan = _hy_plan(L)
    m, nb, ne = plan["m"], plan["n_blocks"], plan["n_extras"]
    c_bf, s_bf, tw, wt = consts
    z, window = _filter_features(L)
    z2 = np.concatenate([z[:L // 2], z[L // 2:]], axis=1)
    w1d = _block_diag2(jnp.pad(w1, ((0, HY_EMB_PAD - HY_EMB), (0, 0))))
    w2d = _block_diag2(w2)
    w3d = jnp.stack([_block_diag2(w3[:, o * 2 * HY_W:(o + 1) * 2 * HY_W]) for o in range(2)])
    pair = lambda v: jnp.tile(v, (1, 2))
    full = lambda *shape: _resident(shape, lambda o: (0,) * len(shape))
    spectrum = pl.BlockSpec((None, nb, m, HY_W), lambda o: (o, 0, 0, 0))
    return pl.pallas_call(
        functools.partial(_filter_kernel, L=L),
        grid=(2,),
        in_specs=[
            full(L // 2, 2 * HY_EMB_PAD), full(L, HY_W), full(2 * HY_EMB_PAD, 2 * HY_FFN), full(1, 2 * HY_FFN),
            full(2 * HY_FFN, 2 * HY_FFN), full(1, 2 * HY_FFN),
            pl.BlockSpec((None, 2 * HY_FFN, 4 * HY_W), lambda o: (o, 0, 0)),
            full(2, 2 * HY_FFN), full(m, m), full(m, m), full(*tw.shape), full(*wt.shape),
        ],
        out_specs=[spectrum, spectrum, pl.BlockSpec((None, 2, ne, 1, HY_W), lambda o: (o, 0, 0, 0, 0))],
        out_shape=[jax.ShapeDtypeStruct((2, nb, m, HY_W), F32), jax.ShapeDtypeStruct((2, nb, m, HY_W), F32),
                   jax.ShapeDtypeStruct((2, 2, ne, 1, HY_W), F32)],
        scratch_shapes=[pltpu.VMEM((L // 2, 2 * HY_FFN), F32), pltpu.VMEM((HY_W // 128, L, 128), F32),
                        pltpu.VMEM((HY_W // 128, L, 128), F32), pltpu.VMEM((nb, m, HY_W), BF16)],
        compiler_params=_cparams("arbitrary"),
        name="hyena_filter",
    )(jnp.asarray(z2), jnp.asarray(window), w1d, pair(b1[None, :]), w2d, pair(b2[None, :]), w3d, pair(sin_freq),
      c_bf, s_bf, tw, wt)


def _hyena_kernel(v_ref, x1_ref, x2_ref, g_ref, cw_ref, cb_ref, skip_ref, c_ref, s_ref, tw_ref,
                  hr_ref, hi_ref, hx_ref, o_ref, a_scr, x_scr, y_scr, xbf_scr, gc_scr, gs_scr, *, L):
    plan = _hy_plan(L)
    depth = plan["depth"]
    ck = min(HY_ROWS, L)
    halo = 16
    chunks = [(t0, t0 + ck) for t0 in range(0, L, ck)]

    def split_store(scr, t0, t1, val):
        for h, part in enumerate(_lane_halves(val)):
            scr[h, t0:t1, :] = part

    def joined(scr, t0, t1):
        return jnp.concatenate([scr[h, t0:t1, :] for h in range(HY_W // 128)], axis=1)

    def short_conv(ref, j, t0, t1):
        lo, hi = max(t0 - halo, 0), min(t1 + halo, L)
        u = ref[lo:hi, :].astype(F32)
        row = lo + lax.broadcasted_iota(jnp.int32, (hi - lo, HY_W), 0)
        prev = jnp.where(row == 0, 0.0, pltpu.roll(u, 1, 0))
        nxt = jnp.where(row == L - 1, 0.0, pltpu.roll(u, hi - lo - 1, 0))
        w = cw_ref[:, j * HY_W:(j + 1) * HY_W]
        z = prev * w[0:1, :] + u * w[1:2, :] + nxt * w[2:3, :] + cb_ref[:, j * HY_W:(j + 1) * HY_W]
        return z[t0 - lo:t1 - lo, :]

    def long_conv(order):
        def product(spec, hr, hi):
            return _Spec([c * r + s * i for c, s, r, i in zip(spec.c, spec.s, hr, hi)],
                         [s * r - c * i for c, s, r, i in zip(spec.c, spec.s, hr, hi)])

        def consume(k0, k1, spec):
            nb = len(spec.c)
            z = product(spec, [hr_ref[order, b, k0:k1, :] for b in range(nb)],
                        [hi_ref[order, b, k0:k1, :] for b in range(nb)])
            for i, leaf in enumerate(_backward(depth, z, _block_twiddles(plan, tw_ref, k0, k1))):
                gc_scr[i, k0:k1, :] = leaf.c[0].astype(BF16)
                gs_scr[i, k0:k1, :] = leaf.s[0].astype(BF16)

        extras = _spectrum_of(plan, a_scr, xbf_scr, c_ref, s_ref, tw_ref, consume)
        ne = len(extras.c)
        zx = product(extras, [hx_ref[order, 0, e] for e in range(ne)], [hx_ref[order, 1, e] for e in range(ne)])
        tops = [leaf.c[0] for leaf in _backward(depth, zx, plan["tw_extras"])]
        _sequence_of(plan, gc_scr, gs_scr, tops, c_ref, s_ref, y_scr)

    for t0, t1 in chunks:
        split_store(a_scr, t0, t1, short_conv(v_ref, 0, t0, t1))
        x_scr[t0:t1, :] = short_conv(x1_ref, 1, t0, t1)
    long_conv(0)
    for t0, t1 in chunks:
        conv = joined(y_scr, t0, t1) + joined(a_scr, t0, t1) * skip_ref[0:1, :]
        split_store(a_scr, t0, t1, x_scr[t0:t1, :] * conv)
    long_conv(1)
    for t0, t1 in chunks:
        conv = joined(y_scr, t0, t1) + joined(a_scr, t0, t1) * skip_ref[1:2, :]
        y = short_conv(x2_ref, 2, t0, t1) * conv
        o_ref[t0:t1, :] = (y * _silu(g_ref[t0:t1, :].astype(F32))).astype(o_ref.dtype)


def _hyena(u, conv_w, conv_b, skip, consts, hr, hi, hx, B, L):
    plan = _hy_plan(L)
    m, nb, ne = plan["m"], plan["n_blocks"], plan["n_extras"]
    c_bf, s_bf, tw, _ = consts
    col = lambda j: pl.BlockSpec((L, CB), lambda b: (b, j))
    const = lambda *shape: _resident(shape, lambda b: (0,) * len(shape))
    halves = HY_W // 128
    return pl.pallas_call(
        functools.partial(_hyena_kernel, L=L),
        grid=(B,),
        in_specs=[
            col(CB_HY_V), col(CB_HY_X1), col(CB_HY_X2), col(CB_HY_G),
            const(3, 3 * HY_W), const(1, 3 * HY_W), const(2, HY_W),
            const(m, m), const(m, m), const(*tw.shape),
            const(2, nb, m, HY_W), const(2, nb, m, HY_W), const(2, 2, ne, 1, HY_W),
        ],
        out_specs=pl.BlockSpec((L, HY_W), lambda b: (b, 0)),
        out_shape=jax.ShapeDtypeStruct((B * L, HY_W), BF16),
        scratch_shapes=[
            pltpu.VMEM((halves, L, 128), F32), pltpu.VMEM((L, HY_W), F32), pltpu.VMEM((halves, L, 128), F32),
            pltpu.VMEM((nb, m, HY_W), BF16), pltpu.VMEM((nb, m, HY_W), BF16), pltpu.VMEM((nb, m, HY_W), BF16),
        ],
        compiler_params=_cparams("parallel"),
        name="hyena",
    )(u, u, u, u, conv_w, conv_b[None, :], skip, c_bf, s_bf, tw, hr, hi, hx)


def _ret_kernel(*refs, L, Lc, has_init):
    if has_init:
        (q_ref, k_ref, v_ref, g_ref, kc_ref, vc_ref, cos_ref, sin_ref, rate_ref, rrow_ref, ones_ref,
         o_ref, q_scr, k_scr, sf_scr, sb_scr, r_scr) = refs
    else:
        (q_ref, k_ref, v_ref, g_ref, cos_ref, sin_ref, rate_ref, rrow_ref, ones_ref,
         o_ref, q_scr, k_scr, sf_scr, sb_scr, r_scr) = refs
    C = min(RET_CHUNK, L)
    nch = L // C
    W = RET_W
    quarter = RET_HEAD_DIM // 4
    block = ones_ref[...]
    lg = -jnp.exp(rate_ref[...])
    lg_f, lg_b = lg[0:1, :], lg[1:2, :]

    lane = lax.broadcasted_iota(jnp.int32, (L, W), 1)
    first_quarter = (lane % (2 * quarter)) < quarter

    def rope(a):
        swapped = jnp.where(first_quarter, pltpu.roll(a, W - quarter, 1), pltpu.roll(a, quarter, 1))
        return a * cos_ref[...] + swapped * sin_ref[...]

    if has_init:
        q_scr[...] = rope(q_ref[...].astype(F32)).astype(BF16)
        k_scr[...] = rope(k_ref[...].astype(F32) * (RET_HEAD_DIM ** -0.5)).astype(BF16)
    else:
        q_scr[...] = q_ref[...]
        k_scr[...] = (k_ref[...].astype(F32) * (RET_HEAD_DIM ** -0.5)).astype(BF16)

    def decays(n_rows):
        pos = lax.broadcasted_iota(jnp.int32, (n_rows, W), 0).astype(F32)
        return (jnp.exp(lg_f * (n_rows - 1.0 - pos)),
                jnp.exp(lg_b * pos))

    def chunk_states(k_bf, v_bf, zf, zb):
        kf = (k_bf.astype(F32) * zf).astype(BF16)
        kb = (k_bf.astype(F32) * zb).astype(BF16)
        return _dot_tn(kf, v_bf) * block, _dot_tn(kb, v_bf) * block

    zeta_f, zeta_b = decays(C)
    if has_init:
        zc_f, zc_b = decays(Lc)
        kc = (kc_ref[...].astype(F32) * (RET_HEAD_DIM ** -0.5)).astype(BF16)
        s0_f, s0_b = chunk_states(kc, vc_ref[...], zc_f, zc_b)
    else:
        s0_f = jnp.zeros((W, W), F32)
        s0_b = jnp.zeros((W, W), F32)

    for n in range(nch):
        kv_f, kv_b = chunk_states(k_scr[n * C:(n + 1) * C, :], v_ref[n * C:(n + 1) * C, :], zeta_f, zeta_b)
        sf_scr[n] = kv_f
        sb_scr[n] = kv_b
    dec_f = jnp.exp(lg_f * float(C))
    dec_b = jnp.exp(lg_b * float(C))
    state = s0_f
    for n in range(nch):
        kv = sf_scr[n]
        sf_scr[n] = state
        state = dec_f * state + kv
    state = s0_b
    for n in range(nch - 1, -1, -1):
        kv = sb_scr[n]
        sb_scr[n] = state
        state = dec_b * state + kv

    posc = lax.broadcasted_iota(jnp.int32, (C, W), 0).astype(F32)
    xi_f = jnp.exp(lg_f * (posc + 1.0))
    xi_b = jnp.exp(lg_b * (float(C) - posc))
    diff = (lax.broadcasted_iota(jnp.int32, (C, C), 0) - lax.broadcasted_iota(jnp.int32, (C, C), 1)).astype(F32)
    lane_c = lax.broadcasted_iota(jnp.int32, (1, W), 1)
    dmask = []
    for h in range(RET_HEADS):
        rf = -jnp.exp(rrow_ref[h:h + 1, 0:C])
        rb = -jnp.exp(rrow_ref[RET_HEADS + h:RET_HEADS + h + 1, 0:C])
        dmask.append(jnp.where(diff >= 0, jnp.exp(rf * jnp.maximum(diff, 0.0)), 0.0)
                     + jnp.where(diff <= 0, jnp.exp(rb * jnp.maximum(-diff, 0.0)), 0.0))

    for n in range(nch):
        qn = q_scr[n * C:(n + 1) * C, :]
        kn = k_scr[n * C:(n + 1) * C, :]
        vn = v_ref[n * C:(n + 1) * C, :]
        qf = qn.astype(F32)
        lhs = [(qf * xi_f).astype(BF16), (qf * xi_b).astype(BF16)]
        rhs = [sf_scr[n].astype(BF16), sb_scr[n].astype(BF16)]
        zero = jnp.zeros_like(qn)
        for h in range(RET_HEADS):
            sel = (lane_c // RET_HEAD_DIM) == h
            s = _dot_nt(jnp.where(sel, qn, zero), kn) * dmask[h]
            lhs.append(s.astype(BF16))
            rhs.append(jnp.where(sel, vn, zero))
        r_scr[n * C:(n + 1) * C, :] = _dot(jnp.concatenate(lhs, axis=1), jnp.concatenate(rhs, axis=0))

    r = r_scr[...]
    ss = _dot((r * r).astype(BF16), block.astype(BF16))
    rn = r * lax.rsqrt(ss * (1.0 / RET_HEAD_DIM) + EPS)
    o_ref[...] = (rn * _silu(g_ref[...].astype(F32))).astype(o_ref.dtype)


def _retention(u, uc, ret_log_rate, B, L, Lc, has_init):
    C = min(RET_CHUNK, L)
    nch = L // C
    col = lambda j: pl.BlockSpec((L, CB), lambda b: (b, j))
    ccol = lambda j: pl.BlockSpec((Lc, CB), lambda b: (b, j))
    cos, sin = _rope_tables(L)
    rate_lane = jnp.repeat(ret_log_rate, RET_HEAD_DIM, axis=1)
    rate_row = jnp.broadcast_to(ret_log_rate.reshape(2 * RET_HEADS, 1), (2 * RET_HEADS, RET_W))
    ones = jnp.asarray(_head_block_ones(RET_W))
    in_specs = [col(CB_RE_Q), col(CB_RE_K), col(CB_RE_V), col(CB_RE_G)]
    args = [u, u, u, u]
    if has_init:
        in_specs += [ccol(CB_RE_K), ccol(CB_RE_V)]
        args += [uc, uc]
    in_specs += [
        _resident((L, RET_W), lambda b: (0, 0)),
        _resident((L, RET_W), lambda b: (0, 0)),
        _resident((2, RET_W), lambda b: (0, 0)),
        _resident((2 * RET_HEADS, RET_W), lambda b: (0, 0)),
        _resident((RET_W, RET_W), lambda b: (0, 0)),
    ]
    args += [jnp.asarray(cos), jnp.asarray(sin), rate_lane, rate_row, ones]
    return pl.pallas_call(
        functools.partial(_ret_kernel, L=L, Lc=Lc, has_init=has_init),
        grid=(B,),
        in_specs=in_specs,
        out_specs=pl.BlockSpec((L, RET_W), lambda b: (b, 0)),
        out_shape=jax.ShapeDtypeStruct((B * L, RET_W), BF16),
        scratch_shapes=[
            pltpu.VMEM((L, RET_W), BF16), pltpu.VMEM((L, RET_W), BF16),
            pltpu.VMEM((nch, RET_W, RET_W), F32), pltpu.VMEM((nch, RET_W, RET_W), F32),
            pltpu.VMEM((L, RET_W), F32),
        ],
        compiler_params=_cparams("parallel"),
        name="retention",
    )(*args)


def kernel(x, c, ctx, c_ctx, norm_w, ada_w, ada_b, w_in, w_out, na_q_gain, na_k_gain, na_rpb, hy_conv_w, hy_conv_b,
           hy_w1, hy_b1, hy_w2, hy_b2, hy_w3, hy_sin_freq, hy_skip, ret_log_rate):
    B, L, D = x.shape
    Lc = ctx.shape[1]
    assert D == D_MODEL and L % (GRID_W * NA_G * NA_GPS) == 0 and L // GRID_W >= NA_WIN
    assert NA_G == NA_KT // GRID_W and (L // GRID_W - NA_WIN) % NA_G == 0 and Lc == NA_KT

    n_cond = 16
    cc = jnp.concatenate([c, c_ctx[None, :], jnp.zeros((n_cond - B - 1, D), F32)], axis=0)
    mods = _modulation(cc, ada_w, ada_b)

    dft_x = _hyena_consts(L)
    dft_c = _hyena_consts(Lc)

    tm_x = 512
    xf = x.reshape(B * L, D)
    cf = ctx.reshape(B * Lc, D)
    x_mod = lambda i: i // (L // tm_x)
    c_mod = lambda i: B

    def layer_params(i):
        return mods[i].reshape(n_cond, 1, 3 * D), norm_w[i][None, :], w_in[i].astype(BF16)

    mod, nw, w_in_bf = layer_params(0)
    u = _projection(xf, x_mod, tm_x, in_args=(mod, nw, w_in_bf))
    uc = _projection(cf, c_mod, Lc, in_args=(mod, nw, w_in_bf))
    for i in range(DEPTH):
        w_out_bf = w_out[i].astype(BF16)
        filt = (hy_w1[i], hy_b1[i], hy_w2[i], hy_b2[i], hy_w3[i], hy_sin_freq[i])
        last = i == DEPTH - 1
        nxt = None if last else layer_params(i + 1)

        za = _neighborhood_attention(u, uc, na_q_gain[i], na_k_gain[i], na_rpb[i], B, L, Lc)
        hr, hi, hm = _hyena_filter(L, *filt, dft_x)
        zy = _hyena(u, hy_conv_w[i], hy_conv_b[i], hy_skip[i], dft_x, hr, hi, hm, B, L)
        zr = _retention(u, uc, ret_log_rate[i], B, L, Lc, True)
        if not last:
            zac = _context_attention(uc, na_q_gain[i], na_k_gain[i], B, Lc)
            hrc, hic, hmc = _hyena_filter(Lc, *filt, dft_c)
            zyc = _hyena(uc, hy_conv_w[i], hy_conv_b[i], hy_skip[i], dft_c, hrc, hic, hmc, B, Lc)
            zrc = _retention(uc, uc, ret_log_rate[i], B, Lc, Lc, False)
            cf, uc = _projection(cf, c_mod, Lc, out_args=(zac, zyc, zrc, mod, w_out_bf), in_args=nxt)
            xf, u = _projection(xf, x_mod, tm_x, out_args=(za, zy, zr, mod, w_out_bf), in_args=nxt)
            mod = nxt[0]
        else:
            tm_last = 2 * tm_x
            xf = _projection(xf, lambda i: i // (L // tm_last), tm_last, out_args=(za, zy, zr, mod, w_out_bf))

    return xf.reshape(B, L, D)
```

```python
import functools
import math

import numpy as np
import jax
import jax.numpy as jnp
from jax import lax
from jax.experimental import pallas as pl
from jax.experimental.pallas import tpu as pltpu

F32 = jnp.float32
BF16 = jnp.bfloat16

D_MODEL = 1024
DEPTH = 2
GRID_W = 64
NA_HEADS = 8
NA_HEAD_DIM = 64
NA_W = NA_HEADS * NA_HEAD_DIM
NA_KH = 8
NA_KW = 16
HY_W = 256
HY_BANDS = 8
HY_EMB = 1 + 2 * HY_BANDS
HY_EMB_PAD = 32
HY_FFN = 64
HY_FAST_DECAY = 0.3
HY_SLOW_DECAY = 1.5
HY_TARGET = 1e-2
RET_HEADS = 4
RET_HEAD_DIM = 64
RET_W = RET_HEADS * RET_HEAD_DIM
ROPE_BASE = 10000.0
EPS = 1e-6
NEG_INF = -1e30
LOG2E = 1.4426950408889634
IN_W = 4 * NA_W + 4 * HY_W + 4 * RET_W
MIX_W = NA_W + HY_W + RET_W

CB = 256
CB_HY_V, CB_HY_X1, CB_HY_X2, CB_HY_G = 8, 9, 10, 11
CB_RE_Q, CB_RE_K, CB_RE_V, CB_RE_G = 12, 13, 14, 15

NA_G = 4
NA_WIN = NA_G + NA_KH
NA_GPS = 4
NA_KT = 256
NA_DR_MIN = -4
NA_DR_NUM = 23
RET_CHUNK = 256
HY_ROWS = 512
HY_BASE = 512
VMEM_LIMIT = 56 * 1024 * 1024


def _cparams(*sem):
    return pltpu.CompilerParams(dimension_semantics=sem, vmem_limit_bytes=VMEM_LIMIT)


def _resident(shape, index_map):
    return pl.BlockSpec(shape, index_map, pipeline_mode=pl.Buffered(1))


def _silu(x):
    return x * (1.0 / (1.0 + jnp.exp(-x)))


def _dot(a, b):
    return jnp.dot(a, b, preferred_element_type=F32)


def _dot_nt(a, b):
    return lax.dot_general(a, b, (((1,), (1,)), ((), ())), preferred_element_type=F32)


def _dot_tn(a, b):
    return lax.dot_general(a, b, (((0,), (0,)), ((), ())), preferred_element_type=F32)


@functools.lru_cache(maxsize=None)
def _bin_map(n, depth):
    if depth == 0:
        return [np.arange(n)], [n]
    hb, he = _bin_map(n // 2, depth - 1)
    blocks = [b for b in hb] + [n - b for b in hb]
    extras = list(he) + [n - e for e in he if e < n // 2]
    return blocks, extras


@functools.lru_cache(maxsize=None)
def _hy_plan(L):
    depth = 0
    while L >> depth > HY_BASE:
        depth += 1
    m = L >> depth
    k = np.arange(m, dtype=np.int64)[:, None] * np.arange(m, dtype=np.int64)[None, :]
    ang = (k % (2 * m)).astype(np.float64) * (math.pi / m)
    base = [np.cos(ang), np.sin(ang)]
    if depth:
        odd = (np.arange(m, dtype=np.int64)[:, None] * (2 * np.arange(m, dtype=np.int64)[None, :] + 1)) % (4 * m)
        odd = odd.astype(np.float64) * (math.pi / (2 * m))
        base += [np.cos(odd), np.sin(odd), np.cos(odd).T, np.sin(odd).T]
    base = tuple(np.ascontiguousarray(a).astype(np.float32) for a in base)
    tw_blocks, tw_extras = [], []
    for level in range(1, depth + 1):
        n = m << level
        hb, he = _bin_map(n // 2, level - 1)
        tw_blocks.append([(np.cos(math.pi * b / n), np.sin(math.pi * b / n)) for b in hb])
        tw_extras.append([(math.cos(math.pi * e / n), math.sin(math.pi * e / n), e == n // 2) for e in he])
    blocks, extras = _bin_map(L, depth)
    inv_n = 1.0 / (2 * L)
    seen = set()

    def weight(b):
        w = 0.0 if b in seen else (inv_n if b in (0, L) else 2.0 * inv_n)
        seen.add(b)
        return w

    w_blocks = [np.array([weight(int(b)) for b in blk]) for blk in blocks]
    w_extras = [weight(int(e)) for e in extras]
    tables = [np.broadcast_to(t[:, None], (m, HY_W)) for lvl in tw_blocks for pair in lvl for t in pair]
    tw_table = np.stack(tables).astype(np.float32) if tables else np.zeros((1, 8, HY_W), np.float32)
    w_table = np.stack([np.broadcast_to(w[:, None], (m, HY_W)) for w in w_blocks]).astype(np.float32)
    return dict(depth=depth, m=m, base=base, tw_extras=tw_extras, n_blocks=len(blocks), n_extras=len(extras),
                w_extras=w_extras, tw_table=tw_table, w_table=w_table,
                tw_index=[[2 * (sum(len(l) for l in tw_blocks[:lv]) + i) for i in range(len(tw_blocks[lv]))]
                          for lv in range(depth)])


@functools.lru_cache(maxsize=None)
def _filter_features(L):
    t = np.linspace(0.0, 1.0, L)[:, None]
    omega = 2.0 * math.pi * np.arange(L)[:, None] / L
    bands = np.linspace(1e-4, HY_BANDS - 1, HY_BANDS)[None, :]
    z = np.concatenate([t, np.cos(bands * omega), -np.sin(bands * omega)], axis=-1)
    z = np.pad(z, ((0, 0), (0, HY_EMB_PAD - HY_EMB)))
    deltas = np.abs(np.linspace(math.log(HY_TARGET) / HY_SLOW_DECAY, math.log(HY_TARGET) / HY_FAST_DECAY, HY_W))
    window = np.exp(-t * deltas[None, :])
    return z.astype(np.float32), window.astype(np.float32)


@functools.lru_cache(maxsize=None)
def _rope_tables(L):
    half = RET_HEAD_DIM // 2
    quarter = half // 2
    t = np.arange(L)
    pos = np.stack([t // GRID_W, t % GRID_W], axis=0).astype(np.float64)
    freqs = ROPE_BASE ** (-np.arange(quarter, dtype=np.float64) / quarter)
    cos = np.zeros((L, RET_HEAD_DIM))
    sin = np.zeros((L, RET_HEAD_DIM))
    for a in range(2):
        ang = pos[a][:, None] * freqs[None, :]
        base = a * half
        cos[:, base:base + quarter] = np.cos(ang)
        cos[:, base + quarter:base + half] = np.cos(ang)
        sin[:, base:base + quarter] = -np.sin(ang)
        sin[:, base + quarter:base + half] = np.sin(ang)
    return (np.tile(cos, (1, RET_HEADS)).astype(np.float32), np.tile(sin, (1, RET_HEADS)).astype(np.float32))


@functools.lru_cache(maxsize=None)
def _head_block_ones(width):
    i = np.arange(width) // NA_HEAD_DIM
    return (i[:, None] == i[None, :]).astype(np.float32)


def _mod_kernel(c_ref, w_ref, b_ref, o_ref):
    s = _silu(c_ref[...])
    w = w_ref[...]
    s_hi = s.astype(BF16)
    w_hi = w.astype(BF16)
    s_lo = (s - s_hi.astype(F32)).astype(BF16)
    w_lo = (w - w_hi.astype(F32)).astype(BF16)
    o_ref[...] = _dot(s_hi, w_hi) + (_dot(s_hi, w_lo) + _dot(s_lo, w_hi)) + b_ref[...]


def _modulation(cc, ada_w, ada_b):
    R = cc.shape[0]
    tn = 1024
    return pl.pallas_call(
        _mod_kernel,
        grid=(DEPTH, 3 * D_MODEL // tn),
        in_specs=[
            pl.BlockSpec((R, D_MODEL), lambda i, j: (0, 0)),
            pl.BlockSpec((None, D_MODEL, tn), lambda i, j: (i, 0, j)),
            pl.BlockSpec((None, 1, tn), lambda i, j: (i, 0, j)),
        ],
        out_specs=pl.BlockSpec((None, R, tn), lambda i, j: (i, 0, j)),
        out_shape=jax.ShapeDtypeStruct((DEPTH, R, 3 * D_MODEL), F32),
        compiler_params=_cparams("parallel", "parallel"),
        name="modulation",
    )(cc, ada_w, ada_b.reshape(DEPTH, 1, 3 * D_MODEL))


def _proj_kernel(*refs, with_out, with_in):
    refs = list(refs)
    if with_out:
        za_ref, zy_ref, zr_ref, x_ref, mod_o_ref, w_out_ref = refs[:6]
        del refs[:6]
    else:
        x_ref = refs.pop(0)
    if with_in:
        mod_i_ref, nw_ref, w_in_ref = refs[:3]
        del refs[:3]
    x = x_ref[...]
    if with_out:
        acc = _dot(za_ref[...], w_out_ref[0:NA_W, :])
        acc += _dot(zy_ref[...], w_out_ref[NA_W:NA_W + HY_W, :])
        acc += _dot(zr_ref[...], w_out_ref[NA_W + HY_W:MIX_W, :])
        x = x + mod_o_ref[:, 2 * D_MODEL:3 * D_MODEL] * acc
        refs.pop(0)[...] = x
    if with_in:
        u_ref = refs.pop(0)
        xn = x * lax.rsqrt(jnp.mean(x * x, axis=-1, keepdims=True) + EPS)
        shift = mod_i_ref[:, 0:D_MODEL]
        scale = mod_i_ref[:, D_MODEL:2 * D_MODEL]
        h = (xn * nw_ref[...] * (1.0 + scale) + shift).astype(BF16)
        tn = 1024
        for j in range(IN_W // tn):
            u_ref[:, j * tn:(j + 1) * tn] = _dot(h, w_in_ref[:, j * tn:(j + 1) * tn]).astype(u_ref.dtype)


def _projection(xf, mod_index, tm, out_args=None, in_args=None):
    R = xf.shape[0]
    row = lambda w: pl.BlockSpec((tm, w), lambda i: (i, 0))
    mod_spec = pl.BlockSpec((None, 1, 3 * D_MODEL), lambda i: (mod_index(i), 0, 0))
    args, in_specs, out_specs, out_shape = [], [], [], []
    if out_args is not None:
        za, zy, zr, mod_o, (w_out_bf, layer_o) = out_args
        args += [za, zy, zr, xf, mod_o, w_out_bf]
        in_specs += [row(NA_W), row(HY_W), row(RET_W), row(D_MODEL), mod_spec,
                     _resident((None, MIX_W, D_MODEL), lambda i: (layer_o, 0, 0))]
        out_specs.append(row(D_MODEL))
        out_shape.append(jax.ShapeDtypeStruct((R, D_MODEL), F32))
    else:
        args.append(xf)
        in_specs.append(row(D_MODEL))
    if in_args is not None:
        mod_i, norm_w, (w_in_bf, layer_i) = in_args
        args += [mod_i, norm_w, w_in_bf]
        in_specs += [mod_spec, _resident((1, D_MODEL), lambda i: (0, 0)),
                     _resident((None, D_MODEL, IN_W), lambda i: (layer_i, 0, 0))]
        out_specs.append(row(IN_W))
        out_shape.append(jax.ShapeDtypeStruct((R, IN_W), BF16))
    outs = pl.pallas_call(
        functools.partial(_proj_kernel, with_out=out_args is not None, with_in=in_args is not None),
        grid=(R // tm,),
        in_specs=in_specs,
        out_specs=out_specs,
        out_shape=out_shape,
        compiler_params=_cparams("parallel"),
        name="projection",
    )(*args)
    return outs if len(outs) > 1 else outs[0]


def _head_rms(x, ones_bf, gain):
    ss = _dot((x * x).astype(BF16), ones_bf)
    return x * lax.rsqrt(ss * (1.0 / NA_HEAD_DIM) + EPS) * gain


def _pair_masks():
    lane = lax.broadcasted_iota(jnp.int32, (1, 2 * NA_HEAD_DIM), 1)
    return lane < NA_HEAD_DIM


def _attend_heads(q, key_parts, val_parts, bias_fn):
    first = _pair_masks()
    outs = []
    for pair in range(NA_HEADS // 2):
        lo = pair * 2 * NA_HEAD_DIM
        hi = lo + 2 * NA_HEAD_DIM
        qp = q[:, lo:hi]
        o_pair = None
        for sub in range(2):
            h = 2 * pair + sub
            sel = first if sub == 0 else jnp.logical_not(first)
            qm = jnp.where(sel, qp, 0.0).astype(BF16)
            scores = []
            for part, kp in enumerate(key_parts):
                s = _dot_nt(qm, kp[:, lo:hi])
                b = bias_fn(h, part)
                if b is not None:
                    s = s + b
                scores.append(s)
            m = scores[0].max(axis=-1, keepdims=True)
            for s in scores[1:]:
                m = jnp.maximum(m, s.max(axis=-1, keepdims=True))
            denom = None
            acc = None
            for s, vp in zip(scores, val_parts):
                p = jnp.exp(s - m)
                ps = p.sum(axis=-1, keepdims=True)
                denom = ps if denom is None else denom + ps
                pv = _dot(p.astype(BF16), vp[:, lo:hi])
                acc = pv if acc is None else acc + pv
            o_h = acc * (1.0 / denom)
            o_pair = o_h if o_pair is None else jnp.where(first, o_pair, o_h)
        outs.append(o_pair)
    return jnp.concatenate(outs, axis=-1)


def _na_kernel(q_ref, k_ref, v_ref, g_ref, kc_ref, vc_ref, qg_ref, kg_ref, ones_ref, rpb_ref, o_ref,
               kn_scr, kcn_scr, vat_scr, vbt_scr, vcat_scr, vcbt_scr, bias_scr, mask_scr, s0_scr, s1_scr,
               p0_scr, p1_scr, qm_scr, ot_scr, *, rows):
    batch = pl.program_id(0)
    grp = pl.program_id(1)
    ones_bf = ones_ref[...]
    tq = NA_G * GRID_W
    nwin = NA_WIN * GRID_W
    Lc = kc_ref.shape[0]
    pair_w = 2 * NA_HEAD_DIM
    kt = NA_KT
    first = lax.broadcasted_iota(jnp.int32, (1, pair_w), 1) < NA_HEAD_DIM

    @pl.when((batch == 0) & (grp == 0))
    def _():
        ck = lax.broadcasted_iota(jnp.int32, (GRID_W, pair_w), 0)
        lane = lax.broadcasted_iota(jnp.int32, (GRID_W, pair_w), 1)
        cq = lane % GRID_W
        col_start = jnp.clip(cq - NA_KW // 2, 0, GRID_W - NA_KW)
        col_ok = (ck >= col_start) & (ck < col_start + NA_KW)
        left = lane < GRID_W

        def body(i, carry):
            for h in range(NA_HEADS):
                v0 = jnp.broadcast_to(rpb_ref[i + 1, h:h + 1, :], (GRID_W, pair_w))
                v1 = jnp.broadcast_to(rpb_ref[i, h:h + 1, :], (GRID_W, pair_w))
                t0 = pltpu.roll(v0, pair_w - (NA_KW - 1), 1, stride=1, stride_axis=0)
                t1 = pltpu.roll(v1, GRID_W - (NA_KW - 1), 1, stride=1, stride_axis=0)
                bias_scr[i, h] = jnp.where(col_ok, jnp.where(left, t0, t1), NEG_INF)
            return carry

        lax.fori_loop(0, NA_DR_NUM, body, 0)

    @pl.when(grp == 0)
    def _():
        kgain = kg_ref[...]
        own_a = (lax.broadcasted_iota(jnp.int32, (NA_W, 1), 0) % pair_w) < NA_HEAD_DIM
        for src, dst, vat, vbt, vsrc in ((k_ref, kn_scr, vat_scr, vbt_scr, v_ref),
                                         (kc_ref, kcn_scr, vcat_scr, vcbt_scr, vc_ref)):
            for i in range(src.shape[0] // kt):
                sl = slice(i * kt, (i + 1) * kt)
                dst[sl, :] = _head_rms(src[sl, :].astype(F32), ones_bf, kgain).astype(BF16)
                vt = vsrc[sl, :].astype(F32).T
                vat[i] = jnp.where(own_a, vt, 1.0).astype(BF16)
                vbt[i] = jnp.where(own_a, 1.0, vt).astype(BF16)

    n_kt = (nwin + Lc) // kt
    rows_per_tile = kt // GRID_W
    n_ch = (nwin + Lc) // GRID_W
    n_items = NA_GPS * NA_HEADS

    s_bufs = (s0_scr, s1_scr)
    p_bufs = (p0_scr, p1_scr)
    row_zero = jnp.minimum(grp, 0)

    def rows_of(start, size):
        return pl.ds(pl.multiple_of(row_zero + start, GRID_W), size)

    def head_lanes(h):
        return slice((h // 2) * pair_w, (h // 2 + 1) * pair_w)

    key0, tile0, dr_base = [], [], []
    for gi in range(NA_GPS):
        r0 = (grp * NA_GPS + gi) * NA_G
        ws = jnp.clip(r0 - NA_KH // 2, 0, rows - NA_WIN)
        key0.append(pl.multiple_of(ws * GRID_W, kt))
        tile0.append(ws // rows_per_tile)
        dr_base.append(ws - r0 + (NA_KH - 1) - NA_DR_MIN)
        w_id = lax.broadcasted_iota(jnp.int32, (16, tq), 0)
        g_id = lax.broadcasted_iota(jnp.int32, (16, tq), 1) // GRID_W
        lo = jnp.clip(r0 + g_id - NA_KH // 2, 0, rows - NA_KH) - ws
        mask_scr[gi] = jnp.where((w_id >= lo) & (w_id < lo + NA_KH), 0.0, NEG_INF).astype(F32)
        q = _head_rms(q_ref[gi * tq:(gi + 1) * tq, :].astype(F32), ones_bf,
                      qg_ref[...] * (NA_HEAD_DIM ** -0.5 * LOG2E))
        for h in range(NA_HEADS):
            sel = first if h % 2 == 0 else jnp.logical_not(first)
            qm_scr[gi * NA_HEADS + h, rows_of(0, tq), :] = jnp.where(sel, q[:, head_lanes(h)], 0.0).astype(BF16)

    col_max = {}

    def qk_piece(n, j):
        gi, h = divmod(n, NA_HEADS)
        last = j == n_kt - 1
        keys = kcn_scr[:, head_lanes(h)] if last else kn_scr[pl.ds(key0[gi] + j * kt, kt), head_lanes(h)]
        sv = _dot_nt(keys, qm_scr[n, rows_of(0, tq), :])
        for dw in range(rows_per_tile):
            sw = sv[dw * GRID_W:(dw + 1) * GRID_W, :]
            if not last:
                w = j * rows_per_tile + dw
                bias = jnp.concatenate([bias_scr[dr_base[gi] + (w - g), h] for g in range(0, NA_G, 2)], axis=-1)
                sw = sw + bias + mask_scr[gi, w:w + 1, :]
            s_bufs[n % 2][rows_of(j * kt + dw * GRID_W, GRID_W), :] = sw
            part = sw.reshape(GRID_W // 8, 8, tq).max(axis=0)
            col_max[n] = part if (j == 0 and dw == 0) else jnp.maximum(col_max[n], part)

    def exp_chunk(n, c):
        if c == 0:
            col_max[n] = col_max[n].max(axis=0, keepdims=True)
        rows_c = rows_of(c * GRID_W, GRID_W)
        p_bufs[n % 2][rows_c, :] = jnp.exp2(s_bufs[n % 2][rows_c, :] - col_max[n]).astype(BF16)
        if c == n_ch - 1:
            del col_max[n]

    def pv(n):
        gi, h = divmod(n, NA_HEADS)
        vt, vct = (vat_scr, vcat_scr) if h % 2 == 0 else (vbt_scr, vcbt_scr)
        ot = _dot(vct[0, head_lanes(h), :], p_bufs[n % 2][rows_of(nwin, Lc), :])
        for j in range(n_kt - 1):
            ot += _dot(vt[tile0[gi] + j, head_lanes(h), :], p_bufs[n % 2][rows_of(j * kt, kt), :])
        own, den = (ot[0:NA_HEAD_DIM], ot[NA_HEAD_DIM:]) if h % 2 == 0 else (ot[NA_HEAD_DIM:], ot[0:NA_HEAD_DIM])
        ot_scr[rows_of(n * NA_HEAD_DIM, NA_HEAD_DIM), :] = own * (1.0 / den)
        if h % 2 == 1:
            o_pair = ot_scr[rows_of((n - 1) * NA_HEAD_DIM, pair_w), :].T
            q_rows = slice(gi * tq, (gi + 1) * tq)
            gate = _silu(g_ref[q_rows, head_lanes(h)].astype(F32))
            o_ref[q_rows, head_lanes(h)] = (o_pair * gate).astype(o_ref.dtype)

    for t in range(n_items + 2):
        for i in range(n_ch):
            if 0 <= t - 1 < n_items:
                exp_chunk(t - 1, i)
            if i % 4 == 0 and t < n_items and i // 4 < n_kt:
                qk_piece(t, i // 4)
            if i == 2 and 0 <= t - 2:
                pv(t - 2)


def _neighborhood_attention(u, uc, q_gain, k_gain, rpb, B, L, Lc):
    rows = L // GRID_W
    ngrp = rows // (NA_G * NA_GPS)
    tq = NA_G * GRID_W
    tstep = tq * NA_GPS
    ones = jnp.asarray(_head_block_ones(NA_W)).astype(BF16)
    qg = jnp.tile(q_gain, NA_HEADS)[None, :]
    kg = jnp.tile(k_gain, NA_HEADS)[None, :]
    n_dr = 2 * NA_KH - 1
    rpb_t = jnp.transpose(rpb.astype(F32)[:, :, ::-1] * LOG2E, (1, 0, 2))
    lo_pad = 1 - NA_DR_MIN
    rpb_t = jnp.pad(rpb_t, ((lo_pad, NA_DR_NUM + 1 - lo_pad - n_dr), (0, 0), (0, 2 * GRID_W - (2 * NA_KW - 1))))
    return pl.pallas_call(
        functools.partial(_na_kernel, rows=rows),
        grid=(B, ngrp),
        in_specs=[
            pl.BlockSpec((tstep, NA_W), lambda b, g: (b * ngrp + g, 0)),
            pl.BlockSpec((L, NA_W), lambda b, g: (b, 1)),
            pl.BlockSpec((L, NA_W), lambda b, g: (b, 2)),
            pl.BlockSpec((tstep, NA_W), lambda b, g: (b * ngrp + g, 3)),
            pl.BlockSpec((Lc, NA_W), lambda b, g: (b, 1)),
            pl.BlockSpec((Lc, NA_W), lambda b, g: (b, 2)),
            _resident((1, NA_W), lambda b, g: (0, 0)),
            _resident((1, NA_W), lambda b, g: (0, 0)),
            _resident((NA_W, NA_W), lambda b, g: (0, 0)),
            _resident((NA_DR_NUM + 1, NA_HEADS, 2 * GRID_W), lambda b, g: (0, 0, 0)),
        ],
        out_specs=pl.BlockSpec((tstep, NA_W), lambda b, g: (b * ngrp + g, 0)),
        out_shape=jax.ShapeDtypeStruct((B * L, NA_W), BF16),
        scratch_shapes=[
            pltpu.VMEM((L, NA_W), BF16), pltpu.VMEM((Lc, NA_W), BF16),
            pltpu.VMEM((L // NA_KT, NA_W, NA_KT), BF16), pltpu.VMEM((L // NA_KT, NA_W, NA_KT), BF16),
            pltpu.VMEM((Lc // NA_KT, NA_W, NA_KT), BF16), pltpu.VMEM((Lc // NA_KT, NA_W, NA_KT), BF16),
            pltpu.VMEM((NA_DR_NUM, NA_HEADS, GRID_W, 2 * GRID_W), F32),
            pltpu.VMEM((NA_GPS, 16, tq), F32),
            pltpu.VMEM((NA_WIN * GRID_W + Lc, tq), F32), pltpu.VMEM((NA_WIN * GRID_W + Lc, tq), F32),
            pltpu.VMEM((NA_WIN * GRID_W + Lc, tq), BF16), pltpu.VMEM((NA_WIN * GRID_W + Lc, tq), BF16),
            pltpu.VMEM((NA_GPS * NA_HEADS, tq, 2 * NA_HEAD_DIM), BF16), pltpu.VMEM((NA_GPS * NA_W, tq), F32),
        ],
        compiler_params=_cparams("arbitrary", "arbitrary"),
        name="neighborhood_attention",
    )(u, u, u, u, uc, uc, qg, kg, ones, rpb_t)


def _ctx_attn_kernel(q_ref, k_ref, v_ref, g_ref, qg_ref, kg_ref, ones_ref, o_ref):
    ones_bf = ones_ref[...]
    q = _head_rms(q_ref[...].astype(F32), ones_bf, qg_ref[...] * (NA_HEAD_DIM ** -0.5))
    k = _head_rms(k_ref[...].astype(F32), ones_bf, kg_ref[...]).astype(BF16)
    o = _attend_heads(q, [k], [v_ref[...]], lambda h, part: None)
    o_ref[...] = (o * _silu(g_ref[...].astype(F32))).astype(o_ref.dtype)


def _context_attention(uc, q_gain, k_gain, B, Lc):
    ones = jnp.asarray(_head_block_ones(NA_W)).astype(BF16)
    qg = jnp.tile(q_gain, NA_HEADS)[None, :]
    kg = jnp.tile(k_gain, NA_HEADS)[None, :]
    return pl.pallas_call(
        _ctx_attn_kernel,
        grid=(B,),
        in_specs=[
            pl.BlockSpec((Lc, NA_W), lambda b: (b, 0)),
            pl.BlockSpec((Lc, NA_W), lambda b: (b, 1)),
            pl.BlockSpec((Lc, NA_W), lambda b: (b, 2)),
            pl.BlockSpec((Lc, NA_W), lambda b: (b, 3)),
            _resident((1, NA_W), lambda b: (0, 0)),
            _resident((1, NA_W), lambda b: (0, 0)),
            _resident((NA_W, NA_W), lambda b: (0, 0)),
        ],
        out_specs=pl.BlockSpec((Lc, NA_W), lambda b: (b, 0)),
        out_shape=jax.ShapeDtypeStruct((B * Lc, NA_W), BF16),
        compiler_params=_cparams("parallel"),
        name="context_attention",
    )(uc, uc, uc, uc, qg, kg, ones)


def _alt_sign(shape):
    row = lax.broadcasted_iota(jnp.int32, shape, 0)
    return (1 - 2 * (row & 1)).astype(F32)


def _lane_halves(x):
    return [x[:, h * 128:(h + 1) * 128] for h in range(HY_W // 128)]


class _Spec:
    def __init__(self, c, s):
        self.c, self.s = list(c), list(s)


def _combine(P, R, tw):
    lo_c, lo_s, mi_c, mi_s = [], [], [], []
    for (pc, ps, rc, rs, (ct, st, single)) in zip(P.c, P.s, R.c, R.s, tw):
        if single:
            lo_c.append(pc)
            lo_s.append(rc)
            continue
        qc, qs = (rc, rs) if ct is None else (ct * rc - st * rs, st * rc + ct * rs)
        lo_c.append(pc + qc)
        lo_s.append(ps + qs)
        mi_c.append(pc - qc)
        mi_s.append(qs - ps)
    return _Spec(lo_c + mi_c, lo_s + mi_s)


def _combine_t(Z, tw):
    n_lo = len(tw)
    lo_c, lo_s, mi_c, mi_s = Z.c[:n_lo], Z.s[:n_lo], Z.c[n_lo:], Z.s[n_lo:]
    pc, ps, rc, rs = [], [], [], []
    mi = 0
    for i, (ct, st, single) in enumerate(tw):
        if single:
            pc.append(lo_c[i])
            ps.append(jnp.zeros_like(lo_c[i]))
            rc.append(lo_s[i])
            rs.append(jnp.zeros_like(lo_c[i]))
            continue
        g_qc = lo_c[i] - mi_c[mi]
        g_qs = lo_s[i] + mi_s[mi]
        pc.append(lo_c[i] + mi_c[mi])
        ps.append(lo_s[i] - mi_s[mi])
        rc.append(g_qc if ct is None else ct * g_qc + st * g_qs)
        rs.append(g_qs if ct is None else ct * g_qs - st * g_qc)
        mi += 1
    return _Spec(pc, ps), _Spec(rc, rs)


def _forward(level, leaves, tws):
    if level == 0:
        return leaves[0]
    half = len(leaves) // 2
    return _combine(_forward(level - 1, leaves[:half], tws), _forward(level - 1, leaves[half:], tws), tws[level - 1])


def _backward(level, Z, tws):
    if level == 0:
        return [Z]
    P, R = _combine_t(Z, tws[level - 1])
    return _backward(level - 1, P, tws) + _backward(level - 1, R, tws)


def _leaf_samples(depth, off=0, stride=1):
    if depth == 0:
        return [(off, stride)]
    return _leaf_samples(depth - 1, off, 2 * stride) + _leaf_samples(depth - 1, off + stride, 2 * stride)


def _block_twiddles(plan, tw_ref, k0, k1):
    return [[(None, None, False) if lv == 0 else (tw_ref[i, k0:k1, :], tw_ref[i + 1, k0:k1, :], False)
             for i in plan["tw_index"][lv]] for lv in range(plan["depth"])]


def _leaf_matrices(plan, mats, i, transposed):
    if plan["depth"] == 0 or i % 2 == 0:
        return mats[0], mats[1]
    return (mats[4], mats[5]) if transposed else (mats[2], mats[3])


def _spectrum_of(plan, src_scr, xbf_scr, mats, tw_ref, consume):
    depth, m = plan["depth"], plan["m"]
    cm = min(HY_ROWS // 2, m)
    sign = _alt_sign((cm, HY_W))
    tops = []
    for i, (off, stride) in enumerate(_leaf_samples(depth)):
        top = jnp.zeros((1, HY_W), F32)
        for j0 in range(0, m, cm):
            x = jnp.concatenate([src_scr[h, pl.ds(off + stride * j0, cm, stride=stride), :]
                                 for h in range(HY_W // 128)], axis=1)
            xbf_scr[i, j0:j0 + cm, :] = x.astype(BF16)
            top += jnp.sum(x * sign, axis=0, keepdims=True)
        tops.append(top)
    for k0 in range(0, m, cm):
        k1 = k0 + cm
        base = []
        for i in range(len(tops)):
            c_ref, s_ref = _leaf_matrices(plan, mats, i, False)
            base.append(_Spec([_dot(c_ref[k0:k1, :], xbf_scr[i])], [_dot(s_ref[k0:k1, :], xbf_scr[i])]))
        consume(k0, k1, _forward(depth, base, _block_twiddles(plan, tw_ref, k0, k1)))
    zero = jnp.zeros((1, HY_W), F32)
    return _forward(depth, [_Spec([t], [zero]) for t in tops], plan["tw_extras"])


def _sequence_of(plan, gc_scr, gs_scr, tops, mats, dst_scr):
    depth, m = plan["depth"], plan["m"]
    cm = min(HY_ROWS // 2, m)
    sign = _alt_sign((cm, HY_W))
    for i, (off, stride) in enumerate(_leaf_samples(depth)):
        c_ref, s_ref = _leaf_matrices(plan, mats, i, True)
        for j0 in range(0, m, cm):
            y = _dot(c_ref[j0:j0 + cm, :], gc_scr[i]) + _dot(s_ref[j0:j0 + cm, :], gs_scr[i]) + sign * tops[i]
            for h in range(HY_W // 128):
                dst_scr[h, pl.ds(off + stride * j0, cm, stride=stride), :] = y[:, h * 128:(h + 1) * 128]


def _filter_kernel(z_ref, win_ref, w1_ref, b1_ref, w2_ref, b2_ref, w3_ref, sf_ref, *refs, L):
    plan = _hy_plan(L)
    n_mats = len(plan["base"])
    mats = refs[:n_mats]
    tw_ref, wt_ref, hr_ref, hi_ref, hx_ref, h_scr, a_scr, d_scr, xbf_scr = refs[n_mats:]
    hp = lax.Precision.HIGHEST
    dot = lambda a, b: jnp.dot(a, b, preferred_element_type=F32, precision=hp)

    @pl.when(pl.program_id(0) == 0)
    def _():
        h = jnp.sin(sf_ref[0:1, :] * (dot(z_ref[...], w1_ref[...]) + b1_ref[...]))
        h_scr[...] = jnp.sin(sf_ref[1:2, :] * (dot(h, w2_ref[...]) + b2_ref[...]))

    taps = dot(h_scr[...], w3_ref[...])
    win = win_ref[...]
    row = lax.broadcasted_iota(jnp.int32, (L, HY_W), 0)
    hf = jnp.concatenate([taps[:, 0:HY_W], taps[:, 2 * HY_W:3 * HY_W]], axis=0) * win
    hb = jnp.concatenate([taps[:, HY_W:2 * HY_W], taps[:, 3 * HY_W:4 * HY_W]], axis=0) * win
    hb = jnp.where(row == 0, 0.0, hb)
    norm = jnp.sum(jnp.abs(hf), axis=0, keepdims=True) + jnp.sum(jnp.abs(hb), axis=0, keepdims=True)
    inv = 1.0 / norm
    for h, (a_half, d_half) in enumerate(zip(_lane_halves((hf + hb) * inv), _lane_halves((hf - hb) * inv))):
        a_scr[h] = a_half
        d_scr[h] = d_half

    def store_real(k0, k1, spec):
        for b, blk in enumerate(spec.c):
            hr_ref[b, k0:k1, :] = blk * wt_ref[b, k0:k1, :]

    def store_imag(k0, k1, spec):
        for b, blk in enumerate(spec.s):
            hi_ref[b, k0:k1, :] = -blk * wt_ref[b, k0:k1, :]

    extra_a = _spectrum_of(plan, a_scr, xbf_scr, mats, tw_ref, store_real)
    extra_d = _spectrum_of(plan, d_scr, xbf_scr, mats, tw_ref, store_imag)
    for e, w in enumerate(plan["w_extras"]):
        hx_ref[0, e] = extra_a.c[e] * w
        hx_ref[1, e] = -extra_d.s[e] * w


def _block_diag2(w):
    z = jnp.zeros_like(w)
    return jnp.concatenate([jnp.concatenate([w, z], axis=1), jnp.concatenate([z, w], axis=1)], axis=0)


def _hyena_consts(L):
    plan = _hy_plan(L)
    mats = tuple(jnp.asarray(a).astype(BF16) for a in plan["base"])
    return mats, jnp.asarray(plan["tw_table"]), jnp.asarray(plan["w_table"])


def _hyena_filter(L, w1, b1, w2, b2, w3, sin_freq, consts):
    plan = _hy_plan(L)
    m, nb, ne = plan["m"], plan["n_blocks"], plan["n_extras"]
    mats, tw, wt = consts
    z, window = _filter_features(L)
    z2 = np.concatenate([z[:L // 2], z[L // 2:]], axis=1)
    w1d = _block_diag2(jnp.pad(w1, ((0, HY_EMB_PAD - HY_EMB), (0, 0))))
    w2d = _block_diag2(w2)
    w3d = jnp.stack([_block_diag2(w3[:, o * 2 * HY_W:(o + 1) * 2 * HY_W]) for o in range(2)])
    pair = lambda v: jnp.tile(v, (1, 2))
    full = lambda *shape: _resident(shape, lambda o: (0,) * len(shape))
    spectrum = pl.BlockSpec((None, nb, m, HY_W), lambda o: (o, 0, 0, 0))
    return pl.pallas_call(
        functools.partial(_filter_kernel, L=L),
        grid=(2,),
        in_specs=[
            full(L // 2, 2 * HY_EMB_PAD), full(L, HY_W), full(2 * HY_EMB_PAD, 2 * HY_FFN), full(1, 2 * HY_FFN),
            full(2 * HY_FFN, 2 * HY_FFN), full(1, 2 * HY_FFN),
            pl.BlockSpec((None, 2 * HY_FFN, 4 * HY_W), lambda o: (o, 0, 0)),
            full(2, 2 * HY_FFN), *[full(m, m) for _ in mats], full(*tw.shape), full(*wt.shape),
        ],
        out_specs=[spectrum, spectrum, pl.BlockSpec((None, 2, ne, 1, HY_W), lambda o: (o, 0, 0, 0, 0))],
        out_shape=[jax.ShapeDtypeStruct((2, nb, m, HY_W), F32), jax.ShapeDtypeStruct((2, nb, m, HY_W), F32),
                   jax.ShapeDtypeStruct((2, 2, ne, 1, HY_W), F32)],
        scratch_shapes=[pltpu.VMEM((L // 2, 2 * HY_FFN), F32), pltpu.VMEM((HY_W // 128, L, 128), F32),
                        pltpu.VMEM((HY_W // 128, L, 128), F32), pltpu.VMEM((nb, m, HY_W), BF16)],
        compiler_params=_cparams("arbitrary"),
        name="hyena_filter",
    )(jnp.asarray(z2), jnp.asarray(window), w1d, pair(b1[None, :]), w2d, pair(b2[None, :]), w3d, pair(sin_freq),
      *mats, tw, wt)


def _hyena_kernel(v_ref, x1_ref, x2_ref, g_ref, cw_ref, cb_ref, skip_ref, *refs, L):
    plan = _hy_plan(L)
    depth = plan["depth"]
    n_mats = len(plan["base"])
    mats = refs[:n_mats]
    tw_ref, hr_ref, hi_ref, hx_ref, o_ref, a_scr, x_scr, y_scr, xbf_scr, gc_scr, gs_scr = refs[n_mats:]
    ck = min(HY_ROWS, L)
    halo = 16
    chunks = [(t0, t0 + ck) for t0 in range(0, L, ck)]

    def split_store(scr, t0, t1, val):
        for h, part in enumerate(_lane_halves(val)):
            scr[h, t0:t1, :] = part

    def joined(scr, t0, t1):
        return jnp.concatenate([scr[h, t0:t1, :] for h in range(HY_W // 128)], axis=1)

    def short_conv(ref, j, t0, t1):
        lo, hi = max(t0 - halo, 0), min(t1 + halo, L)
        u = ref[lo:hi, :].astype(F32)
        row = lax.broadcasted_iota(jnp.int32, (hi - lo, HY_W), 0)
        prev = pltpu.roll(u, 1, 0)
        nxt = pltpu.roll(u, hi - lo - 1, 0)
        if lo == 0:
            prev = jnp.where(row == 0, 0.0, prev)
        if hi == L:
            nxt = jnp.where(row == hi - lo - 1, 0.0, nxt)
        w = cw_ref[:, j * HY_W:(j + 1) * HY_W]
        z = prev * w[0:1, :] + u * w[1:2, :] + nxt * w[2:3, :] + cb_ref[:, j * HY_W:(j + 1) * HY_W]
        return z[t0 - lo:t1 - lo, :]

    def long_conv(order):
        def product(spec, hr, hi):
            return _Spec([c * r + s * i for c, s, r, i in zip(spec.c, spec.s, hr, hi)],
                         [s * r - c * i for c, s, r, i in zip(spec.c, spec.s, hr, hi)])

        def consume(k0, k1, spec):
            nb = len(spec.c)
            z = product(spec, [hr_ref[order, b, k0:k1, :] for b in range(nb)],
                        [hi_ref[order, b, k0:k1, :] for b in range(nb)])
            for i, leaf in enumerate(_backward(depth, z, _block_twiddles(plan, tw_ref, k0, k1))):
                gc_scr[i, k0:k1, :] = leaf.c[0].astype(BF16)
                gs_scr[i, k0:k1, :] = leaf.s[0].astype(BF16)

        extras = _spectrum_of(plan, a_scr, xbf_scr, mats, tw_ref, consume)
        ne = len(extras.c)
        zx = product(extras, [hx_ref[order, 0, e] for e in range(ne)], [hx_ref[order, 1, e] for e in range(ne)])
        tops = [leaf.c[0] for leaf in _backward(depth, zx, plan["tw_extras"])]
        _sequence_of(plan, gc_scr, gs_scr, tops, mats, y_scr)

    for t0, t1 in chunks:
        split_store(a_scr, t0, t1, short_conv(v_ref, 0, t0, t1))
        x_scr[t0:t1, :] = short_conv(x1_ref, 1, t0, t1)
    long_conv(0)
    for t0, t1 in chunks:
        conv = joined(y_scr, t0, t1) + joined(a_scr, t0, t1) * skip_ref[0:1, :]
        split_store(a_scr, t0, t1, x_scr[t0:t1, :] * conv)
    long_conv(1)
    for t0, t1 in chunks:
        conv = joined(y_scr, t0, t1) + joined(a_scr, t0, t1) * skip_ref[1:2, :]
        y = short_conv(x2_ref, 2, t0, t1) * conv
        o_ref[t0:t1, :] = (y * _silu(g_ref[t0:t1, :].astype(F32))).astype(o_ref.dtype)


def _hyena(u, conv_w, conv_b, skip, consts, hr, hi, hx, B, L):
    plan = _hy_plan(L)
    m, nb, ne = plan["m"], plan["n_blocks"], plan["n_extras"]
    mats, tw, _ = consts
    col = lambda j: pl.BlockSpec((L, CB), lambda b: (b, j))
    const = lambda *shape: _resident(shape, lambda b: (0,) * len(shape))
    halves = HY_W // 128
    return pl.pallas_call(
        functools.partial(_hyena_kernel, L=L),
        grid=(B,),
        in_specs=[
            col(CB_HY_V), col(CB_HY_X1), col(CB_HY_X2), col(CB_HY_G),
            const(3, 3 * HY_W), const(1, 3 * HY_W), const(2, HY_W),
            *[const(m, m) for _ in mats], const(*tw.shape),
            const(2, nb, m, HY_W), const(2, nb, m, HY_W), const(2, 2, ne, 1, HY_W),
        ],
        out_specs=pl.BlockSpec((L, HY_W), lambda b: (b, 0)),
        out_shape=jax.ShapeDtypeStruct((B * L, HY_W), BF16),
        scratch_shapes=[
            pltpu.VMEM((halves, L, 128), F32), pltpu.VMEM((L, HY_W), F32), pltpu.VMEM((halves, L, 128), F32),
            pltpu.VMEM((nb, m, HY_W), BF16), pltpu.VMEM((nb, m, HY_W), BF16), pltpu.VMEM((nb, m, HY_W), BF16),
        ],
        compiler_params=_cparams("parallel"),
        name="hyena",
    )(u, u, u, u, conv_w, conv_b[None, :], skip, *mats, tw, hr, hi, hx)


def _ret_kernel(*refs, L, Lc, has_init):
    if has_init:
        (q_ref, k_ref, v_ref, g_ref, kc_ref, vc_ref, cos_ref, sin_ref, rate_ref, rrow_ref, ones_ref,
         o_ref, q_scr, k_scr, sf_scr, sb_scr, r_scr) = refs
    else:
        (q_ref, k_ref, v_ref, g_ref, cos_ref, sin_ref, rate_ref, rrow_ref, ones_ref,
         o_ref, q_scr, k_scr, sf_scr, sb_scr, r_scr) = refs
    C = min(RET_CHUNK, L)
    nch = L // C
    W = RET_W
    quarter = RET_HEAD_DIM // 4
    block = ones_ref[...]
    lg = -jnp.exp(rate_ref[...])
    lg_f, lg_b = lg[0:1, :], lg[1:2, :]

    lane = lax.broadcasted_iota(jnp.int32, (L, W), 1)
    first_quarter = (lane % (2 * quarter)) < quarter

    def rope(a):
        swapped = jnp.where(first_quarter, pltpu.roll(a, W - quarter, 1), pltpu.roll(a, quarter, 1))
        return a * cos_ref[...] + swapped * sin_ref[...]

    if has_init:
        q_scr[...] = rope(q_ref[...].astype(F32)).astype(BF16)
        k_scr[...] = rope(k_ref[...].astype(F32) * (RET_HEAD_DIM ** -0.5)).astype(BF16)
    else:
        q_scr[...] = q_ref[...]
        k_scr[...] = (k_ref[...].astype(F32) * (RET_HEAD_DIM ** -0.5)).astype(BF16)

    def decays(n_rows):
        pos = lax.broadcasted_iota(jnp.int32, (n_rows, W), 0).astype(F32)
        return (jnp.exp(lg_f * (n_rows - 1.0 - pos)),
                jnp.exp(lg_b * pos))

    def chunk_states(k_bf, v_bf, zf, zb):
        kf = (k_bf.astype(F32) * zf).astype(BF16)
        kb = (k_bf.astype(F32) * zb).astype(BF16)
        return _dot_tn(kf, v_bf) * block, _dot_tn(kb, v_bf) * block

    zeta_f, zeta_b = decays(C)
    if has_init:
        zc_f, zc_b = decays(Lc)
        kc = (kc_ref[...].astype(F32) * (RET_HEAD_DIM ** -0.5)).astype(BF16)
        s0_f, s0_b = chunk_states(kc, vc_ref[...], zc_f, zc_b)
    else:
        s0_f = jnp.zeros((W, W), F32)
        s0_b = jnp.zeros((W, W), F32)

    for n in range(nch):
        kv_f, kv_b = chunk_states(k_scr[n * C:(n + 1) * C, :], v_ref[n * C:(n + 1) * C, :], zeta_f, zeta_b)
        sf_scr[n] = kv_f
        sb_scr[n] = kv_b
    dec_f = jnp.exp(lg_f * float(C))
    dec_b = jnp.exp(lg_b * float(C))
    state = s0_f
    for n in range(nch):
        kv = sf_scr[n]
        sf_scr[n] = state
        state = dec_f * state + kv
    state = s0_b
    for n in range(nch - 1, -1, -1):
        kv = sb_scr[n]
        sb_scr[n] = state
        state = dec_b * state + kv

    posc = lax.broadcasted_iota(jnp.int32, (C, W), 0).astype(F32)
    xi_f = jnp.exp(lg_f * (posc + 1.0))
    xi_b = jnp.exp(lg_b * (float(C) - posc))
    diff = (lax.broadcasted_iota(jnp.int32, (C, C), 0) - lax.broadcasted_iota(jnp.int32, (C, C), 1)).astype(F32)
    lane_c = lax.broadcasted_iota(jnp.int32, (1, W), 1)
    dmask = []
    for h in range(RET_HEADS):
        rf = -jnp.exp(rrow_ref[h:h + 1, 0:C])
        rb = -jnp.exp(rrow_ref[RET_HEADS + h:RET_HEADS + h + 1, 0:C])
        dmask.append(jnp.where(diff >= 0, jnp.exp(rf * jnp.maximum(diff, 0.0)), 0.0)
                     + jnp.where(diff <= 0, jnp.exp(rb * jnp.maximum(-diff, 0.0)), 0.0))

    for n in range(nch):
        qn = q_scr[n * C:(n + 1) * C, :]
        kn = k_scr[n * C:(n + 1) * C, :]
        vn = v_ref[n * C:(n + 1) * C, :]
        qf = qn.astype(F32)
        lhs = [(qf * xi_f).astype(BF16), (qf * xi_b).astype(BF16)]
        rhs = [sf_scr[n].astype(BF16), sb_scr[n].astype(BF16)]
        zero = jnp.zeros_like(qn)
        for h in range(RET_HEADS):
            sel = (lane_c // RET_HEAD_DIM) == h
            s = _dot_nt(jnp.where(sel, qn, zero), kn) * dmask[h]
            lhs.append(s.astype(BF16))
            rhs.append(jnp.where(sel, vn, zero))
        r_scr[n * C:(n + 1) * C, :] = _dot(jnp.concatenate(lhs, axis=1), jnp.concatenate(rhs, axis=0))

    r = r_scr[...]
    ss = _dot((r * r).astype(BF16), block.astype(BF16))
    rn = r * lax.rsqrt(ss * (1.0 / RET_HEAD_DIM) + EPS)
    o_ref[...] = (rn * _silu(g_ref[...].astype(F32))).astype(o_ref.dtype)


def _retention(u, uc, ret_log_rate, B, L, Lc, has_init):
    C = min(RET_CHUNK, L)
    nch = L // C
    col = lambda j: pl.BlockSpec((L, CB), lambda b: (b, j))
    ccol = lambda j: pl.BlockSpec((Lc, CB), lambda b: (b, j))
    cos, sin = _rope_tables(L)
    rate_lane = jnp.repeat(ret_log_rate, RET_HEAD_DIM, axis=1)
    rate_row = jnp.broadcast_to(ret_log_rate.reshape(2 * RET_HEADS, 1), (2 * RET_HEADS, RET_W))
    ones = jnp.asarray(_head_block_ones(RET_W))
    in_specs = [col(CB_RE_Q), col(CB_RE_K), col(CB_RE_V), col(CB_RE_G)]
    args = [u, u, u, u]
    if has_init:
        in_specs += [ccol(CB_RE_K), ccol(CB_RE_V)]
        args += [uc, uc]
    in_specs += [
        _resident((L, RET_W), lambda b: (0, 0)),
        _resident((L, RET_W), lambda b: (0, 0)),
        _resident((2, RET_W), lambda b: (0, 0)),
        _resident((2 * RET_HEADS, RET_W), lambda b: (0, 0)),
        _resident((RET_W, RET_W), lambda b: (0, 0)),
    ]
    args += [jnp.asarray(cos), jnp.asarray(sin), rate_lane, rate_row, ones]
    return pl.pallas_call(
        functools.partial(_ret_kernel, L=L, Lc=Lc, has_init=has_init),
        grid=(B,),
        in_specs=in_specs,
        out_specs=pl.BlockSpec((L, RET_W), lambda b: (b, 0)),
        out_shape=jax.ShapeDtypeStruct((B * L, RET_W), BF16),
        scratch_shapes=[
            pltpu.VMEM((L, RET_W), BF16), pltpu.VMEM((L, RET_W), BF16),
            pltpu.VMEM((nch, RET_W, RET_W), F32), pltpu.VMEM((nch, RET_W, RET_W), F32),
            pltpu.VMEM((L, RET_W), F32),
        ],
        compiler_params=_cparams("parallel"),
        name="retention",
    )(*args)


def kernel(x, c, ctx, c_ctx, norm_w, ada_w, ada_b, w_in, w_out, na_q_gain, na_k_gain, na_rpb, hy_conv_w, hy_conv_b,
           hy_w1, hy_b1, hy_w2, hy_b2, hy_w3, hy_sin_freq, hy_skip, ret_log_rate):
    B, L, D = x.shape
    Lc = ctx.shape[1]
    assert D == D_MODEL and L % (GRID_W * NA_G * NA_GPS) == 0 and L // GRID_W >= NA_WIN
    assert NA_G == NA_KT // GRID_W and (L // GRID_W - NA_WIN) % NA_G == 0 and Lc == NA_KT

    n_cond = 16
    cc = jnp.concatenate([c, c_ctx[None, :], jnp.zeros((n_cond - B - 1, D), F32)], axis=0)
    mods = _modulation(cc, ada_w, ada_b)

    dft_x = _hyena_consts(L)
    dft_c = _hyena_consts(Lc)

    tm_x = 512
    xf = x.reshape(B * L, D)
    cf = ctx.reshape(B * Lc, D)
    x_mod = lambda i: i // (L // tm_x)
    c_mod = lambda i: B

    w_in_bf_all = w_in.astype(BF16)
    w_out_bf_all = w_out.astype(BF16)

    def layer_params(i):
        return mods[i].reshape(n_cond, 1, 3 * D), norm_w[i][None, :], (w_in_bf_all, i)

    mod, nw, w_in_bf = layer_params(0)
    u = _projection(xf, x_mod, tm_x, in_args=(mod, nw, w_in_bf))
    uc = _projection(cf, c_mod, Lc, in_args=(mod, nw, w_in_bf))
    for i in range(DEPTH):
        w_out_bf = (w_out_bf_all, i)
        filt = (hy_w1[i], hy_b1[i], hy_w2[i], hy_b2[i], hy_w3[i], hy_sin_freq[i])
        last = i == DEPTH - 1
        nxt = None if last else layer_params(i + 1)

        za = _neighborhood_attention(u, uc, na_q_gain[i], na_k_gain[i], na_rpb[i], B, L, Lc)
        hr, hi, hm = _hyena_filter(L, *filt, dft_x)
        zy = _hyena(u, hy_conv_w[i], hy_conv_b[i], hy_skip[i], dft_x, hr, hi, hm, B, L)
        zr = _retention(u, uc, ret_log_rate[i], B, L, Lc, True)
        if not last:
            zac = _context_attention(uc, na_q_gain[i], na_k_gain[i], B, Lc)
            hrc, hic, hmc = _hyena_filter(Lc, *filt, dft_c)
            zyc = _hyena(uc, hy_conv_w[i], hy_conv_b[i], hy_skip[i], dft_c, hrc, hic, hmc, B, Lc)
            zrc = _retention(uc, uc, ret_log_rate[i], B, Lc, Lc, False)
            cf, uc = _projection(cf, c_mod, Lc, out_args=(zac, zyc, zrc, mod, w_out_bf), in_args=nxt)
            xf, u = _projection(xf, x_mod, tm_x, out_args=(za, zy, zr, mod, w_out_bf), in_args=nxt)
            mod = nxt[0]
        else:
            tm_last = 2 * tm_x
            xf = _projection(xf, lambda i: i // (L // tm_last), tm_last, out_args=(za, zy, zr, mod, w_out_bf))

    return xf.reshape(B, L, D)
```

```python
import functools
import math

import numpy as np
import jax
import jax.numpy as jnp
from jax import lax
from jax.experimental import pallas as pl
from jax.experimental.pallas import tpu as pltpu

F32 = jnp.float32
BF16 = jnp.bfloat16

D_MODEL = 1024
DEPTH = 2
GRID_W = 64
NA_HEADS = 8
NA_HEAD_DIM = 64
NA_W = NA_HEADS * NA_HEAD_DIM
NA_KH = 8
NA_KW = 16
HY_W = 256
HY_BANDS = 8
HY_EMB = 1 + 2 * HY_BANDS
HY_EMB_PAD = 32
HY_FFN = 64
HY_FAST_DECAY = 0.3
HY_SLOW_DECAY = 1.5
HY_TARGET = 1e-2
RET_HEADS = 4
RET_HEAD_DIM = 64
RET_W = RET_HEADS * RET_HEAD_DIM
ROPE_BASE = 10000.0
EPS = 1e-6
NEG_INF = -1e30
LOG2E = 1.4426950408889634
IN_W = 4 * NA_W + 4 * HY_W + 4 * RET_W
MIX_W = NA_W + HY_W + RET_W

CB = 256
CB_HY_V, CB_HY_X1, CB_HY_X2, CB_HY_G = 8, 9, 10, 11
CB_RE_Q, CB_RE_K, CB_RE_V, CB_RE_G = 12, 13, 14, 15

NA_G = 4
NA_WIN = NA_G + NA_KH
NA_GPS = 4
NA_KT = 256
NA_DR_MIN = -4
NA_DR_NUM = 23
RET_CHUNK = 256
HY_ROWS = 512
HY_BASE = 512
VMEM_LIMIT = 56 * 1024 * 1024


def _cparams(*sem):
    return pltpu.CompilerParams(dimension_semantics=sem, vmem_limit_bytes=VMEM_LIMIT)


def _resident(shape, index_map):
    return pl.BlockSpec(shape, index_map, pipeline_mode=pl.Buffered(1))


def _silu(x):
    return x * (1.0 / (1.0 + jnp.exp(-x)))


def _dot(a, b):
    return jnp.dot(a, b, preferred_element_type=F32)


def _dot_nt(a, b):
    return lax.dot_general(a, b, (((1,), (1,)), ((), ())), preferred_element_type=F32)


def _dot_tn(a, b):
    return lax.dot_general(a, b, (((0,), (0,)), ((), ())), preferred_element_type=F32)


@functools.lru_cache(maxsize=None)
def _bin_map(n, depth):
    if depth == 0:
        return [np.arange(n)], [n]
    hb, he = _bin_map(n // 2, depth - 1)
    blocks = [b for b in hb] + [n - b for b in hb]
    extras = list(he) + [n - e for e in he if e < n // 2]
    return blocks, extras


@functools.lru_cache(maxsize=None)
def _hy_plan(L):
    depth = 0
    while L >> depth > HY_BASE:
        depth += 1
    m = L >> depth
    k = np.arange(m, dtype=np.int64)[:, None] * np.arange(m, dtype=np.int64)[None, :]
    ang = (k % (2 * m)).astype(np.float64) * (math.pi / m)
    base = [np.cos(ang), np.sin(ang)]
    if depth:
        odd = (np.arange(m, dtype=np.int64)[:, None] * (2 * np.arange(m, dtype=np.int64)[None, :] + 1)) % (4 * m)
        odd = odd.astype(np.float64) * (math.pi / (2 * m))
        base += [np.cos(odd), np.sin(odd), np.cos(odd).T, np.sin(odd).T]
    base = tuple(np.ascontiguousarray(a).astype(np.float32) for a in base)
    tw_blocks, tw_extras = [], []
    for level in range(1, depth + 1):
        n = m << level
        hb, he = _bin_map(n // 2, level - 1)
        tw_blocks.append([(np.cos(math.pi * b / n), np.sin(math.pi * b / n)) for b in hb])
        tw_extras.append([(math.cos(math.pi * e / n), math.sin(math.pi * e / n), e == n // 2) for e in he])
    blocks, extras = _bin_map(L, depth)
    inv_n = 1.0 / (2 * L)
    seen = set()

    def weight(b):
        w = 0.0 if b in seen else (inv_n if b in (0, L) else 2.0 * inv_n)
        seen.add(b)
        return w

    w_blocks = [np.array([weight(int(b)) for b in blk]) for blk in blocks]
    w_extras = [weight(int(e)) for e in extras]
    tables = [np.broadcast_to(t[:, None], (m, HY_W)) for lvl in tw_blocks for pair in lvl for t in pair]
    tw_table = np.stack(tables).astype(np.float32) if tables else np.zeros((1, 8, HY_W), np.float32)
    w_table = np.stack([np.broadcast_to(w[:, None], (m, HY_W)) for w in w_blocks]).astype(np.float32)
    return dict(depth=depth, m=m, base=base, tw_extras=tw_extras, n_blocks=len(blocks), n_extras=len(extras),
                w_extras=w_extras, tw_table=tw_table, w_table=w_table,
                tw_index=[[2 * (sum(len(l) for l in tw_blocks[:lv]) + i) for i in range(len(tw_blocks[lv]))]
                          for lv in range(depth)])


@functools.lru_cache(maxsize=None)
def _filter_features(L):
    t = np.linspace(0.0, 1.0, L)[:, None]
    omega = 2.0 * math.pi * np.arange(L)[:, None] / L
    bands = np.linspace(1e-4, HY_BANDS - 1, HY_BANDS)[None, :]
    z = np.concatenate([t, np.cos(bands * omega), -np.sin(bands * omega)], axis=-1)
    z = np.pad(z, ((0, 0), (0, HY_EMB_PAD - HY_EMB)))
    deltas = np.abs(np.linspace(math.log(HY_TARGET) / HY_SLOW_DECAY, math.log(HY_TARGET) / HY_FAST_DECAY, HY_W))
    window = np.exp(-t * deltas[None, :])
    return z.astype(np.float32), window.astype(np.float32)


@functools.lru_cache(maxsize=None)
def _rope_tables(L):
    half = RET_HEAD_DIM // 2
    quarter = half // 2
    t = np.arange(L)
    pos = np.stack([t // GRID_W, t % GRID_W], axis=0).astype(np.float64)
    freqs = ROPE_BASE ** (-np.arange(quarter, dtype=np.float64) / quarter)
    cos = np.zeros((L, RET_HEAD_DIM))
    sin = np.zeros((L, RET_HEAD_DIM))
    for a in range(2):
        ang = pos[a][:, None] * freqs[None, :]
        base = a * half
        cos[:, base:base + quarter] = np.cos(ang)
        cos[:, base + quarter:base + half] = np.cos(ang)
        sin[:, base:base + quarter] = -np.sin(ang)
        sin[:, base + quarter:base + half] = np.sin(ang)
    return (np.tile(cos, (1, RET_HEADS)).astype(np.float32), np.tile(sin, (1, RET_HEADS)).astype(np.float32))


@functools.lru_cache(maxsize=None)
def _head_block_ones(width):
    i = np.arange(width) // NA_HEAD_DIM
    return (i[:, None] == i[None, :]).astype(np.float32)


def _mod_kernel(c_ref, w_ref, b_ref, o_ref):
    s = _silu(c_ref[...])
    w = w_ref[...]
    s_hi = s.astype(BF16)
    w_hi = w.astype(BF16)
    s_lo = (s - s_hi.astype(F32)).astype(BF16)
    w_lo = (w - w_hi.astype(F32)).astype(BF16)
    o_ref[...] = _dot(s_hi, w_hi) + (_dot(s_hi, w_lo) + _dot(s_lo, w_hi)) + b_ref[...]


def _modulation(cc, ada_w, ada_b):
    R = cc.shape[0]
    tn = 1024
    return pl.pallas_call(
        _mod_kernel,
        grid=(DEPTH, 3 * D_MODEL // tn),
        in_specs=[
            pl.BlockSpec((R, D_MODEL), lambda i, j: (0, 0)),
            pl.BlockSpec((None, D_MODEL, tn), lambda i, j: (i, 0, j)),
            pl.BlockSpec((None, 1, tn), lambda i, j: (i, 0, j)),
        ],
        out_specs=pl.BlockSpec((None, R, tn), lambda i, j: (i, 0, j)),
        out_shape=jax.ShapeDtypeStruct((DEPTH, R, 3 * D_MODEL), F32),
        compiler_params=_cparams("parallel", "parallel"),
        name="modulation",
    )(cc, ada_w, ada_b.reshape(DEPTH, 1, 3 * D_MODEL))


def _proj_kernel(*refs, with_out, with_in):
    refs = list(refs)
    if with_out:
        za_ref, zy_ref, zr_ref, x_ref, mod_o_ref, w_out_ref = refs[:6]
        del refs[:6]
    else:
        x_ref = refs.pop(0)
    if with_in:
        mod_i_ref, nw_ref, w_in_ref = refs[:3]
        del refs[:3]
    x = x_ref[...]
    if with_out:
        acc = _dot(za_ref[...], w_out_ref[0:NA_W, :])
        acc += _dot(zy_ref[...], w_out_ref[NA_W:NA_W + HY_W, :])
        acc += _dot(zr_ref[...], w_out_ref[NA_W + HY_W:MIX_W, :])
        x = x + mod_o_ref[:, 2 * D_MODEL:3 * D_MODEL] * acc
        refs.pop(0)[...] = x
    if with_in:
        u_ref = refs.pop(0)
        xn = x * lax.rsqrt(jnp.mean(x * x, axis=-1, keepdims=True) + EPS)
        shift = mod_i_ref[:, 0:D_MODEL]
        scale = mod_i_ref[:, D_MODEL:2 * D_MODEL]
        h = (xn * nw_ref[...] * (1.0 + scale) + shift).astype(BF16)
        tn = 1024
        for j in range(IN_W // tn):
            u_ref[:, j * tn:(j + 1) * tn] = _dot(h, w_in_ref[:, j * tn:(j + 1) * tn]).astype(u_ref.dtype)


def _projection(xf, mod_index, tm, out_args=None, in_args=None):
    R = xf.shape[0]
    row = lambda w: pl.BlockSpec((tm, w), lambda i: (i, 0))
    mod_spec = pl.BlockSpec((None, 1, 3 * D_MODEL), lambda i: (mod_index(i), 0, 0))
    args, in_specs, out_specs, out_shape = [], [], [], []
    if out_args is not None:
        za, zy, zr, mod_o, (w_out_bf, layer_o) = out_args
        args += [za, zy, zr, xf, mod_o, w_out_bf]
        in_specs += [row(NA_W), row(HY_W), row(RET_W), row(D_MODEL), mod_spec,
                     _resident((None, MIX_W, D_MODEL), lambda i: (layer_o, 0, 0))]
        out_specs.append(row(D_MODEL))
        out_shape.append(jax.ShapeDtypeStruct((R, D_MODEL), F32))
    else:
        args.append(xf)
        in_specs.append(row(D_MODEL))
    if in_args is not None:
        mod_i, norm_w, (w_in_bf, layer_i) = in_args
        args += [mod_i, norm_w, w_in_bf]
        in_specs += [mod_spec, _resident((1, D_MODEL), lambda i: (0, 0)),
                     _resident((None, D_MODEL, IN_W), lambda i: (layer_i, 0, 0))]
        out_specs.append(row(IN_W))
        out_shape.append(jax.ShapeDtypeStruct((R, IN_W), BF16))
    outs = pl.pallas_call(
        functools.partial(_proj_kernel, with_out=out_args is not None, with_in=in_args is not None),
        grid=(R // tm,),
        in_specs=in_specs,
        out_specs=out_specs,
        out_shape=out_shape,
        compiler_params=_cparams("parallel"),
        name="projection",
    )(*args)
    return outs if len(outs) > 1 else outs[0]


def _head_rms(x, ones_bf, gain):
    ss = _dot((x * x).astype(BF16), ones_bf)
    return x * lax.rsqrt(ss * (1.0 / NA_HEAD_DIM) + EPS) * gain


def _pair_masks():
    lane = lax.broadcasted_iota(jnp.int32, (1, 2 * NA_HEAD_DIM), 1)
    return lane < NA_HEAD_DIM


def _attend_heads(q, key_parts, val_parts, bias_fn):
    first = _pair_masks()
    outs = []
    for pair in range(NA_HEADS // 2):
        lo = pair * 2 * NA_HEAD_DIM
        hi = lo + 2 * NA_HEAD_DIM
        qp = q[:, lo:hi]
        o_pair = None
        for sub in range(2):
            h = 2 * pair + sub
            sel = first if sub == 0 else jnp.logical_not(first)
            qm = jnp.where(sel, qp, 0.0).astype(BF16)
            scores = []
            for part, kp in enumerate(key_parts):
                s = _dot_nt(qm, kp[:, lo:hi])
                b = bias_fn(h, part)
                if b is not None:
                    s = s + b
                scores.append(s)
            m = scores[0].max(axis=-1, keepdims=True)
            for s in scores[1:]:
                m = jnp.maximum(m, s.max(axis=-1, keepdims=True))
            denom = None
            acc = None
            for s, vp in zip(scores, val_parts):
                p = jnp.exp(s - m)
                ps = p.sum(axis=-1, keepdims=True)
                denom = ps if denom is None else denom + ps
                pv = _dot(p.astype(BF16), vp[:, lo:hi])
                acc = pv if acc is None else acc + pv
            o_h = acc * (1.0 / denom)
            o_pair = o_h if o_pair is None else jnp.where(first, o_pair, o_h)
        outs.append(o_pair)
    return jnp.concatenate(outs, axis=-1)


def _na_kernel(q_ref, k_ref, v_ref, g_ref, kc_ref, vc_ref, qg_ref, kg_ref, ones_ref, rpb_ref, o_ref,
               kn_scr, kcn_scr, vat_scr, vbt_scr, vcat_scr, vcbt_scr, bias_scr, mask_scr, s0_scr, s1_scr,
               p0_scr, p1_scr, qm_scr, ot_scr, *, rows):
    batch = pl.program_id(0)
    grp = pl.program_id(1)
    ones_bf = ones_ref[...]
    tq = NA_G * GRID_W
    nwin = NA_WIN * GRID_W
    Lc = kc_ref.shape[0]
    pair_w = 2 * NA_HEAD_DIM
    kt = NA_KT
    first = lax.broadcasted_iota(jnp.int32, (1, pair_w), 1) < NA_HEAD_DIM

    @pl.when((batch == 0) & (grp == 0))
    def _():
        ck = lax.broadcasted_iota(jnp.int32, (GRID_W, pair_w), 0)
        lane = lax.broadcasted_iota(jnp.int32, (GRID_W, pair_w), 1)
        cq = lane % GRID_W
        col_start = jnp.clip(cq - NA_KW // 2, 0, GRID_W - NA_KW)
        col_ok = (ck >= col_start) & (ck < col_start + NA_KW)
        left = lane < GRID_W

        def body(i, carry):
            for h in range(NA_HEADS):
                v0 = jnp.broadcast_to(rpb_ref[i + 1, h:h + 1, :], (GRID_W, pair_w))
                v1 = jnp.broadcast_to(rpb_ref[i, h:h + 1, :], (GRID_W, pair_w))
                t0 = pltpu.roll(v0, pair_w - (NA_KW - 1), 1, stride=1, stride_axis=0)
                t1 = pltpu.roll(v1, GRID_W - (NA_KW - 1), 1, stride=1, stride_axis=0)
                bias_scr[i, h] = jnp.where(col_ok, jnp.where(left, t0, t1), NEG_INF)
            return carry

        lax.fori_loop(0, NA_DR_NUM, body, 0)

    @pl.when(grp == 0)
    def _():
        kgain = kg_ref[...]
        own_a = (lax.broadcasted_iota(jnp.int32, (NA_W, 1), 0) % pair_w) < NA_HEAD_DIM
        for src, dst, vat, vbt, vsrc in ((k_ref, kn_scr, vat_scr, vbt_scr, v_ref),
                                         (kc_ref, kcn_scr, vcat_scr, vcbt_scr, vc_ref)):
            for i in range(src.shape[0] // kt):
                sl = slice(i * kt, (i + 1) * kt)
                dst[sl, :] = _head_rms(src[sl, :].astype(F32), ones_bf, kgain).astype(BF16)
                vt = vsrc[sl, :].astype(F32).T
                vat[i] = jnp.where(own_a, vt, 1.0).astype(BF16)
                vbt[i] = jnp.where(own_a, 1.0, vt).astype(BF16)

    n_kt = (nwin + Lc) // kt
    rows_per_tile = kt // GRID_W
    n_ch = (nwin + Lc) // GRID_W
    n_items = NA_GPS * NA_HEADS

    s_bufs = (s0_scr, s1_scr)
    p_bufs = (p0_scr, p1_scr)
    row_zero = jnp.minimum(grp, 0)

    def rows_of(start, size):
        return pl.ds(pl.multiple_of(row_zero + start, GRID_W), size)

    def head_lanes(h):
        return slice((h // 2) * pair_w, (h // 2 + 1) * pair_w)

    key0, tile0, dr_base = [], [], []
    for gi in range(NA_GPS):
        r0 = (grp * NA_GPS + gi) * NA_G
        ws = jnp.clip(r0 - NA_KH // 2, 0, rows - NA_WIN)
        key0.append(pl.multiple_of(ws * GRID_W, kt))
        tile0.append(ws // rows_per_tile)
        dr_base.append(ws - r0 + (NA_KH - 1) - NA_DR_MIN)
        w_id = lax.broadcasted_iota(jnp.int32, (16, tq), 0)
        g_id = lax.broadcasted_iota(jnp.int32, (16, tq), 1) // GRID_W
        lo = jnp.clip(r0 + g_id - NA_KH // 2, 0, rows - NA_KH) - ws
        mask_scr[gi] = jnp.where((w_id >= lo) & (w_id < lo + NA_KH), 0.0, NEG_INF).astype(F32)
        q = _head_rms(q_ref[gi * tq:(gi + 1) * tq, :].astype(F32), ones_bf,
                      qg_ref[...] * (NA_HEAD_DIM ** -0.5 * LOG2E))
        for h in range(NA_HEADS):
            sel = first if h % 2 == 0 else jnp.logical_not(first)
            qm_scr[gi * NA_HEADS + h, rows_of(0, tq), :] = jnp.where(sel, q[:, head_lanes(h)], 0.0).astype(BF16)

    col_max = {}

    def qk_piece(n, j):
        gi, h = divmod(n, NA_HEADS)
        last = j == n_kt - 1
        keys = kcn_scr[:, head_lanes(h)] if last else kn_scr[pl.ds(key0[gi] + j * kt, kt), head_lanes(h)]
        sv = _dot_nt(keys, qm_scr[n, rows_of(0, tq), :])
        for dw in range(rows_per_tile):
            sw = sv[dw * GRID_W:(dw + 1) * GRID_W, :]
            if not last:
                w = j * rows_per_tile + dw
                bias = jnp.concatenate([bias_scr[dr_base[gi] + (w - g), h] for g in range(0, NA_G, 2)], axis=-1)
                sw = sw + bias + mask_scr[gi, w:w + 1, :]
            s_bufs[n % 2][rows_of(j * kt + dw * GRID_W, GRID_W), :] = sw
            part = sw.reshape(GRID_W // 8, 8, tq).max(axis=0)
            col_max[n] = part if (j == 0 and dw == 0) else jnp.maximum(col_max[n], part)

    def exp_chunk(n, c):
        if c == 0:
            col_max[n] = col_max[n].max(axis=0, keepdims=True)
        rows_c = rows_of(c * GRID_W, GRID_W)
        p_bufs[n % 2][rows_c, :] = jnp.exp2(s_bufs[n % 2][rows_c, :] - col_max[n]).astype(BF16)
        if c == n_ch - 1:
            del col_max[n]

    def pv(n):
        gi, h = divmod(n, NA_HEADS)
        vt, vct = (vat_scr, vcat_scr) if h % 2 == 0 else (vbt_scr, vcbt_scr)
        ot = _dot(vct[0, head_lanes(h), :], p_bufs[n % 2][rows_of(nwin, Lc), :])
        for j in range(n_kt - 1):
            ot += _dot(vt[tile0[gi] + j, head_lanes(h), :], p_bufs[n % 2][rows_of(j * kt, kt), :])
        own, den = (ot[0:NA_HEAD_DIM], ot[NA_HEAD_DIM:]) if h % 2 == 0 else (ot[NA_HEAD_DIM:], ot[0:NA_HEAD_DIM])
        ot_scr[rows_of(n * NA_HEAD_DIM, NA_HEAD_DIM), :] = own * (1.0 / den)
        if h % 2 == 1:
            o_pair = ot_scr[rows_of((n - 1) * NA_HEAD_DIM, pair_w), :].T
            q_rows = slice(gi * tq, (gi + 1) * tq)
            gate = _silu(g_ref[q_rows, head_lanes(h)].astype(F32))
            o_ref[q_rows, head_lanes(h)] = (o_pair * gate).astype(o_ref.dtype)

    for t in range(n_items + 2):
        for i in range(n_ch):
            if 0 <= t - 1 < n_items:
                exp_chunk(t - 1, i)
            if i % 4 == 0 and t < n_items and i // 4 < n_kt:
                qk_piece(t, i // 4)
            if i == 2 and 0 <= t - 2:
                pv(t - 2)


def _neighborhood_attention(u, uc, q_gain, k_gain, rpb, B, L, Lc):
    rows = L // GRID_W
    ngrp = rows // (NA_G * NA_GPS)
    tq = NA_G * GRID_W
    tstep = tq * NA_GPS
    ones = jnp.asarray(_head_block_ones(NA_W)).astype(BF16)
    qg = jnp.tile(q_gain, NA_HEADS)[None, :]
    kg = jnp.tile(k_gain, NA_HEADS)[None, :]
    n_dr = 2 * NA_KH - 1
    rpb_t = jnp.transpose(rpb.astype(F32)[:, :, ::-1] * LOG2E, (1, 0, 2))
    lo_pad = 1 - NA_DR_MIN
    rpb_t = jnp.pad(rpb_t, ((lo_pad, NA_DR_NUM + 1 - lo_pad - n_dr), (0, 0), (0, 2 * GRID_W - (2 * NA_KW - 1))))
    return pl.pallas_call(
        functools.partial(_na_kernel, rows=rows),
        grid=(B, ngrp),
        in_specs=[
            pl.BlockSpec((tstep, NA_W), lambda b, g: (b * ngrp + g, 0)),
            pl.BlockSpec((L, NA_W), lambda b, g: (b, 1)),
            pl.BlockSpec((L, NA_W), lambda b, g: (b, 2)),
            pl.BlockSpec((tstep, NA_W), lambda b, g: (b * ngrp + g, 3)),
            pl.BlockSpec((Lc, NA_W), lambda b, g: (b, 1)),
            pl.BlockSpec((Lc, NA_W), lambda b, g: (b, 2)),
            _resident((1, NA_W), lambda b, g: (0, 0)),
            _resident((1, NA_W), lambda b, g: (0, 0)),
            _resident((NA_W, NA_W), lambda b, g: (0, 0)),
            _resident((NA_DR_NUM + 1, NA_HEADS, 2 * GRID_W), lambda b, g: (0, 0, 0)),
        ],
        out_specs=pl.BlockSpec((tstep, NA_W), lambda b, g: (b * ngrp + g, 0)),
        out_shape=jax.ShapeDtypeStruct((B * L, NA_W), BF16),
        scratch_shapes=[
            pltpu.VMEM((L, NA_W), BF16), pltpu.VMEM((Lc, NA_W), BF16),
            pltpu.VMEM((L // NA_KT, NA_W, NA_KT), BF16), pltpu.VMEM((L // NA_KT, NA_W, NA_KT), BF16),
            pltpu.VMEM((Lc // NA_KT, NA_W, NA_KT), BF16), pltpu.VMEM((Lc // NA_KT, NA_W, NA_KT), BF16),
            pltpu.VMEM((NA_DR_NUM, NA_HEADS, GRID_W, 2 * GRID_W), F32),
            pltpu.VMEM((NA_GPS, 16, tq), F32),
            pltpu.VMEM((NA_WIN * GRID_W + Lc, tq), F32), pltpu.VMEM((NA_WIN * GRID_W + Lc, tq), F32),
            pltpu.VMEM((NA_WIN * GRID_W + Lc, tq), BF16), pltpu.VMEM((NA_WIN * GRID_W + Lc, tq), BF16),
            pltpu.VMEM((NA_GPS * NA_HEADS, tq, 2 * NA_HEAD_DIM), BF16), pltpu.VMEM((NA_GPS * NA_W, tq), F32),
        ],
        compiler_params=_cparams("arbitrary", "arbitrary"),
        name="neighborhood_attention",
    )(u, u, u, u, uc, uc, qg, kg, ones, rpb_t)


def _ctx_attn_kernel(q_ref, k_ref, v_ref, g_ref, qg_ref, kg_ref, ones_ref, o_ref):
    ones_bf = ones_ref[...]
    q = _head_rms(q_ref[...].astype(F32), ones_bf, qg_ref[...] * (NA_HEAD_DIM ** -0.5))
    k = _head_rms(k_ref[...].astype(F32), ones_bf, kg_ref[...]).astype(BF16)
    o = _attend_heads(q, [k], [v_ref[...]], lambda h, part: None)
    o_ref[...] = (o * _silu(g_ref[...].astype(F32))).astype(o_ref.dtype)


def _context_attention(uc, q_gain, k_gain, B, Lc):
    ones = jnp.asarray(_head_block_ones(NA_W)).astype(BF16)
    qg = jnp.tile(q_gain, NA_HEADS)[None, :]
    kg = jnp.tile(k_gain, NA_HEADS)[None, :]
    return pl.pallas_call(
        _ctx_attn_kernel,
        grid=(B,),
        in_specs=[
            pl.BlockSpec((Lc, NA_W), lambda b: (b, 0)),
            pl.BlockSpec((Lc, NA_W), lambda b: (b, 1)),
            pl.BlockSpec((Lc, NA_W), lambda b: (b, 2)),
            pl.BlockSpec((Lc, NA_W), lambda b: (b, 3)),
            _resident((1, NA_W), lambda b: (0, 0)),
            _resident((1, NA_W), lambda b: (0, 0)),
            _resident((NA_W, NA_W), lambda b: (0, 0)),
        ],
        out_specs=pl.BlockSpec((Lc, NA_W), lambda b: (b, 0)),
        out_shape=jax.ShapeDtypeStruct((B * Lc, NA_W), BF16),
        compiler_params=_cparams("parallel"),
        name="context_attention",
    )(uc, uc, uc, uc, qg, kg, ones)


def _alt_sign(shape):
    row = lax.broadcasted_iota(jnp.int32, shape, 0)
    return (1 - 2 * (row & 1)).astype(F32)


def _lane_halves(x):
    return [x[:, h * 128:(h + 1) * 128] for h in range(HY_W // 128)]


class _Spec:
    def __init__(self, c, s):
        self.c, self.s = list(c), list(s)


def _combine(P, R, tw):
    lo_c, lo_s, mi_c, mi_s = [], [], [], []
    for (pc, ps, rc, rs, (ct, st, single)) in zip(P.c, P.s, R.c, R.s, tw):
        if single:
            lo_c.append(pc)
            lo_s.append(rc)
            continue
        qc, qs = (rc, rs) if ct is None else (ct * rc - st * rs, st * rc + ct * rs)
        lo_c.append(pc + qc)
        lo_s.append(ps + qs)
        mi_c.append(pc - qc)
        mi_s.append(qs - ps)
    return _Spec(lo_c + mi_c, lo_s + mi_s)


def _combine_t(Z, tw):
    n_lo = len(tw)
    lo_c, lo_s, mi_c, mi_s = Z.c[:n_lo], Z.s[:n_lo], Z.c[n_lo:], Z.s[n_lo:]
    pc, ps, rc, rs = [], [], [], []
    mi = 0
    for i, (ct, st, single) in enumerate(tw):
        if single:
            pc.append(lo_c[i])
            ps.append(jnp.zeros_like(lo_c[i]))
            rc.append(lo_s[i])
            rs.append(jnp.zeros_like(lo_c[i]))
            continue
        g_qc = lo_c[i] - mi_c[mi]
        g_qs = lo_s[i] + mi_s[mi]
        pc.append(lo_c[i] + mi_c[mi])
        ps.append(lo_s[i] - mi_s[mi])
        rc.append(g_qc if ct is None else ct * g_qc + st * g_qs)
        rs.append(g_qs if ct is None else ct * g_qs - st * g_qc)
        mi += 1
    return _Spec(pc, ps), _Spec(rc, rs)


def _forward(level, leaves, tws):
    if level == 0:
        return leaves[0]
    half = len(leaves) // 2
    return _combine(_forward(level - 1, leaves[:half], tws), _forward(level - 1, leaves[half:], tws), tws[level - 1])


def _backward(level, Z, tws):
    if level == 0:
        return [Z]
    P, R = _combine_t(Z, tws[level - 1])
    return _backward(level - 1, P, tws) + _backward(level - 1, R, tws)


def _leaf_samples(depth, off=0, stride=1):
    if depth == 0:
        return [(off, stride)]
    return _leaf_samples(depth - 1, off, 2 * stride) + _leaf_samples(depth - 1, off + stride, 2 * stride)


def _block_twiddles(plan, tw_ref, k0, k1):
    return [[(None, None, False) if lv == 0 else (tw_ref[i, k0:k1, :], tw_ref[i + 1, k0:k1, :], False)
             for i in plan["tw_index"][lv]] for lv in range(plan["depth"])]


def _leaf_matrices(plan, mats, i, transposed):
    if plan["depth"] == 0 or i % 2 == 0:
        return mats[0], mats[1]
    return (mats[4], mats[5]) if transposed else (mats[2], mats[3])


def _spectrum_of(plan, src_scr, xbf_scr, mats, tw_ref, consume):
    depth, m = plan["depth"], plan["m"]
    cm = min(HY_ROWS // 2, m)
    sign = _alt_sign((cm, HY_W))
    tops = []
    for i, (off, stride) in enumerate(_leaf_samples(depth)):
        top = jnp.zeros((1, HY_W), F32)
        for j0 in range(0, m, cm):
            x = jnp.concatenate([src_scr[h, pl.ds(off + stride * j0, cm, stride=stride), :]
                                 for h in range(HY_W // 128)], axis=1)
            xbf_scr[i, j0:j0 + cm, :] = x.astype(BF16)
            top += jnp.sum(x * sign, axis=0, keepdims=True)
        tops.append(top)
    for k0 in range(0, m, cm):
        k1 = k0 + cm
        base = []
        for i in range(len(tops)):
            c_ref, s_ref = _leaf_matrices(plan, mats, i, False)
            base.append(_Spec([_dot(c_ref[k0:k1, :], xbf_scr[i])], [_dot(s_ref[k0:k1, :], xbf_scr[i])]))
        consume(k0, k1, _forward(depth, base, _block_twiddles(plan, tw_ref, k0, k1)))
    zero = jnp.zeros((1, HY_W), F32)
    return _forward(depth, [_Spec([t], [zero]) for t in tops], plan["tw_extras"])


def _sequence_of(plan, gc_scr, gs_scr, tops, mats, dst_scr):
    depth, m = plan["depth"], plan["m"]
    cm = min(HY_ROWS // 2, m)
    sign = _alt_sign((cm, HY_W))
    for i, (off, stride) in enumerate(_leaf_samples(depth)):
        c_ref, s_ref = _leaf_matrices(plan, mats, i, True)
        for j0 in range(0, m, cm):
            y = _dot(c_ref[j0:j0 + cm, :], gc_scr[i]) + _dot(s_ref[j0:j0 + cm, :], gs_scr[i]) + sign * tops[i]
            for h in range(HY_W // 128):
                dst_scr[h, pl.ds(off + stride * j0, cm, stride=stride), :] = y[:, h * 128:(h + 1) * 128]


def _filter_kernel(z_ref, win_ref, w1_ref, b1_ref, w2_ref, b2_ref, w3_ref, sf_ref, *refs, L):
    plan = _hy_plan(L)
    n_mats = len(plan["base"])
    mats = refs[:n_mats]
    tw_ref, wt_ref, hr_ref, hi_ref, hx_ref, h_scr, a_scr, d_scr, xbf_scr = refs[n_mats:]
    hp = lax.Precision.HIGHEST
    dot = lambda a, b: jnp.dot(a, b, preferred_element_type=F32, precision=hp)

    @pl.when(pl.program_id(0) == 0)
    def _():
        h = jnp.sin(sf_ref[0:1, :] * (dot(z_ref[...], w1_ref[...]) + b1_ref[...]))
        h_scr[...] = jnp.sin(sf_ref[1:2, :] * (dot(h, w2_ref[...]) + b2_ref[...]))

    taps = dot(h_scr[...], w3_ref[...])
    win = win_ref[...]
    row = lax.broadcasted_iota(jnp.int32, (L, HY_W), 0)
    hf = jnp.concatenate([taps[:, 0:HY_W], taps[:, 2 * HY_W:3 * HY_W]], axis=0) * win
    hb = jnp.concatenate([taps[:, HY_W:2 * HY_W], taps[:, 3 * HY_W:4 * HY_W]], axis=0) * win
    hb = jnp.where(row == 0, 0.0, hb)
    norm = jnp.sum(jnp.abs(hf), axis=0, keepdims=True) + jnp.sum(jnp.abs(hb), axis=0, keepdims=True)
    inv = 1.0 / norm
    for h, (a_half, d_half) in enumerate(zip(_lane_halves((hf + hb) * inv), _lane_halves((hf - hb) * inv))):
        a_scr[h] = a_half
        d_scr[h] = d_half

    def store_real(k0, k1, spec):
        for b, blk in enumerate(spec.c):
            hr_ref[b, k0:k1, :] = blk * wt_ref[b, k0:k1, :]

    def store_imag(k0, k1, spec):
        for b, blk in enumerate(spec.s):
            hi_ref[b, k0:k1, :] = -blk * wt_ref[b, k0:k1, :]

    extra_a = _spectrum_of(plan, a_scr, xbf_scr, mats, tw_ref, store_real)
    extra_d = _spectrum_of(plan, d_scr, xbf_scr, mats, tw_ref, store_imag)
    for e, w in enumerate(plan["w_extras"]):
        hx_ref[0, e] = extra_a.c[e] * w
        hx_ref[1, e] = -extra_d.s[e] * w


def _block_diag2(w):
    z = jnp.zeros_like(w)
    return jnp.concatenate([jnp.concatenate([w, z], axis=1), jnp.concatenate([z, w], axis=1)], axis=0)


def _hyena_consts(L):
    plan = _hy_plan(L)
    mats = tuple(jnp.asarray(a).astype(BF16) for a in plan["base"])
    return mats, jnp.asarray(plan["tw_table"]), jnp.asarray(plan["w_table"])


def _hyena_filter(L, w1, b1, w2, b2, w3, sin_freq, consts):
    plan = _hy_plan(L)
    m, nb, ne = plan["m"], plan["n_blocks"], plan["n_extras"]
    mats, tw, wt = consts
    z, window = _filter_features(L)
    z2 = np.concatenate([z[:L // 2], z[L // 2:]], axis=1)
    w1d = _block_diag2(jnp.pad(w1, ((0, HY_EMB_PAD - HY_EMB), (0, 0))))
    w2d = _block_diag2(w2)
    w3d = jnp.stack([_block_diag2(w3[:, o * 2 * HY_W:(o + 1) * 2 * HY_W]) for o in range(2)])
    pair = lambda v: jnp.tile(v, (1, 2))
    full = lambda *shape: _resident(shape, lambda o: (0,) * len(shape))
    spectrum = pl.BlockSpec((None, nb, m, HY_W), lambda o: (o, 0, 0, 0))
    return pl.pallas_call(
        functools.partial(_filter_kernel, L=L),
        grid=(2,),
        in_specs=[
            full(L // 2, 2 * HY_EMB_PAD), full(L, HY_W), full(2 * HY_EMB_PAD, 2 * HY_FFN), full(1, 2 * HY_FFN),
            full(2 * HY_FFN, 2 * HY_FFN), full(1, 2 * HY_FFN),
            pl.BlockSpec((None, 2 * HY_FFN, 4 * HY_W), lambda o: (o, 0, 0)),
            full(2, 2 * HY_FFN), *[full(m, m) for _ in mats], full(*tw.shape), full(*wt.shape),
        ],
        out_specs=[spectrum, spectrum, pl.BlockSpec((None, 2, ne, 1, HY_W), lambda o: (o, 0, 0, 0, 0))],
        out_shape=[jax.ShapeDtypeStruct((2, nb, m, HY_W), F32), jax.ShapeDtypeStruct((2, nb, m, HY_W), F32),
                   jax.ShapeDtypeStruct((2, 2, ne, 1, HY_W), F32)],
        scratch_shapes=[pltpu.VMEM((L // 2, 2 * HY_FFN), F32), pltpu.VMEM((HY_W // 128, L, 128), F32),
                        pltpu.VMEM((HY_W // 128, L, 128), F32), pltpu.VMEM((nb, m, HY_W), BF16)],
        compiler_params=_cparams("arbitrary"),
        name="hyena_filter",
    )(jnp.asarray(z2), jnp.asarray(window), w1d, pair(b1[None, :]), w2d, pair(b2[None, :]), w3d, pair(sin_freq),
      *mats, tw, wt)


def _hyena_kernel(v_ref, x1_ref, x2_ref, g_ref, cw_ref, cb_ref, skip_ref, *refs, L):
    plan = _hy_plan(L)
    depth = plan["depth"]
    n_mats = len(plan["base"])
    mats = refs[:n_mats]
    tw_ref, hr_ref, hi_ref, hx_ref, o_ref, a_scr, x_scr, y_scr, xbf_scr, gc_scr, gs_scr = refs[n_mats:]
    ck = min(HY_ROWS, L)
    halo = 16
    chunks = [(t0, t0 + ck) for t0 in range(0, L, ck)]

    def split_store(scr, t0, t1, val):
        for h, part in enumerate(_lane_halves(val)):
            scr[h, t0:t1, :] = part

    def joined(scr, t0, t1):
        return jnp.concatenate([scr[h, t0:t1, :] for h in range(HY_W // 128)], axis=1)

    def short_conv(ref, j, t0, t1):
        lo, hi = max(t0 - halo, 0), min(t1 + halo, L)
        u = ref[lo:hi, :].astype(F32)
        row = lax.broadcasted_iota(jnp.int32, (hi - lo, HY_W), 0)
        prev = pltpu.roll(u, 1, 0)
        nxt = pltpu.roll(u, hi - lo - 1, 0)
        if lo == 0:
            prev = jnp.where(row == 0, 0.0, prev)
        if hi == L:
            nxt = jnp.where(row == hi - lo - 1, 0.0, nxt)
        w = cw_ref[:, j * HY_W:(j + 1) * HY_W]
        z = prev * w[0:1, :] + u * w[1:2, :] + nxt * w[2:3, :] + cb_ref[:, j * HY_W:(j + 1) * HY_W]
        return z[t0 - lo:t1 - lo, :]

    def long_conv(order):
        def product(spec, hr, hi):
            return _Spec([c * r + s * i for c, s, r, i in zip(spec.c, spec.s, hr, hi)],
                         [s * r - c * i for c, s, r, i in zip(spec.c, spec.s, hr, hi)])

        def consume(k0, k1, spec):
            nb = len(spec.c)
            z = product(spec, [hr_ref[order, b, k0:k1, :] for b in range(nb)],
                        [hi_ref[order, b, k0:k1, :] for b in range(nb)])
            for i, leaf in enumerate(_backward(depth, z, _block_twiddles(plan, tw_ref, k0, k1))):
                gc_scr[i, k0:k1, :] = leaf.c[0].astype(BF16)
                gs_scr[i, k0:k1, :] = leaf.s[0].astype(BF16)

        extras = _spectrum_of(plan, a_scr, xbf_scr, mats, tw_ref, consume)
        ne = len(extras.c)
        zx = product(extras, [hx_ref[order, 0, e] for e in range(ne)], [hx_ref[order, 1, e] for e in range(ne)])
        tops = [leaf.c[0] for leaf in _backward(depth, zx, plan["tw_extras"])]
        _sequence_of(plan, gc_scr, gs_scr, tops, mats, y_scr)

    for t0, t1 in chunks:
        split_store(a_scr, t0, t1, short_conv(v_ref, 0, t0, t1))
        x_scr[t0:t1, :] = short_conv(x1_ref, 1, t0, t1)
    long_conv(0)
    for t0, t1 in chunks:
        conv = joined(y_scr, t0, t1) + joined(a_scr, t0, t1) * skip_ref[0:1, :]
        split_store(a_scr, t0, t1, x_scr[t0:t1, :] * conv)
    long_conv(1)
    for t0, t1 in chunks:
        conv = joined(y_scr, t0, t1) + joined(a_scr, t0, t1) * skip_ref[1:2, :]
        y = short_conv(x2_ref, 2, t0, t1) * conv
        o_ref[t0:t1, :] = (y * _silu(g_ref[t0:t1, :].astype(F32))).astype(o_ref.dtype)


def _hyena(u, conv_w, conv_b, skip, consts, hr, hi, hx, B, L):
    plan = _hy_plan(L)
    m, nb, ne = plan["m"], plan["n_blocks"], plan["n_extras"]
    mats, tw, _ = consts
    col = lambda j: pl.BlockSpec((L, CB), lambda b: (b, j))
    const = lambda *shape: _resident(shape, lambda b: (0,) * len(shape))
    halves = HY_W // 128
    return pl.pallas_call(
        functools.partial(_hyena_kernel, L=L),
        grid=(B,),
        in_specs=[
            col(CB_HY_V), col(CB_HY_X1), col(CB_HY_X2), col(CB_HY_G),
            const(3, 3 * HY_W), const(1, 3 * HY_W), const(2, HY_W),
            *[const(m, m) for _ in mats], const(*tw.shape),
            const(2, nb, m, HY_W), const(2, nb, m, HY_W), const(2, 2, ne, 1, HY_W),
        ],
        out_specs=pl.BlockSpec((L, HY_W), lambda b: (b, 0)),
        out_shape=jax.ShapeDtypeStruct((B * L, HY_W), BF16),
        scratch_shapes=[
            pltpu.VMEM((halves, L, 128), F32), pltpu.VMEM((L, HY_W), F32), pltpu.VMEM((halves, L, 128), F32),
            pltpu.VMEM((nb, m, HY_W), BF16), pltpu.VMEM((nb, m, HY_W), BF16), pltpu.VMEM((nb, m, HY_W), BF16),
        ],
        compiler_params=_cparams("parallel"),
        name="hyena",
    )(u, u, u, u, conv_w, conv_b[None, :], skip, *mats, tw, hr, hi, hx)


def _ret_kernel(*refs, L, Lc, has_init):
    if has_init:
        (q_ref, k_ref, v_ref, g_ref, kc_ref, vc_ref, cos_ref, sin_ref, rate_ref, rrow_ref, ones_ref,
         o_ref, q_scr, k_scr, sf_scr, sb_scr, r_scr) = refs
    else:
        (q_ref, k_ref, v_ref, g_ref, cos_ref, sin_ref, rate_ref, rrow_ref, ones_ref,
         o_ref, q_scr, k_scr, sf_scr, sb_scr, r_scr) = refs
    C = min(RET_CHUNK, L)
    nch = L // C
    W = RET_W
    quarter = RET_HEAD_DIM // 4
    block = ones_ref[...]
    lg = -jnp.exp(rate_ref[...])
    lg_f, lg_b = lg[0:1, :], lg[1:2, :]

    lane = lax.broadcasted_iota(jnp.int32, (L, W), 1)
    first_quarter = (lane % (2 * quarter)) < quarter

    def rope(a):
        swapped = jnp.where(first_quarter, pltpu.roll(a, W - quarter, 1), pltpu.roll(a, quarter, 1))
        return a * cos_ref[...] + swapped * sin_ref[...]

    if has_init:
        q_scr[...] = rope(q_ref[...].astype(F32)).astype(BF16)
        k_scr[...] = rope(k_ref[...].astype(F32) * (RET_HEAD_DIM ** -0.5)).astype(BF16)
    else:
        q_scr[...] = q_ref[...]
        k_scr[...] = (k_ref[...].astype(F32) * (RET_HEAD_DIM ** -0.5)).astype(BF16)

    def decays(n_rows):
        pos = lax.broadcasted_iota(jnp.int32, (n_rows, W), 0).astype(F32)
        return (jnp.exp(lg_f * (n_rows - 1.0 - pos)).astype(BF16),
                jnp.exp(lg_b * pos).astype(BF16))

    def chunk_states(k_bf, v_bf, zf, zb):
        return _dot_tn(k_bf * zf, v_bf) * block, _dot_tn(k_bf * zb, v_bf) * block

    zeta_f, zeta_b = decays(C)
    if has_init:
        zc_f, zc_b = decays(Lc)
        kc = (kc_ref[...].astype(F32) * (RET_HEAD_DIM ** -0.5)).astype(BF16)
        s0_f, s0_b = chunk_states(kc, vc_ref[...], zc_f, zc_b)
    else:
        s0_f = jnp.zeros((W, W), F32)
        s0_b = jnp.zeros((W, W), F32)

    for n in range(nch):
        kv_f, kv_b = chunk_states(k_scr[n * C:(n + 1) * C, :], v_ref[n * C:(n + 1) * C, :], zeta_f, zeta_b)
        sf_scr[n] = kv_f
        sb_scr[n] = kv_b
    dec_f = jnp.exp(lg_f * float(C))
    dec_b = jnp.exp(lg_b * float(C))
    state = s0_f
    for n in range(nch):
        kv = sf_scr[n]
        sf_scr[n] = state
        state = dec_f * state + kv
    state = s0_b
    for n in range(nch - 1, -1, -1):
        kv = sb_scr[n]
        sb_scr[n] = state
        state = dec_b * state + kv

    posc = lax.broadcasted_iota(jnp.int32, (C, W), 0).astype(F32)
    xi_f = jnp.exp(lg_f * (posc + 1.0)).astype(BF16)
    xi_b = jnp.exp(lg_b * (float(C) - posc)).astype(BF16)
    diff = (lax.broadcasted_iota(jnp.int32, (C, C), 0) - lax.broadcasted_iota(jnp.int32, (C, C), 1)).astype(F32)
    lane_c = lax.broadcasted_iota(jnp.int32, (1, W), 1)
    dmask = []
    for h in range(RET_HEADS):
        rf = -jnp.exp(rrow_ref[h:h + 1, 0:C])
        rb = -jnp.exp(rrow_ref[RET_HEADS + h:RET_HEADS + h + 1, 0:C])
        dmask.append((jnp.where(diff >= 0, jnp.exp(rf * jnp.maximum(diff, 0.0)), 0.0)
                      + jnp.where(diff <= 0, jnp.exp(rb * jnp.maximum(-diff, 0.0)), 0.0)).astype(BF16))

    for n in range(nch):
        qn = q_scr[n * C:(n + 1) * C, :]
        kn = k_scr[n * C:(n + 1) * C, :]
        vn = v_ref[n * C:(n + 1) * C, :]
        lhs = [qn * xi_f, qn * xi_b]
        rhs = [sf_scr[n].astype(BF16), sb_scr[n].astype(BF16)]
        zero = jnp.zeros_like(qn)
        for h in range(RET_HEADS):
            sel = (lane_c // RET_HEAD_DIM) == h
            lhs.append(_dot_nt(jnp.where(sel, qn, zero), kn).astype(BF16) * dmask[h])
            rhs.append(jnp.where(sel, vn, zero))
        r_scr[n * C:(n + 1) * C, :] = _dot(jnp.concatenate(lhs, axis=1), jnp.concatenate(rhs, axis=0))

    r = r_scr[...]
    ss = _dot((r * r).astype(BF16), block.astype(BF16))
    rn = r * lax.rsqrt(ss * (1.0 / RET_HEAD_DIM) + EPS)
    o_ref[...] = (rn * _silu(g_ref[...].astype(F32))).astype(o_ref.dtype)


def _retention(u, uc, ret_log_rate, B, L, Lc, has_init):
    C = min(RET_CHUNK, L)
    nch = L // C
    col = lambda j: pl.BlockSpec((L, CB), lambda b: (b, j))
    ccol = lambda j: pl.BlockSpec((Lc, CB), lambda b: (b, j))
    cos, sin = _rope_tables(L)
    rate_lane = jnp.repeat(ret_log_rate, RET_HEAD_DIM, axis=1)
    rate_row = jnp.broadcast_to(ret_log_rate.reshape(2 * RET_HEADS, 1), (2 * RET_HEADS, RET_W))
    ones = jnp.asarray(_head_block_ones(RET_W))
    in_specs = [col(CB_RE_Q), col(CB_RE_K), col(CB_RE_V), col(CB_RE_G)]
    args = [u, u, u, u]
    if has_init:
        in_specs += [ccol(CB_RE_K), ccol(CB_RE_V)]
        args += [uc, uc]
    in_specs += [
        _resident((L, RET_W), lambda b: (0, 0)),
        _resident((L, RET_W), lambda b: (0, 0)),
        _resident((2, RET_W), lambda b: (0, 0)),
        _resident((2 * RET_HEADS, RET_W), lambda b: (0, 0)),
        _resident((RET_W, RET_W), lambda b: (0, 0)),
    ]
    args += [jnp.asarray(cos), jnp.asarray(sin), rate_lane, rate_row, ones]
    return pl.pallas_call(
        functools.partial(_ret_kernel, L=L, Lc=Lc, has_init=has_init),
        grid=(B,),
        in_specs=in_specs,
        out_specs=pl.BlockSpec((L, RET_W), lambda b: (b, 0)),
        out_shape=jax.ShapeDtypeStruct((B * L, RET_W), BF16),
        scratch_shapes=[
            pltpu.VMEM((L, RET_W), BF16), pltpu.VMEM((L, RET_W), BF16),
            pltpu.VMEM((nch, RET_W, RET_W), F32), pltpu.VMEM((nch, RET_W, RET_W), F32),
            pltpu.VMEM((L, RET_W), F32),
        ],
        compiler_params=_cparams("parallel"),
        name="retention",
    )(*args)


def kernel(x, c, ctx, c_ctx, norm_w, ada_w, ada_b, w_in, w_out, na_q_gain, na_k_gain, na_rpb, hy_conv_w, hy_conv_b,
           hy_w1, hy_b1, hy_w2, hy_b2, hy_w3, hy_sin_freq, hy_skip, ret_log_rate):
    B, L, D = x.shape
    Lc = ctx.shape[1]
    assert D == D_MODEL and L % (GRID_W * NA_G * NA_GPS) == 0 and L // GRID_W >= NA_WIN
    assert NA_G == NA_KT // GRID_W and (L // GRID_W - NA_WIN) % NA_G == 0 and Lc == NA_KT

    n_cond = 16
    cc = jnp.concatenate([c, c_ctx[None, :], jnp.zeros((n_cond - B - 1, D), F32)], axis=0)
    mods = _modulation(cc, ada_w, ada_b)

    dft_x = _hyena_consts(L)
    dft_c = _hyena_consts(Lc)

    tm_x = 512
    xf = x.reshape(B * L, D)
    cf = ctx.reshape(B * Lc, D)
    x_mod = lambda i: i // (L // tm_x)
    c_mod = lambda i: B

    w_in_bf_all = w_in.astype(BF16)
    w_out_bf_all = w_out.astype(BF16)

    def layer_params(i):
        return mods[i].reshape(n_cond, 1, 3 * D), norm_w[i][None, :], (w_in_bf_all, i)

    mod, nw, w_in_bf = layer_params(0)
    u = _projection(xf, x_mod, tm_x, in_args=(mod, nw, w_in_bf))
    tm_c = min(1024, B * Lc)
    uc = _projection(cf, c_mod, tm_c, in_args=(mod, nw, w_in_bf))
    for i in range(DEPTH):
        w_out_bf = (w_out_bf_all, i)
        filt = (hy_w1[i], hy_b1[i], hy_w2[i], hy_b2[i], hy_w3[i], hy_sin_freq[i])
        last = i == DEPTH - 1
        nxt = None if last else layer_params(i + 1)

        za = _neighborhood_attention(u, uc, na_q_gain[i], na_k_gain[i], na_rpb[i], B, L, Lc)
        hr, hi, hm = _hyena_filter(L, *filt, dft_x)
        zy = _hyena(u, hy_conv_w[i], hy_conv_b[i], hy_skip[i], dft_x, hr, hi, hm, B, L)
        zr = _retention(u, uc, ret_log_rate[i], B, L, Lc, True)
        if not last:
            zac = _context_attention(uc, na_q_gain[i], na_k_gain[i], B, Lc)
            hrc, hic, hmc = _hyena_filter(Lc, *filt, dft_c)
            zyc = _hyena(uc, hy_conv_w[i], hy_conv_b[i], hy_skip[i], dft_c, hrc, hic, hmc, B, Lc)
            zrc = _retention(uc, uc, ret_log_rate[i], B, Lc, Lc, False)
            cf, uc = _projection(cf, c_mod, tm_c, out_args=(zac, zyc, zrc, mod, w_out_bf), in_args=nxt)
            xf, u = _projection(xf, x_mod, tm_x, out_args=(za, zy, zr, mod, w_out_bf), in_args=nxt)
            mod = nxt[0]
        else:
            tm_last = 2 * tm_x
            xf = _projection(xf, lambda i: i // (L // tm_last), tm_last, out_args=(za, zy, zr, mod, w_out_bf))

    return xf.reshape(B, L, D)
```

```python
import functools
import math

import numpy as np
import jax
import jax.numpy as jnp
from jax import lax
from jax.experimental import pallas as pl
from jax.experimental.pallas import tpu as pltpu

F32 = jnp.float32
BF16 = jnp.bfloat16

D_MODEL = 1024
DEPTH = 2
GRID_W = 64
NA_HEADS = 8
NA_HEAD_DIM = 64
NA_W = NA_HEADS * NA_HEAD_DIM
NA_KH = 8
NA_KW = 16
HY_W = 256
HY_BANDS = 8
HY_EMB = 1 + 2 * HY_BANDS
HY_EMB_PAD = 32
HY_FFN = 64
HY_FAST_DECAY = 0.3
HY_SLOW_DECAY = 1.5
HY_TARGET = 1e-2
RET_HEADS = 4
RET_HEAD_DIM = 64
RET_W = RET_HEADS * RET_HEAD_DIM
ROPE_BASE = 10000.0
EPS = 1e-6
NEG_INF = -1e30
LOG2E = 1.4426950408889634
IN_W = 4 * NA_W + 4 * HY_W + 4 * RET_W
MIX_W = NA_W + HY_W + RET_W

CB = 256
CB_HY_V, CB_HY_X1, CB_HY_X2, CB_HY_G = 8, 9, 10, 11
CB_RE_Q, CB_RE_K, CB_RE_V, CB_RE_G = 12, 13, 14, 15

NA_G = 4
NA_WIN = NA_G + NA_KH
NORM_W = 256
NA_GPS = 4
NA_KT = 256
NA_DR_MIN = -4
NA_DR_NUM = 23
RET_CHUNK = 256
HY_ROWS = 512
HY_BASE = 512
VMEM_LIMIT = 56 * 1024 * 1024


def _cparams(*sem):
    return pltpu.CompilerParams(dimension_semantics=sem, vmem_limit_bytes=VMEM_LIMIT)


def _resident(shape, index_map):
    return pl.BlockSpec(shape, index_map, pipeline_mode=pl.Buffered(1))


def _silu(x):
    return x * (1.0 / (1.0 + jnp.exp(-x)))


def _dot(a, b):
    return jnp.dot(a, b, preferred_element_type=F32)


def _dot_nt(a, b):
    return lax.dot_general(a, b, (((1,), (1,)), ((), ())), preferred_element_type=F32)


def _dot_tn(a, b):
    return lax.dot_general(a, b, (((0,), (0,)), ((), ())), preferred_element_type=F32)


@functools.lru_cache(maxsize=None)
def _bin_map(n, depth):
    if depth == 0:
        return [np.arange(n)], [n]
    hb, he = _bin_map(n // 2, depth - 1)
    blocks = [b for b in hb] + [n - b for b in hb]
    extras = list(he) + [n - e for e in he if e < n // 2]
    return blocks, extras


@functools.lru_cache(maxsize=None)
def _hy_plan(L):
    depth = 0
    while L >> depth > HY_BASE:
        depth += 1
    m = L >> depth
    k = np.arange(m, dtype=np.int64)[:, None] * np.arange(m, dtype=np.int64)[None, :]
    ang = (k % (2 * m)).astype(np.float64) * (math.pi / m)
    base = [np.cos(ang), np.sin(ang)]
    if depth:
        odd = (np.arange(m, dtype=np.int64)[:, None] * (2 * np.arange(m, dtype=np.int64)[None, :] + 1)) % (4 * m)
        odd = odd.astype(np.float64) * (math.pi / (2 * m))
        base += [np.cos(odd), np.sin(odd), np.cos(odd).T, np.sin(odd).T]
    base = tuple(np.ascontiguousarray(a).astype(np.float32) for a in base)
    tw_blocks, tw_extras = [], []
    for level in range(1, depth + 1):
        n = m << level
        hb, he = _bin_map(n // 2, level - 1)
        tw_blocks.append([(np.cos(math.pi * b / n), np.sin(math.pi * b / n)) for b in hb])
        tw_extras.append([(math.cos(math.pi * e / n), math.sin(math.pi * e / n), e == n // 2) for e in he])
    blocks, extras = _bin_map(L, depth)
    inv_n = 1.0 / (2 * L)
    seen = set()

    def weight(b):
        w = 0.0 if b in seen else (inv_n if b in (0, L) else 2.0 * inv_n)
        seen.add(b)
        return w

    w_blocks = [np.array([weight(int(b)) for b in blk]) for blk in blocks]
    w_extras = [weight(int(e)) for e in extras]
    tables = [np.broadcast_to(t[:, None], (m, HY_W)) for lvl in tw_blocks for pair in lvl for t in pair]
    tw_table = np.stack(tables).astype(np.float32) if tables else np.zeros((1, 8, HY_W), np.float32)
    w_table = np.stack([np.broadcast_to(w[:, None], (m, HY_W)) for w in w_blocks]).astype(np.float32)
    return dict(depth=depth, m=m, base=base, tw_extras=tw_extras, n_blocks=len(blocks), n_extras=len(extras),
                w_extras=w_extras, tw_table=tw_table, w_table=w_table,
                tw_index=[[2 * (sum(len(l) for l in tw_blocks[:lv]) + i) for i in range(len(tw_blocks[lv]))]
                          for lv in range(depth)])


@functools.lru_cache(maxsize=None)
def _filter_features(L):
    t = np.linspace(0.0, 1.0, L)[:, None]
    omega = 2.0 * math.pi * np.arange(L)[:, None] / L
    bands = np.linspace(1e-4, HY_BANDS - 1, HY_BANDS)[None, :]
    z = np.concatenate([t, np.cos(bands * omega), -np.sin(bands * omega)], axis=-1)
    z = np.pad(z, ((0, 0), (0, HY_EMB_PAD - HY_EMB)))
    deltas = np.abs(np.linspace(math.log(HY_TARGET) / HY_SLOW_DECAY, math.log(HY_TARGET) / HY_FAST_DECAY, HY_W))
    window = np.exp(-t * deltas[None, :])
    return z.astype(np.float32), window.astype(np.float32)


@functools.lru_cache(maxsize=None)
def _rope_tables(L):
    half = RET_HEAD_DIM // 2
    quarter = half // 2
    t = np.arange(L)
    pos = np.stack([t // GRID_W, t % GRID_W], axis=0).astype(np.float64)
    freqs = ROPE_BASE ** (-np.arange(quarter, dtype=np.float64) / quarter)
    cos = np.zeros((L, RET_HEAD_DIM))
    sin = np.zeros((L, RET_HEAD_DIM))
    for a in range(2):
        ang = pos[a][:, None] * freqs[None, :]
        base = a * half
        cos[:, base:base + quarter] = np.cos(ang)
        cos[:, base + quarter:base + half] = np.cos(ang)
        sin[:, base:base + quarter] = -np.sin(ang)
        sin[:, base + quarter:base + half] = np.sin(ang)
    return (np.tile(cos, (1, RET_HEADS)).astype(np.float32), np.tile(sin, (1, RET_HEADS)).astype(np.float32))


@functools.lru_cache(maxsize=None)
def _head_block_ones(width):
    i = np.arange(width) // NA_HEAD_DIM
    return (i[:, None] == i[None, :]).astype(np.float32)


def _mod_kernel(c_ref, w_ref, b_ref, o_ref):
    s = _silu(c_ref[...])
    w = w_ref[...]
    s_hi = s.astype(BF16)
    w_hi = w.astype(BF16)
    s_lo = (s - s_hi.astype(F32)).astype(BF16)
    w_lo = (w - w_hi.astype(F32)).astype(BF16)
    o_ref[...] = _dot(s_hi, w_hi) + (_dot(s_hi, w_lo) + _dot(s_lo, w_hi)) + b_ref[...]


def _modulation(cc, ada_w, ada_b):
    R = cc.shape[0]
    tn = 1024
    return pl.pallas_call(
        _mod_kernel,
        grid=(DEPTH, 3 * D_MODEL // tn),
        in_specs=[
            pl.BlockSpec((R, D_MODEL), lambda i, j: (0, 0)),
            pl.BlockSpec((None, D_MODEL, tn), lambda i, j: (i, 0, j)),
            pl.BlockSpec((None, 1, tn), lambda i, j: (i, 0, j)),
        ],
        out_specs=pl.BlockSpec((None, R, tn), lambda i, j: (i, 0, j)),
        out_shape=jax.ShapeDtypeStruct((DEPTH, R, 3 * D_MODEL), F32),
        compiler_params=_cparams("parallel", "parallel"),
        name="modulation",
    )(cc, ada_w, ada_b.reshape(DEPTH, 1, 3 * D_MODEL))


def _proj_kernel(*refs, with_out, with_in):
    refs = list(refs)
    if with_out:
        za_ref, zy_ref, zr_ref, x_ref, mod_o_ref, w_out_ref = refs[:6]
        del refs[:6]
    else:
        x_ref = refs.pop(0)
    if with_in:
        mod_i_ref, nw_ref, w_in_ref = refs[:3]
        del refs[:3]
    x = x_ref[...]
    if with_out:
        acc = _dot(za_ref[...], w_out_ref[0:NA_W, :])
        acc += _dot(zy_ref[...], w_out_ref[NA_W:NA_W + HY_W, :])
        acc += _dot(zr_ref[...], w_out_ref[NA_W + HY_W:MIX_W, :])
        x = x + mod_o_ref[:, 2 * D_MODEL:3 * D_MODEL] * acc
        refs.pop(0)[...] = x
    if with_in:
        u_ref = refs.pop(0)
        xn = x * lax.rsqrt(jnp.mean(x * x, axis=-1, keepdims=True) + EPS)
        shift = mod_i_ref[:, 0:D_MODEL]
        scale = mod_i_ref[:, D_MODEL:2 * D_MODEL]
        h = (xn * nw_ref[...] * (1.0 + scale) + shift).astype(BF16)
        tn = 1024
        for j in range(IN_W // tn):
            u_ref[:, j * tn:(j + 1) * tn] = _dot(h, w_in_ref[:, j * tn:(j + 1) * tn]).astype(u_ref.dtype)


def _projection(xf, mod_index, tm, out_args=None, in_args=None):
    R = xf.shape[0]
    row = lambda w: pl.BlockSpec((tm, w), lambda i: (i, 0))
    mod_spec = pl.BlockSpec((None, 1, 3 * D_MODEL), lambda i: (mod_index(i), 0, 0))
    args, in_specs, out_specs, out_shape = [], [], [], []
    if out_args is not None:
        za, zy, zr, mod_o, (w_out_bf, layer_o) = out_args
        args += [za, zy, zr, xf, mod_o, w_out_bf]
        in_specs += [row(NA_W), row(HY_W), row(RET_W), row(D_MODEL), mod_spec,
                     _resident((None, MIX_W, D_MODEL), lambda i: (layer_o, 0, 0))]
        out_specs.append(row(D_MODEL))
        out_shape.append(jax.ShapeDtypeStruct((R, D_MODEL), F32))
    else:
        args.append(xf)
        in_specs.append(row(D_MODEL))
    if in_args is not None:
        mod_i, norm_w, (w_in_bf, layer_i) = in_args
        args += [mod_i, norm_w, w_in_bf]
        in_specs += [mod_spec, _resident((1, D_MODEL), lambda i: (0, 0)),
                     _resident((None, D_MODEL, IN_W), lambda i: (layer_i, 0, 0))]
        out_specs.append(row(IN_W))
        out_shape.append(jax.ShapeDtypeStruct((R, IN_W), BF16))
    outs = pl.pallas_call(
        functools.partial(_proj_kernel, with_out=out_args is not None, with_in=in_args is not None),
        grid=(R // tm,),
        in_specs=in_specs,
        out_specs=out_specs,
        out_shape=out_shape,
        compiler_params=_cparams("parallel"),
        name="projection",
    )(*args)
    return outs if len(outs) > 1 else outs[0]


def _head_rms(x, ones_bf, gain):
    sq = (x * x).astype(BF16)
    ss = jnp.concatenate([_dot(sq[:, c:c + NORM_W], ones_bf) for c in range(0, x.shape[1], NORM_W)], axis=1)
    return x * lax.rsqrt(ss * (1.0 / NA_HEAD_DIM) + EPS) * gain


def _pair_masks():
    lane = lax.broadcasted_iota(jnp.int32, (1, 2 * NA_HEAD_DIM), 1)
    return lane < NA_HEAD_DIM


def _attend_heads(q, key_parts, val_parts, bias_fn):
    first = _pair_masks()
    outs = []
    for pair in range(NA_HEADS // 2):
        lo = pair * 2 * NA_HEAD_DIM
        hi = lo + 2 * NA_HEAD_DIM
        qp = q[:, lo:hi]
        o_pair = None
        for sub in range(2):
            h = 2 * pair + sub
            sel = first if sub == 0 else jnp.logical_not(first)
            qm = jnp.where(sel, qp, 0.0).astype(BF16)
            scores = []
            for part, kp in enumerate(key_parts):
                s = _dot_nt(qm, kp[:, lo:hi])
                b = bias_fn(h, part)
                if b is not None:
                    s = s + b
                scores.append(s)
            m = scores[0].max(axis=-1, keepdims=True)
            for s in scores[1:]:
                m = jnp.maximum(m, s.max(axis=-1, keepdims=True))
            denom = None
            acc = None
            for s, vp in zip(scores, val_parts):
                p = jnp.exp(s - m)
                ps = p.sum(axis=-1, keepdims=True)
                denom = ps if denom is None else denom + ps
                pv = _dot(p.astype(BF16), vp[:, lo:hi])
                acc = pv if acc is None else acc + pv
            o_h = acc * (1.0 / denom)
            o_pair = o_h if o_pair is None else jnp.where(first, o_pair, o_h)
        outs.append(o_pair)
    return jnp.concatenate(outs, axis=-1)


def _na_kernel(q_ref, k_ref, v_ref, g_ref, kc_ref, vc_ref, qg_ref, kg_ref, ones_ref, rpb_ref, o_ref,
               kn_scr, kcn_scr, vat_scr, vbt_scr, vcat_scr, vcbt_scr, bias_scr, mask_scr, s0_scr, s1_scr,
               p0_scr, p1_scr, qm_scr, ot_scr, *, rows):
    batch = pl.program_id(0)
    grp = pl.program_id(1)
    ones_bf = ones_ref[...]
    tq = NA_G * GRID_W
    nwin = NA_WIN * GRID_W
    Lc = kc_ref.shape[0]
    pair_w = 2 * NA_HEAD_DIM
    kt = NA_KT
    first = lax.broadcasted_iota(jnp.int32, (1, pair_w), 1) < NA_HEAD_DIM

    @pl.when((batch == 0) & (grp == 0))
    def _():
        ck = lax.broadcasted_iota(jnp.int32, (GRID_W, pair_w), 0)
        lane = lax.broadcasted_iota(jnp.int32, (GRID_W, pair_w), 1)
        cq = lane % GRID_W
        col_start = jnp.clip(cq - NA_KW // 2, 0, GRID_W - NA_KW)
        col_ok = (ck >= col_start) & (ck < col_start + NA_KW)
        left = lane < GRID_W

        def body(i, carry):
            for h in range(NA_HEADS):
                v0 = jnp.broadcast_to(rpb_ref[i + 1, h:h + 1, :], (GRID_W, pair_w))
                v1 = jnp.broadcast_to(rpb_ref[i, h:h + 1, :], (GRID_W, pair_w))
                t0 = pltpu.roll(v0, pair_w - (NA_KW - 1), 1, stride=1, stride_axis=0)
                t1 = pltpu.roll(v1, GRID_W - (NA_KW - 1), 1, stride=1, stride_axis=0)
                bias_scr[i, h] = jnp.where(col_ok, jnp.where(left, t0, t1), NEG_INF)
            return carry

        lax.fori_loop(0, NA_DR_NUM, body, 0)

    @pl.when(grp == 0)
    def _():
        kgain = kg_ref[...]
        own_a = (lax.broadcasted_iota(jnp.int32, (NA_W, 1), 0) % pair_w) < NA_HEAD_DIM
        for src, dst, vat, vbt, vsrc in ((k_ref, kn_scr, vat_scr, vbt_scr, v_ref),
                                         (kc_ref, kcn_scr, vcat_scr, vcbt_scr, vc_ref)):
            for i in range(src.shape[0] // kt):
                sl = slice(i * kt, (i + 1) * kt)
                dst[sl, :] = _head_rms(src[sl, :].astype(F32), ones_bf, kgain).astype(BF16)
                vt = vsrc[sl, :].astype(F32).T
                vat[i] = jnp.where(own_a, vt, 1.0).astype(BF16)
                vbt[i] = jnp.where(own_a, 1.0, vt).astype(BF16)

    n_kt = (nwin + Lc) // kt
    rows_per_tile = kt // GRID_W
    n_ch = (nwin + Lc) // GRID_W
    n_items = NA_GPS * NA_HEADS

    s_bufs = (s0_scr, s1_scr)
    p_bufs = (p0_scr, p1_scr)
    row_zero = jnp.minimum(grp, 0)

    def rows_of(start, size):
        return pl.ds(pl.multiple_of(row_zero + start, GRID_W), size)

    def head_lanes(h):
        return slice((h // 2) * pair_w, (h // 2 + 1) * pair_w)

    key0, tile0, dr_base = [], [], []
    for gi in range(NA_GPS):
        r0 = (grp * NA_GPS + gi) * NA_G
        ws = jnp.clip(r0 - NA_KH // 2, 0, rows - NA_WIN)
        key0.append(pl.multiple_of(ws * GRID_W, kt))
        tile0.append(ws // rows_per_tile)
        dr_base.append(ws - r0 + (NA_KH - 1) - NA_DR_MIN)
        w_id = lax.broadcasted_iota(jnp.int32, (16, tq), 0)
        g_id = lax.broadcasted_iota(jnp.int32, (16, tq), 1) // GRID_W
        lo = jnp.clip(r0 + g_id - NA_KH // 2, 0, rows - NA_KH) - ws
        mask_scr[gi] = jnp.where((w_id >= lo) & (w_id < lo + NA_KH), 0.0, NEG_INF).astype(F32)
        q = _head_rms(q_ref[gi * tq:(gi + 1) * tq, :].astype(F32), ones_bf,
                      qg_ref[...] * (NA_HEAD_DIM ** -0.5 * LOG2E))
        for h in range(NA_HEADS):
            sel = first if h % 2 == 0 else jnp.logical_not(first)
            qm_scr[gi * NA_HEADS + h, rows_of(0, tq), :] = jnp.where(sel, q[:, head_lanes(h)], 0.0).astype(BF16)

    col_max = {}

    def qk_piece(n, j):
        gi, h = divmod(n, NA_HEADS)
        last = j == n_kt - 1
        keys = kcn_scr[:, head_lanes(h)] if last else kn_scr[pl.ds(key0[gi] + j * kt, kt), head_lanes(h)]
        sv = _dot_nt(keys, qm_scr[n, rows_of(0, tq), :])
        for dw in range(rows_per_tile):
            sw = sv[dw * GRID_W:(dw + 1) * GRID_W, :]
            if not last:
                w = j * rows_per_tile + dw
                bias = jnp.concatenate([bias_scr[dr_base[gi] + (w - g), h] for g in range(0, NA_G, 2)], axis=-1)
                sw = sw + bias + mask_scr[gi, w:w + 1, :]
            s_bufs[n % 2][rows_of(j * kt + dw * GRID_W, GRID_W), :] = sw
            part = sw.reshape(GRID_W // 8, 8, tq).max(axis=0)
            col_max[n] = part if (j == 0 and dw == 0) else jnp.maximum(col_max[n], part)

    def exp_chunk(n, c):
        if c == 0:
            col_max[n] = col_max[n].max(axis=0, keepdims=True)
        rows_c = rows_of(c * GRID_W, GRID_W)
        p_bufs[n % 2][rows_c, :] = jnp.exp2(s_bufs[n % 2][rows_c, :] - col_max[n]).astype(BF16)
        if c == n_ch - 1:
            del col_max[n]

    def pv(n):
        gi, h = divmod(n, NA_HEADS)
        vt, vct = (vat_scr, vcat_scr) if h % 2 == 0 else (vbt_scr, vcbt_scr)
        ot = _dot(vct[0, head_lanes(h), :], p_bufs[n % 2][rows_of(nwin, Lc), :])
        for j in range(n_kt - 1):
            ot += _dot(vt[tile0[gi] + j, head_lanes(h), :], p_bufs[n % 2][rows_of(j * kt, kt), :])
        own, den = (ot[0:NA_HEAD_DIM], ot[NA_HEAD_DIM:]) if h % 2 == 0 else (ot[NA_HEAD_DIM:], ot[0:NA_HEAD_DIM])
        ot_scr[rows_of(n * NA_HEAD_DIM, NA_HEAD_DIM), :] = own * (1.0 / den)
        if h % 2 == 1:
            o_pair = ot_scr[rows_of((n - 1) * NA_HEAD_DIM, pair_w), :].T
            q_rows = slice(gi * tq, (gi + 1) * tq)
            gate = _silu(g_ref[q_rows, head_lanes(h)].astype(F32))
            o_ref[q_rows, head_lanes(h)] = (o_pair * gate).astype(o_ref.dtype)

    for t in range(n_items + 2):
        for i in range(n_ch):
            if 0 <= t - 1 < n_items:
                exp_chunk(t - 1, i)
            if i % 4 == 0 and t < n_items and i // 4 < n_kt:
                qk_piece(t, i // 4)
            if i == 2 and 0 <= t - 2:
                pv(t - 2)


def _neighborhood_attention(u, uc, q_gain, k_gain, rpb, B, L, Lc):
    rows = L // GRID_W
    ngrp = rows // (NA_G * NA_GPS)
    tq = NA_G * GRID_W
    tstep = tq * NA_GPS
    ones = jnp.asarray(_head_block_ones(NORM_W)).astype(BF16)
    qg = jnp.tile(q_gain, NA_HEADS)[None, :]
    kg = jnp.tile(k_gain, NA_HEADS)[None, :]
    n_dr = 2 * NA_KH - 1
    rpb_t = jnp.transpose(rpb.astype(F32)[:, :, ::-1] * LOG2E, (1, 0, 2))
    lo_pad = 1 - NA_DR_MIN
    rpb_t = jnp.pad(rpb_t, ((lo_pad, NA_DR_NUM + 1 - lo_pad - n_dr), (0, 0), (0, 2 * GRID_W - (2 * NA_KW - 1))))
    return pl.pallas_call(
        functools.partial(_na_kernel, rows=rows),
        grid=(B, ngrp),
        in_specs=[
            pl.BlockSpec((tstep, NA_W), lambda b, g: (b * ngrp + g, 0)),
            pl.BlockSpec((L, NA_W), lambda b, g: (b, 1)),
            pl.BlockSpec((L, NA_W), lambda b, g: (b, 2)),
            pl.BlockSpec((tstep, NA_W), lambda b, g: (b * ngrp + g, 3)),
            pl.BlockSpec((Lc, NA_W), lambda b, g: (b, 1)),
            pl.BlockSpec((Lc, NA_W), lambda b, g: (b, 2)),
            _resident((1, NA_W), lambda b, g: (0, 0)),
            _resident((1, NA_W), lambda b, g: (0, 0)),
            _resident((NORM_W, NORM_W), lambda b, g: (0, 0)),
            _resident((NA_DR_NUM + 1, NA_HEADS, 2 * GRID_W), lambda b, g: (0, 0, 0)),
        ],
        out_specs=pl.BlockSpec((tstep, NA_W), lambda b, g: (b * ngrp + g, 0)),
        out_shape=jax.ShapeDtypeStruct((B * L, NA_W), BF16),
        scratch_shapes=[
            pltpu.VMEM((L, NA_W), BF16), pltpu.VMEM((Lc, NA_W), BF16),
            pltpu.VMEM((L // NA_KT, NA_W, NA_KT), BF16), pltpu.VMEM((L // NA_KT, NA_W, NA_KT), BF16),
            pltpu.VMEM((Lc // NA_KT, NA_W, NA_KT), BF16), pltpu.VMEM((Lc // NA_KT, NA_W, NA_KT), BF16),
            pltpu.VMEM((NA_DR_NUM, NA_HEADS, GRID_W, 2 * GRID_W), F32),
            pltpu.VMEM((NA_GPS, 16, tq), F32),
            pltpu.VMEM((NA_WIN * GRID_W + Lc, tq), F32), pltpu.VMEM((NA_WIN * GRID_W + Lc, tq), F32),
            pltpu.VMEM((NA_WIN * GRID_W + Lc, tq), BF16), pltpu.VMEM((NA_WIN * GRID_W + Lc, tq), BF16),
            pltpu.VMEM((NA_GPS * NA_HEADS, tq, 2 * NA_HEAD_DIM), BF16), pltpu.VMEM((NA_GPS * NA_W, tq), F32),
        ],
        compiler_params=_cparams("arbitrary", "arbitrary"),
        name="neighborhood_attention",
    )(u, u, u, u, uc, uc, qg, kg, ones, rpb_t)


def _ctx_attn_kernel(q_ref, k_ref, v_ref, g_ref, qg_ref, kg_ref, ones_ref, o_ref):
    ones_bf = ones_ref[...]
    q = _head_rms(q_ref[...].astype(F32), ones_bf, qg_ref[...] * (NA_HEAD_DIM ** -0.5))
    k = _head_rms(k_ref[...].astype(F32), ones_bf, kg_ref[...]).astype(BF16)
    o = _attend_heads(q, [k], [v_ref[...]], lambda h, part: None)
    o_ref[...] = (o * _silu(g_ref[...].astype(F32))).astype(o_ref.dtype)


def _context_attention(uc, q_gain, k_gain, B, Lc):
    ones = jnp.asarray(_head_block_ones(NORM_W)).astype(BF16)
    qg = jnp.tile(q_gain, NA_HEADS)[None, :]
    kg = jnp.tile(k_gain, NA_HEADS)[None, :]
    return pl.pallas_call(
        _ctx_attn_kernel,
        grid=(B,),
        in_specs=[
            pl.BlockSpec((Lc, NA_W), lambda b: (b, 0)),
            pl.BlockSpec((Lc, NA_W), lambda b: (b, 1)),
            pl.BlockSpec((Lc, NA_W), lambda b: (b, 2)),
            pl.BlockSpec((Lc, NA_W), lambda b: (b, 3)),
            _resident((1, NA_W), lambda b: (0, 0)),
            _resident((1, NA_W), lambda b: (0, 0)),
            _resident((NORM_W, NORM_W), lambda b: (0, 0)),
        ],
        out_specs=pl.BlockSpec((Lc, NA_W), lambda b: (b, 0)),
        out_shape=jax.ShapeDtypeStruct((B * Lc, NA_W), BF16),
        compiler_params=_cparams("parallel"),
        name="context_attention",
    )(uc, uc, uc, uc, qg, kg, ones)


def _alt_sign(shape):
    row = lax.broadcasted_iota(jnp.int32, shape, 0)
    return (1 - 2 * (row & 1)).astype(F32)


def _lane_halves(x):
    return [x[:, h * 128:(h + 1) * 128] for h in range(HY_W // 128)]


class _Spec:
    def __init__(self, c, s):
        self.c, self.s = list(c), list(s)


def _combine(P, R, tw):
    lo_c, lo_s, mi_c, mi_s = [], [], [], []
    for (pc, ps, rc, rs, (ct, st, single)) in zip(P.c, P.s, R.c, R.s, tw):
        if single:
            lo_c.append(pc)
            lo_s.append(rc)
            continue
        qc, qs = (rc, rs) if ct is None else (ct * rc - st * rs, st * rc + ct * rs)
        lo_c.append(pc + qc)
        lo_s.append(ps + qs)
        mi_c.append(pc - qc)
        mi_s.append(qs - ps)
    return _Spec(lo_c + mi_c, lo_s + mi_s)


def _combine_t(Z, tw):
    n_lo = len(tw)
    lo_c, lo_s, mi_c, mi_s = Z.c[:n_lo], Z.s[:n_lo], Z.c[n_lo:], Z.s[n_lo:]
    pc, ps, rc, rs = [], [], [], []
    mi = 0
    for i, (ct, st, single) in enumerate(tw):
        if single:
            pc.append(lo_c[i])
            ps.append(jnp.zeros_like(lo_c[i]))
            rc.append(lo_s[i])
            rs.append(jnp.zeros_like(lo_c[i]))
            continue
        g_qc = lo_c[i] - mi_c[mi]
        g_qs = lo_s[i] + mi_s[mi]
        pc.append(lo_c[i] + mi_c[mi])
        ps.append(lo_s[i] - mi_s[mi])
        rc.append(g_qc if ct is None else ct * g_qc + st * g_qs)
        rs.append(g_qs if ct is None else ct * g_qs - st * g_qc)
        mi += 1
    return _Spec(pc, ps), _Spec(rc, rs)


def _forward(level, leaves, tws):
    if level == 0:
        return leaves[0]
    half = len(leaves) // 2
    return _combine(_forward(level - 1, leaves[:half], tws), _forward(level - 1, leaves[half:], tws), tws[level - 1])


def _backward(level, Z, tws):
    if level == 0:
        return [Z]
    P, R = _combine_t(Z, tws[level - 1])
    return _backward(level - 1, P, tws) + _backward(level - 1, R, tws)


def _leaf_samples(depth, off=0, stride=1):
    if depth == 0:
        return [(off, stride)]
    return _leaf_samples(depth - 1, off, 2 * stride) + _leaf_samples(depth - 1, off + stride, 2 * stride)


def _block_twiddles(plan, tw_ref, k0, k1):
    return [[(None, None, False) if lv == 0 else (tw_ref[i, k0:k1, :], tw_ref[i + 1, k0:k1, :], False)
             for i in plan["tw_index"][lv]] for lv in range(plan["depth"])]


def _leaf_matrices(plan, mats, i, transposed):
    if plan["depth"] == 0 or i % 2 == 0:
        return mats[0], mats[1]
    return (mats[4], mats[5]) if transposed else (mats[2], mats[3])


def _spectrum_of(plan, src_scr, xbf_scr, mats, tw_ref, consume):
    depth, m = plan["depth"], plan["m"]
    cm = min(HY_ROWS // 2, m)
    sign = _alt_sign((cm, HY_W))
    tops = []
    for i, (off, stride) in enumerate(_leaf_samples(depth)):
        top = jnp.zeros((1, HY_W), F32)
        for j0 in range(0, m, cm):
            x = jnp.concatenate([src_scr[h, pl.ds(off + stride * j0, cm, stride=stride), :]
                                 for h in range(HY_W // 128)], axis=1)
            xbf_scr[i, j0:j0 + cm, :] = x.astype(BF16)
            top += jnp.sum(x * sign, axis=0, keepdims=True)
        tops.append(top)
    for k0 in range(0, m, cm):
        k1 = k0 + cm
        base = []
        for i in range(len(tops)):
            c_ref, s_ref = _leaf_matrices(plan, mats, i, False)
            base.append(_Spec([_dot(c_ref[k0:k1, :], xbf_scr[i])], [_dot(s_ref[k0:k1, :], xbf_scr[i])]))
        consume(k0, k1, _forward(depth, base, _block_twiddles(plan, tw_ref, k0, k1)))
    zero = jnp.zeros((1, HY_W), F32)
    return _forward(depth, [_Spec([t], [zero]) for t in tops], plan["tw_extras"])


def _sequence_of(plan, gc_scr, gs_scr, tops, mats, dst_scr):
    depth, m = plan["depth"], plan["m"]
    cm = min(HY_ROWS // 2, m)
    sign = _alt_sign((cm, HY_W))
    for i, (off, stride) in enumerate(_leaf_samples(depth)):
        c_ref, s_ref = _leaf_matrices(plan, mats, i, True)
        for j0 in range(0, m, cm):
            y = _dot(c_ref[j0:j0 + cm, :], gc_scr[i]) + _dot(s_ref[j0:j0 + cm, :], gs_scr[i]) + sign * tops[i]
            for h in range(HY_W // 128):
                dst_scr[h, pl.ds(off + stride * j0, cm, stride=stride), :] = y[:, h * 128:(h + 1) * 128]


def _filter_kernel(z_ref, win_ref, w1_ref, b1_ref, w2_ref, b2_ref, w3_ref, sf_ref, *refs, L):
    plan = _hy_plan(L)
    n_mats = len(plan["base"])
    mats = refs[:n_mats]
    tw_ref, wt_ref, hr_ref, hi_ref, hx_ref, h_scr, a_scr, d_scr, xbf_scr = refs[n_mats:]
    hp = lax.Precision.HIGHEST
    dot = lambda a, b: jnp.dot(a, b, preferred_element_type=F32, precision=hp)

    @pl.when(pl.program_id(0) == 0)
    def _():
        h = jnp.sin(sf_ref[0:1, :] * (dot(z_ref[...], w1_ref[...]) + b1_ref[...]))
        h_scr[...] = jnp.sin(sf_ref[1:2, :] * (dot(h, w2_ref[...]) + b2_ref[...]))

    taps = dot(h_scr[...], w3_ref[...])
    win = win_ref[...]
    row = lax.broadcasted_iota(jnp.int32, (L, HY_W), 0)
    hf = jnp.concatenate([taps[:, 0:HY_W], taps[:, 2 * HY_W:3 * HY_W]], axis=0) * win
    hb = jnp.concatenate([taps[:, HY_W:2 * HY_W], taps[:, 3 * HY_W:4 * HY_W]], axis=0) * win
    hb = jnp.where(row == 0, 0.0, hb)
    norm = jnp.sum(jnp.abs(hf), axis=0, keepdims=True) + jnp.sum(jnp.abs(hb), axis=0, keepdims=True)
    inv = 1.0 / norm
    for h, (a_half, d_half) in enumerate(zip(_lane_halves((hf + hb) * inv), _lane_halves((hf - hb) * inv))):
        a_scr[h] = a_half
        d_scr[h] = d_half

    def store_real(k0, k1, spec):
        for b, blk in enumerate(spec.c):
            hr_ref[b, k0:k1, :] = blk * wt_ref[b, k0:k1, :]

    def store_imag(k0, k1, spec):
        for b, blk in enumerate(spec.s):
            hi_ref[b, k0:k1, :] = -blk * wt_ref[b, k0:k1, :]

    extra_a = _spectrum_of(plan, a_scr, xbf_scr, mats, tw_ref, store_real)
    extra_d = _spectrum_of(plan, d_scr, xbf_scr, mats, tw_ref, store_imag)
    for e, w in enumerate(plan["w_extras"]):
        hx_ref[0, e] = extra_a.c[e] * w
        hx_ref[1, e] = -extra_d.s[e] * w


def _block_diag2(w):
    z = jnp.zeros_like(w)
    return jnp.concatenate([jnp.concatenate([w, z], axis=1), jnp.concatenate([z, w], axis=1)], axis=0)


def _hyena_consts(L):
    plan = _hy_plan(L)
    mats = tuple(jnp.asarray(a).astype(BF16) for a in plan["base"])
    return mats, jnp.asarray(plan["tw_table"]), jnp.asarray(plan["w_table"])


def _hyena_filter(L, w1, b1, w2, b2, w3, sin_freq, consts):
    plan = _hy_plan(L)
    m, nb, ne = plan["m"], plan["n_blocks"], plan["n_extras"]
    mats, tw, wt = consts
    z, window = _filter_features(L)
    z2 = np.concatenate([z[:L // 2], z[L // 2:]], axis=1)
    w1d = _block_diag2(jnp.pad(w1, ((0, HY_EMB_PAD - HY_EMB), (0, 0))))
    w2d = _block_diag2(w2)
    w3d = jnp.stack([_block_diag2(w3[:, o * 2 * HY_W:(o + 1) * 2 * HY_W]) for o in range(2)])
    pair = lambda v: jnp.tile(v, (1, 2))
    full = lambda *shape: _resident(shape, lambda o: (0,) * len(shape))
    spectrum = pl.BlockSpec((None, nb, m, HY_W), lambda o: (o, 0, 0, 0))
    return pl.pallas_call(
        functools.partial(_filter_kernel, L=L),
        grid=(2,),
        in_specs=[
            full(L // 2, 2 * HY_EMB_PAD), full(L, HY_W), full(2 * HY_EMB_PAD, 2 * HY_FFN), full(1, 2 * HY_FFN),
            full(2 * HY_FFN, 2 * HY_FFN), full(1, 2 * HY_FFN),
            pl.BlockSpec((None, 2 * HY_FFN, 4 * HY_W), lambda o: (o, 0, 0)),
            full(2, 2 * HY_FFN), *[full(m, m) for _ in mats], full(*tw.shape), full(*wt.shape),
        ],
        out_specs=[spectrum, spectrum, pl.BlockSpec((None, 2, ne, 1, HY_W), lambda o: (o, 0, 0, 0, 0))],
        out_shape=[jax.ShapeDtypeStruct((2, nb, m, HY_W), F32), jax.ShapeDtypeStruct((2, nb, m, HY_W), F32),
                   jax.ShapeDtypeStruct((2, 2, ne, 1, HY_W), F32)],
        scratch_shapes=[pltpu.VMEM((L // 2, 2 * HY_FFN), F32), pltpu.VMEM((HY_W // 128, L, 128), F32),
                        pltpu.VMEM((HY_W // 128, L, 128), F32), pltpu.VMEM((nb, m, HY_W), BF16)],
        compiler_params=_cparams("arbitrary"),
        name="hyena_filter",
    )(jnp.asarray(z2), jnp.asarray(window), w1d, pair(b1[None, :]), w2d, pair(b2[None, :]), w3d, pair(sin_freq),
      *mats, tw, wt)


def _hyena_kernel(v_ref, x1_ref, x2_ref, g_ref, cw_ref, cb_ref, skip_ref, *refs, L):
    plan = _hy_plan(L)
    depth = plan["depth"]
    n_mats = len(plan["base"])
    mats = refs[:n_mats]
    tw_ref, hr_ref, hi_ref, hx_ref, o_ref, a_scr, x_scr, y_scr, xbf_scr, gc_scr, gs_scr = refs[n_mats:]
    ck = min(HY_ROWS, L)
    halo = 16
    chunks = [(t0, t0 + ck) for t0 in range(0, L, ck)]

    def split_store(scr, t0, t1, val):
        for h, part in enumerate(_lane_halves(val)):
            scr[h, t0:t1, :] = part

    def joined(scr, t0, t1):
        return jnp.concatenate([scr[h, t0:t1, :] for h in range(HY_W // 128)], axis=1)

    def short_conv(ref, j, t0, t1):
        lo, hi = max(t0 - halo, 0), min(t1 + halo, L)
        u = ref[lo:hi, :].astype(F32)
        row = lax.broadcasted_iota(jnp.int32, (hi - lo, HY_W), 0)
        prev = pltpu.roll(u, 1, 0)
        nxt = pltpu.roll(u, hi - lo - 1, 0)
        if lo == 0:
            prev = jnp.where(row == 0, 0.0, prev)
        if hi == L:
            nxt = jnp.where(row == hi - lo - 1, 0.0, nxt)
        w = cw_ref[:, j * HY_W:(j + 1) * HY_W]
        z = prev * w[0:1, :] + u * w[1:2, :] + nxt * w[2:3, :] + cb_ref[:, j * HY_W:(j + 1) * HY_W]
        return z[t0 - lo:t1 - lo, :]

    def long_conv(order):
        def product(spec, hr, hi):
            return _Spec([c * r + s * i for c, s, r, i in zip(spec.c, spec.s, hr, hi)],
                         [s * r - c * i for c, s, r, i in zip(spec.c, spec.s, hr, hi)])

        def consume(k0, k1, spec):
            nb = len(spec.c)
            z = product(spec, [hr_ref[order, b, k0:k1, :] for b in range(nb)],
                        [hi_ref[order, b, k0:k1, :] for b in range(nb)])
            for i, leaf in enumerate(_backward(depth, z, _block_twiddles(plan, tw_ref, k0, k1))):
                gc_scr[i, k0:k1, :] = leaf.c[0].astype(BF16)
                gs_scr[i, k0:k1, :] = leaf.s[0].astype(BF16)

        extras = _spectrum_of(plan, a_scr, xbf_scr, mats, tw_ref, consume)
        ne = len(extras.c)
        zx = product(extras, [hx_ref[order, 0, e] for e in range(ne)], [hx_ref[order, 1, e] for e in range(ne)])
        tops = [leaf.c[0] for leaf in _backward(depth, zx, plan["tw_extras"])]
        _sequence_of(plan, gc_scr, gs_scr, tops, mats, y_scr)

    for t0, t1 in chunks:
        split_store(a_scr, t0, t1, short_conv(v_ref, 0, t0, t1))
        x_scr[t0:t1, :] = short_conv(x1_ref, 1, t0, t1)
    long_conv(0)
    for t0, t1 in chunks:
        conv = joined(y_scr, t0, t1) + joined(a_scr, t0, t1) * skip_ref[0:1, :]
        split_store(a_scr, t0, t1, x_scr[t0:t1, :] * conv)
    long_conv(1)
    for t0, t1 in chunks:
        conv = joined(y_scr, t0, t1) + joined(a_scr, t0, t1) * skip_ref[1:2, :]
        y = short_conv(x2_ref, 2, t0, t1) * conv
        o_ref[t0:t1, :] = (y * _silu(g_ref[t0:t1, :].astype(F32))).astype(o_ref.dtype)


def _hyena(u, conv_w, conv_b, skip, consts, hr, hi, hx, B, L):
    plan = _hy_plan(L)
    m, nb, ne = plan["m"], plan["n_blocks"], plan["n_extras"]
    mats, tw, _ = consts
    col = lambda j: pl.BlockSpec((L, CB), lambda b: (b, j))
    const = lambda *shape: _resident(shape, lambda b: (0,) * len(shape))
    halves = HY_W // 128
    return pl.pallas_call(
        functools.partial(_hyena_kernel, L=L),
        grid=(B,),
        in_specs=[
            col(CB_HY_V), col(CB_HY_X1), col(CB_HY_X2), col(CB_HY_G),
            const(3, 3 * HY_W), const(1, 3 * HY_W), const(2, HY_W),
            *[const(m, m) for _ in mats], const(*tw.shape),
            const(2, nb, m, HY_W), const(2, nb, m, HY_W), const(2, 2, ne, 1, HY_W),
        ],
        out_specs=pl.BlockSpec((L, HY_W), lambda b: (b, 0)),
        out_shape=jax.ShapeDtypeStruct((B * L, HY_W), BF16),
        scratch_shapes=[
            pltpu.VMEM((halves, L, 128), F32), pltpu.VMEM((L, HY_W), F32), pltpu.VMEM((halves, L, 128), F32),
            pltpu.VMEM((nb, m, HY_W), BF16), pltpu.VMEM((nb, m, HY_W), BF16), pltpu.VMEM((nb, m, HY_W), BF16),
        ],
        compiler_params=_cparams("parallel"),
        name="hyena",
    )(u, u, u, u, conv_w, conv_b[None, :], skip, *mats, tw, hr, hi, hx)


def _ret_kernel(*refs, L, Lc, has_init):
    if has_init:
        (q_ref, k_ref, v_ref, g_ref, kc_ref, vc_ref, cos_ref, sin_ref, rate_ref, rrow_ref, ones_ref,
         o_ref, q_scr, k_scr, sf_scr, sb_scr, r_scr) = refs
    else:
        (q_ref, k_ref, v_ref, g_ref, cos_ref, sin_ref, rate_ref, rrow_ref, ones_ref,
         o_ref, q_scr, k_scr, sf_scr, sb_scr, r_scr) = refs
    C = min(RET_CHUNK, L)
    nch = L // C
    W = RET_W
    quarter = RET_HEAD_DIM // 4
    block = ones_ref[...]
    lg = -jnp.exp(rate_ref[...])
    lg_f, lg_b = lg[0:1, :], lg[1:2, :]

    lane = lax.broadcasted_iota(jnp.int32, (L, W), 1)
    first_quarter = (lane % (2 * quarter)) < quarter

    def rope(a):
        swapped = jnp.where(first_quarter, pltpu.roll(a, W - quarter, 1), pltpu.roll(a, quarter, 1))
        return a * cos_ref[...] + swapped * sin_ref[...]

    if has_init:
        q_scr[...] = rope(q_ref[...].astype(F32)).astype(BF16)
        k_scr[...] = rope(k_ref[...].astype(F32) * (RET_HEAD_DIM ** -0.5)).astype(BF16)
    else:
        q_scr[...] = q_ref[...]
        k_scr[...] = (k_ref[...].astype(F32) * (RET_HEAD_DIM ** -0.5)).astype(BF16)

    def decays(n_rows):
        pos = lax.broadcasted_iota(jnp.int32, (n_rows, W), 0).astype(F32)
        return (jnp.exp(lg_f * (n_rows - 1.0 - pos)),
                jnp.exp(lg_b * pos))

    def chunk_states(k_bf, v_bf, zf, zb):
        kf = (k_bf.astype(F32) * zf).astype(BF16)
        kb = (k_bf.astype(F32) * zb).astype(BF16)
        return _dot_tn(kf, v_bf) * block, _dot_tn(kb, v_bf) * block

    zeta_f, zeta_b = decays(C)
    if has_init:
        zc_f, zc_b = decays(Lc)
        kc = (kc_ref[...].astype(F32) * (RET_HEAD_DIM ** -0.5)).astype(BF16)
        s0_f, s0_b = chunk_states(kc, vc_ref[...], zc_f, zc_b)
    else:
        s0_f = jnp.zeros((W, W), F32)
        s0_b = jnp.zeros((W, W), F32)

    for n in range(nch):
        kv_f, kv_b = chunk_states(k_scr[n * C:(n + 1) * C, :], v_ref[n * C:(n + 1) * C, :], zeta_f, zeta_b)
        sf_scr[n] = kv_f
        sb_scr[n] = kv_b
    dec_f = jnp.exp(lg_f * float(C))
    dec_b = jnp.exp(lg_b * float(C))
    state = s0_f
    for n in range(nch):
        kv = sf_scr[n]
        sf_scr[n] = state
        state = dec_f * state + kv
    state = s0_b
    for n in range(nch - 1, -1, -1):
        kv = sb_scr[n]
        sb_scr[n] = state
        state = dec_b * state + kv

    posc = lax.broadcasted_iota(jnp.int32, (C, W), 0).astype(F32)
    xi_f = jnp.exp(lg_f * (posc + 1.0))
    xi_b = jnp.exp(lg_b * (float(C) - posc))
    diff = (lax.broadcasted_iota(jnp.int32, (C, C), 0) - lax.broadcasted_iota(jnp.int32, (C, C), 1)).astype(F32)
    lane_c = lax.broadcasted_iota(jnp.int32, (1, W), 1)
    dmask = []
    for h in range(RET_HEADS):
        rf = -jnp.exp(rrow_ref[h:h + 1, 0:C])
        rb = -jnp.exp(rrow_ref[RET_HEADS + h:RET_HEADS + h + 1, 0:C])
        dmask.append(jnp.where(diff >= 0, jnp.exp(rf * jnp.maximum(diff, 0.0)), 0.0)
                     + jnp.where(diff <= 0, jnp.exp(rb * jnp.maximum(-diff, 0.0)), 0.0))

    for n in range(nch):
        qn = q_scr[n * C:(n + 1) * C, :]
        kn = k_scr[n * C:(n + 1) * C, :]
        vn = v_ref[n * C:(n + 1) * C, :]
        qf = qn.astype(F32)
        lhs = [(qf * xi_f).astype(BF16), (qf * xi_b).astype(BF16)]
        rhs = [sf_scr[n].astype(BF16), sb_scr[n].astype(BF16)]
        zero = jnp.zeros_like(qn)
        for h in range(RET_HEADS):
            sel = (lane_c // RET_HEAD_DIM) == h
            s = _dot_nt(jnp.where(sel, qn, zero), kn) * dmask[h]
            lhs.append(s.astype(BF16))
            rhs.append(jnp.where(sel, vn, zero))
        r_scr[n * C:(n + 1) * C, :] = _dot(jnp.concatenate(lhs, axis=1), jnp.concatenate(rhs, axis=0))

    r = r_scr[...]
    ss = _dot((r * r).astype(BF16), block.astype(BF16))
    rn = r * lax.rsqrt(ss * (1.0 / RET_HEAD_DIM) + EPS)
    o_ref[...] = (rn * _silu(g_ref[...].astype(F32))).astype(o_ref.dtype)


def _retention(u, uc, ret_log_rate, B, L, Lc, has_init):
    C = min(RET_CHUNK, L)
    nch = L // C
    col = lambda j: pl.BlockSpec((L, CB), lambda b: (b, j))
    ccol = lambda j: pl.BlockSpec((Lc, CB), lambda b: (b, j))
    cos, sin = _rope_tables(L)
    rate_lane = jnp.repeat(ret_log_rate, RET_HEAD_DIM, axis=1)
    rate_row = jnp.broadcast_to(ret_log_rate.reshape(2 * RET_HEADS, 1), (2 * RET_HEADS, RET_W))
    ones = jnp.asarray(_head_block_ones(RET_W))
    in_specs = [col(CB_RE_Q), col(CB_RE_K), col(CB_RE_V), col(CB_RE_G)]
    args = [u, u, u, u]
    if has_init:
        in_specs += [ccol(CB_RE_K), ccol(CB_RE_V)]
        args += [uc, uc]
    in_specs += [
        _resident((L, RET_W), lambda b: (0, 0)),
        _resident((L, RET_W), lambda b: (0, 0)),
        _resident((2, RET_W), lambda b: (0, 0)),
        _resident((2 * RET_HEADS, RET_W), lambda b: (0, 0)),
        _resident((RET_W, RET_W), lambda b: (0, 0)),
    ]
    args += [jnp.asarray(cos), jnp.asarray(sin), rate_lane, rate_row, ones]
    return pl.pallas_call(
        functools.partial(_ret_kernel, L=L, Lc=Lc, has_init=has_init),
        grid=(B,),
        in_specs=in_specs,
        out_specs=pl.BlockSpec((L, RET_W), lambda b: (b, 0)),
        out_shape=jax.ShapeDtypeStruct((B * L, RET_W), BF16),
        scratch_shapes=[
            pltpu.VMEM((L, RET_W), BF16), pltpu.VMEM((L, RET_W), BF16),
            pltpu.VMEM((nch, RET_W, RET_W), F32), pltpu.VMEM((nch, RET_W, RET_W), F32),
            pltpu.VMEM((L, RET_W), F32),
        ],
        compiler_params=_cparams("parallel"),
        name="retention",
    )(*args)


def kernel(x, c, ctx, c_ctx, norm_w, ada_w, ada_b, w_in, w_out, na_q_gain, na_k_gain, na_rpb, hy_conv_w, hy_conv_b,
           hy_w1, hy_b1, hy_w2, hy_b2, hy_w3, hy_sin_freq, hy_skip, ret_log_rate):
    B, L, D = x.shape
    Lc = ctx.shape[1]
    assert D == D_MODEL and L % (GRID_W * NA_G * NA_GPS) == 0 and L // GRID_W >= NA_WIN
    assert NA_G == NA_KT // GRID_W and (L // GRID_W - NA_WIN) % NA_G == 0 and Lc == NA_KT

    n_cond = 16
    cc = jnp.concatenate([c, c_ctx[None, :], jnp.zeros((n_cond - B - 1, D), F32)], axis=0)
    mods = _modulation(cc, ada_w, ada_b)

    dft_x = _hyena_consts(L)
    dft_c = _hyena_consts(Lc)

    tm_x = 1024
    xf = x.reshape(B * L, D)
    cf = ctx.reshape(B * Lc, D)
    x_mod = lambda i: i // (L // tm_x)
    c_mod = lambda i: B

    w_in_bf_all = w_in.astype(BF16)
    w_out_bf_all = w_out.astype(BF16)

    def layer_params(i):
        return mods[i].reshape(n_cond, 1, 3 * D), norm_w[i][None, :], (w_in_bf_all, i)

    mod, nw, w_in_bf = layer_params(0)
    u = _projection(xf, x_mod, tm_x, in_args=(mod, nw, w_in_bf))
    uc = _projection(cf, c_mod, Lc, in_args=(mod, nw, w_in_bf))
    for i in range(DEPTH):
        w_out_bf = (w_out_bf_all, i)
        filt = (hy_w1[i], hy_b1[i], hy_w2[i], hy_b2[i], hy_w3[i], hy_sin_freq[i])
        last = i == DEPTH - 1
        nxt = None if last else layer_params(i + 1)

        za = _neighborhood_attention(u, uc, na_q_gain[i], na_k_gain[i], na_rpb[i], B, L, Lc)
        hr, hi, hm = _hyena_filter(L, *filt, dft_x)
        zy = _hyena(u, hy_conv_w[i], hy_conv_b[i], hy_skip[i], dft_x, hr, hi, hm, B, L)
        zr = _retention(u, uc, ret_log_rate[i], B, L, Lc, True)
        if not last:
            zac = _context_attention(uc, na_q_gain[i], na_k_gain[i], B, Lc)
            hrc, hic, hmc = _hyena_filter(Lc, *filt, dft_c)
            zyc = _hyena(uc, hy_conv_w[i], hy_conv_b[i], hy_skip[i], dft_c, hrc, hic, hmc, B, Lc)
            zrc = _retention(uc, uc, ret_log_rate[i], B, Lc, Lc, False)
            cf, uc = _projection(cf, c_mod, Lc, out_args=(zac, zyc, zrc, mod, w_out_bf), in_args=nxt)
            xf, u = _projection(xf, x_mod, tm_x, out_args=(za, zy, zr, mod, w_out_bf), in_args=nxt)
            mod = nxt[0]
        else:
            tm_last = 2 * tm_x
            xf = _projection(xf, lambda i: i // (L // tm_last), tm_last, out_args=(za, zy, zr, mod, w_out_bf))

    return xf.reshape(B, L, D)
```

```python
import functools
import math

import numpy as np
import jax
import jax.numpy as jnp
from jax import lax
from jax.experimental import pallas as pl
from jax.experimental.pallas import tpu as pltpu

F32 = jnp.float32
BF16 = jnp.bfloat16

D_MODEL = 1024
DEPTH = 2
GRID_W = 64
NA_HEADS = 8
NA_HEAD_DIM = 64
NA_W = NA_HEADS * NA_HEAD_DIM
NA_KH = 8
NA_KW = 16
HY_W = 256
HY_BANDS = 8
HY_EMB = 1 + 2 * HY_BANDS
HY_EMB_PAD = 32
HY_FFN = 64
HY_FAST_DECAY = 0.3
HY_SLOW_DECAY = 1.5
HY_TARGET = 1e-2
RET_HEADS = 4
RET_HEAD_DIM = 64
RET_W = RET_HEADS * RET_HEAD_DIM
ROPE_BASE = 10000.0
EPS = 1e-6
NEG_INF = -1e30
LOG2E = 1.4426950408889634
IN_W = 4 * NA_W + 4 * HY_W + 4 * RET_W
MIX_W = NA_W + HY_W + RET_W

CB = 256
CB_HY_V, CB_HY_X1, CB_HY_X2, CB_HY_G = 8, 9, 10, 11
CB_RE_Q, CB_RE_K, CB_RE_V, CB_RE_G = 12, 13, 14, 15

NA_G = 4
NA_WIN = NA_G + NA_KH
NORM_W = 256
NA_GPS = 4
NA_KT = 256
NA_DR_MIN = (NA_KH - 1) - NA_KH - (NA_G - 1)
NA_DR_NUM = (NA_WIN - 1) + (NA_KH - 1) - NA_DR_MIN + 1
RET_CHUNK = 256
HY_ROWS = 512
HY_BASE = 512
VMEM_LIMIT = 56 * 1024 * 1024


def _cparams(*sem):
    return pltpu.CompilerParams(dimension_semantics=sem, vmem_limit_bytes=VMEM_LIMIT)


def _resident(shape, index_map):
    return pl.BlockSpec(shape, index_map, pipeline_mode=pl.Buffered(1))


def _silu(x):
    return x * (1.0 / (1.0 + jnp.exp(-x)))


def _dot(a, b):
    return jnp.dot(a, b, preferred_element_type=F32)


def _dot_nt(a, b):
    return lax.dot_general(a, b, (((1,), (1,)), ((), ())), preferred_element_type=F32)


def _dot_tn(a, b):
    return lax.dot_general(a, b, (((0,), (0,)), ((), ())), preferred_element_type=F32)


@functools.lru_cache(maxsize=None)
def _bin_map(n, depth):
    if depth == 0:
        return [np.arange(n)], [n]
    hb, he = _bin_map(n // 2, depth - 1)
    blocks = [b for b in hb] + [n - b for b in hb]
    extras = list(he) + [n - e for e in he if e < n // 2]
    return blocks, extras


@functools.lru_cache(maxsize=None)
def _hy_plan(L):
    depth = 0
    while L >> depth > HY_BASE:
        depth += 1
    m = L >> depth
    k = np.arange(m, dtype=np.int64)[:, None] * np.arange(m, dtype=np.int64)[None, :]
    ang = (k % (2 * m)).astype(np.float64) * (math.pi / m)
    base = [np.cos(ang), np.sin(ang)]
    if depth:
        odd = (np.arange(m, dtype=np.int64)[:, None] * (2 * np.arange(m, dtype=np.int64)[None, :] + 1)) % (4 * m)
        odd = odd.astype(np.float64) * (math.pi / (2 * m))
        base += [np.cos(odd), np.sin(odd), np.cos(odd).T, np.sin(odd).T]
    base = tuple(np.ascontiguousarray(a).astype(np.float32) for a in base)
    tw_blocks, tw_extras = [], []
    for level in range(1, depth + 1):
        n = m << level
        hb, he = _bin_map(n // 2, level - 1)
        tw_blocks.append([(np.cos(math.pi * b / n), np.sin(math.pi * b / n)) for b in hb])
        tw_extras.append([(math.cos(math.pi * e / n), math.sin(math.pi * e / n), e == n // 2) for e in he])
    blocks, extras = _bin_map(L, depth)
    inv_n = 1.0 / (2 * L)
    seen = set()

    def weight(b):
        w = 0.0 if b in seen else (inv_n if b in (0, L) else 2.0 * inv_n)
        seen.add(b)
        return w

    w_blocks = [np.array([weight(int(b)) for b in blk]) for blk in blocks]
    w_extras = [weight(int(e)) for e in extras]
    tables = [np.broadcast_to(t[:, None], (m, HY_W)) for lvl in tw_blocks for pair in lvl for t in pair]
    tw_table = np.stack(tables).astype(np.float32) if tables else np.zeros((1, 8, HY_W), np.float32)
    w_table = np.stack([np.broadcast_to(w[:, None], (m, HY_W)) for w in w_blocks]).astype(np.float32)
    return dict(depth=depth, m=m, base=base, tw_extras=tw_extras, n_blocks=len(blocks), n_extras=len(extras),
                w_extras=w_extras, tw_table=tw_table, w_table=w_table,
                tw_index=[[2 * (sum(len(l) for l in tw_blocks[:lv]) + i) for i in range(len(tw_blocks[lv]))]
                          for lv in range(depth)])


@functools.lru_cache(maxsize=None)
def _filter_features(L):
    t = np.linspace(0.0, 1.0, L)[:, None]
    omega = 2.0 * math.pi * np.arange(L)[:, None] / L
    bands = np.linspace(1e-4, HY_BANDS - 1, HY_BANDS)[None, :]
    z = np.concatenate([t, np.cos(bands * omega), -np.sin(bands * omega)], axis=-1)
    z = np.pad(z, ((0, 0), (0, HY_EMB_PAD - HY_EMB)))
    deltas = np.abs(np.linspace(math.log(HY_TARGET) / HY_SLOW_DECAY, math.log(HY_TARGET) / HY_FAST_DECAY, HY_W))
    window = np.exp(-t * deltas[None, :])
    return z.astype(np.float32), window.astype(np.float32)


@functools.lru_cache(maxsize=None)
def _rope_tables(L):
    half = RET_HEAD_DIM // 2
    quarter = half // 2
    t = np.arange(L)
    pos = np.stack([t // GRID_W, t % GRID_W], axis=0).astype(np.float64)
    freqs = ROPE_BASE ** (-np.arange(quarter, dtype=np.float64) / quarter)
    cos = np.zeros((L, RET_HEAD_DIM))
    sin = np.zeros((L, RET_HEAD_DIM))
    for a in range(2):
        ang = pos[a][:, None] * freqs[None, :]
        base = a * half
        cos[:, base:base + quarter] = np.cos(ang)
        cos[:, base + quarter:base + half] = np.cos(ang)
        sin[:, base:base + quarter] = -np.sin(ang)
        sin[:, base + quarter:base + half] = np.sin(ang)
    return (np.tile(cos, (1, RET_HEADS)).astype(np.float32), np.tile(sin, (1, RET_HEADS)).astype(np.float32))


@functools.lru_cache(maxsize=None)
def _head_block_ones(width):
    i = np.arange(width) // NA_HEAD_DIM
    return (i[:, None] == i[None, :]).astype(np.float32)


def _mod_kernel(c_ref, w_ref, b_ref, o_ref):
    s = _silu(c_ref[...])
    w = w_ref[...]
    s_hi = s.astype(BF16)
    w_hi = w.astype(BF16)
    s_lo = (s - s_hi.astype(F32)).astype(BF16)
    w_lo = (w - w_hi.astype(F32)).astype(BF16)
    o_ref[...] = _dot(s_hi, w_hi) + (_dot(s_hi, w_lo) + _dot(s_lo, w_hi)) + b_ref[...]


def _modulation(cc, ada_w, ada_b):
    R = cc.shape[0]
    tn = 1024
    return pl.pallas_call(
        _mod_kernel,
        grid=(DEPTH, 3 * D_MODEL // tn),
        in_specs=[
            pl.BlockSpec((R, D_MODEL), lambda i, j: (0, 0)),
            pl.BlockSpec((None, D_MODEL, tn), lambda i, j: (i, 0, j)),
            pl.BlockSpec((None, 1, tn), lambda i, j: (i, 0, j)),
        ],
        out_specs=pl.BlockSpec((None, R, tn), lambda i, j: (i, 0, j)),
        out_shape=jax.ShapeDtypeStruct((DEPTH, R, 3 * D_MODEL), F32),
        compiler_params=_cparams("parallel", "parallel"),
        name="modulation",
    )(cc, ada_w, ada_b.reshape(DEPTH, 1, 3 * D_MODEL))


def _proj_kernel(*refs, with_out, with_in):
    refs = list(refs)
    if with_out:
        za_ref, zy_ref, zr_ref, x_ref, mod_o_ref, w_out_ref = refs[:6]
        del refs[:6]
    else:
        x_ref = refs.pop(0)
    if with_in:
        mod_i_ref, nw_ref, w_in_ref = refs[:3]
        del refs[:3]
    x = x_ref[...]
    if with_out:
        acc = _dot(za_ref[...], w_out_ref[0:NA_W, :])
        acc += _dot(zy_ref[...], w_out_ref[NA_W:NA_W + HY_W, :])
        acc += _dot(zr_ref[...], w_out_ref[NA_W + HY_W:MIX_W, :])
        x = x + mod_o_ref[:, 2 * D_MODEL:3 * D_MODEL] * acc
        refs.pop(0)[...] = x
    if with_in:
        u_ref = refs.pop(0)
        xn = x * lax.rsqrt(jnp.mean(x * x, axis=-1, keepdims=True) + EPS)
        shift = mod_i_ref[:, 0:D_MODEL]
        scale = mod_i_ref[:, D_MODEL:2 * D_MODEL]
        h = (xn * nw_ref[...] * (1.0 + scale) + shift).astype(BF16)
        tn = 1024
        for j in range(IN_W // tn):
            u_ref[:, j * tn:(j + 1) * tn] = _dot(h, w_in_ref[:, j * tn:(j + 1) * tn]).astype(u_ref.dtype)


def _projection(xf, mod_index, tm, out_args=None, in_args=None):
    R = xf.shape[0]
    row = lambda w: pl.BlockSpec((tm, w), lambda i: (i, 0))
    mod_spec = pl.BlockSpec((None, 1, 3 * D_MODEL), lambda i: (mod_index(i), 0, 0))
    args, in_specs, out_specs, out_shape = [], [], [], []
    if out_args is not None:
        za, zy, zr, mod_o, (w_out_bf, layer_o) = out_args
        args += [za, zy, zr, xf, mod_o, w_out_bf]
        in_specs += [row(NA_W), row(HY_W), row(RET_W), row(D_MODEL), mod_spec,
                     _resident((None, MIX_W, D_MODEL), lambda i: (layer_o, 0, 0))]
        out_specs.append(row(D_MODEL))
        out_shape.append(jax.ShapeDtypeStruct((R, D_MODEL), F32))
    else:
        args.append(xf)
        in_specs.append(row(D_MODEL))
    if in_args is not None:
        mod_i, norm_w, (w_in_bf, layer_i) = in_args
        args += [mod_i, norm_w, w_in_bf]
        in_specs += [mod_spec, _resident((1, D_MODEL), lambda i: (0, 0)),
                     _resident((None, D_MODEL, IN_W), lambda i: (layer_i, 0, 0))]
        out_specs.append(row(IN_W))
        out_shape.append(jax.ShapeDtypeStruct((R, IN_W), BF16))
    outs = pl.pallas_call(
        functools.partial(_proj_kernel, with_out=out_args is not None, with_in=in_args is not None),
        grid=(R // tm,),
        in_specs=in_specs,
        out_specs=out_specs,
        out_shape=out_shape,
        compiler_params=_cparams("parallel"),
        name="projection",
    )(*args)
    return outs if len(outs) > 1 else outs[0]


def _head_rms(x, ones_bf, gain):
    sq = (x * x).astype(BF16)
    ss = jnp.concatenate([_dot(sq[:, c:c + NORM_W], ones_bf) for c in range(0, x.shape[1], NORM_W)], axis=1)
    return x * lax.rsqrt(ss * (1.0 / NA_HEAD_DIM) + EPS) * gain


def _attend_heads(q, k, v):
    first = lax.broadcasted_iota(jnp.int32, (1, 2 * NA_HEAD_DIM), 1) < NA_HEAD_DIM
    outs = []
    for pair in range(NA_HEADS // 2):
        lanes = slice(pair * 2 * NA_HEAD_DIM, (pair + 1) * 2 * NA_HEAD_DIM)
        o_pair = None
        for sub in range(2):
            sel = first if sub == 0 else jnp.logical_not(first)
            s = _dot_nt(jnp.where(sel, q[:, lanes], 0.0).astype(BF16), k[:, lanes])
            p = jnp.exp(s - s.max(axis=-1, keepdims=True))
            o_h = _dot(p.astype(BF16), v[:, lanes]) * (1.0 / p.sum(axis=-1, keepdims=True))
            o_pair = o_h if o_pair is None else jnp.where(first, o_pair, o_h)
        outs.append(o_pair)
    return jnp.concatenate(outs, axis=-1)


def _na_kernel(q_ref, k_ref, v_ref, g_ref, kc_ref, vc_ref, qg_ref, kg_ref, ones_ref, rpb_ref, o_ref,
               kn_scr, kcn_scr, vat_scr, vbt_scr, vcat_scr, vcbt_scr, bias_scr, mask_scr, s0_scr, s1_scr,
               p0_scr, p1_scr, qm_scr, ot_scr, *, rows):
    batch = pl.program_id(0)
    grp = pl.program_id(1)
    ones_bf = ones_ref[...]
    tq = NA_G * GRID_W
    nwin = NA_WIN * GRID_W
    Lc = kc_ref.shape[0]
    pair_w = 2 * NA_HEAD_DIM
    kt = NA_KT
    first = lax.broadcasted_iota(jnp.int32, (1, pair_w), 1) < NA_HEAD_DIM

    @pl.when((batch == 0) & (grp == 0))
    def _():
        ck = lax.broadcasted_iota(jnp.int32, (GRID_W, pair_w), 0)
        lane = lax.broadcasted_iota(jnp.int32, (GRID_W, pair_w), 1)
        cq = lane % GRID_W
        col_start = jnp.clip(cq - NA_KW // 2, 0, GRID_W - NA_KW)
        col_ok = (ck >= col_start) & (ck < col_start + NA_KW)
        left = lane < GRID_W

        def body(i, carry):
            for h in range(NA_HEADS):
                v0 = jnp.broadcast_to(rpb_ref[i + 1, h:h + 1, :], (GRID_W, pair_w))
                v1 = jnp.broadcast_to(rpb_ref[i, h:h + 1, :], (GRID_W, pair_w))
                t0 = pltpu.roll(v0, pair_w - (NA_KW - 1), 1, stride=1, stride_axis=0)
                t1 = pltpu.roll(v1, GRID_W - (NA_KW - 1), 1, stride=1, stride_axis=0)
                bias_scr[i, h] = jnp.where(col_ok, jnp.where(left, t0, t1), NEG_INF)
            return carry

        lax.fori_loop(0, NA_DR_NUM, body, 0)

    @pl.when(grp == 0)
    def _():
        kgain = kg_ref[...]
        own_a = (lax.broadcasted_iota(jnp.int32, (NA_W, 1), 0) % pair_w) < NA_HEAD_DIM
        for src, dst, vat, vbt, vsrc in ((k_ref, kn_scr, vat_scr, vbt_scr, v_ref),
                                         (kc_ref, kcn_scr, vcat_scr, vcbt_scr, vc_ref)):
            for i in range(src.shape[0] // kt):
                sl = slice(i * kt, (i + 1) * kt)
                dst[sl, :] = _head_rms(src[sl, :].astype(F32), ones_bf, kgain).astype(BF16)
                vt = vsrc[sl, :].astype(F32).T
                vat[i] = jnp.where(own_a, vt, 1.0).astype(BF16)
                vbt[i] = jnp.where(own_a, 1.0, vt).astype(BF16)

    n_kt = (nwin + Lc) // kt
    rows_per_tile = kt // GRID_W
    n_ch = (nwin + Lc) // GRID_W
    n_items = NA_GPS * NA_HEADS

    s_bufs = (s0_scr, s1_scr)
    p_bufs = (p0_scr, p1_scr)
    row_zero = jnp.minimum(grp, 0)

    def rows_of(start, size):
        return pl.ds(pl.multiple_of(row_zero + start, GRID_W), size)

    def head_lanes(h):
        return slice((h // 2) * pair_w, (h // 2 + 1) * pair_w)

    key0, tile0, dr_base = [], [], []
    for gi in range(NA_GPS):
        r0 = (grp * NA_GPS + gi) * NA_G
        ws = jnp.clip(r0 - NA_KH // 2, 0, rows - NA_WIN)
        key0.append(pl.multiple_of(ws * GRID_W, kt))
        tile0.append(ws // rows_per_tile)
        dr_base.append(ws - r0 + (NA_KH - 1) - NA_DR_MIN)
        w_id = lax.broadcasted_iota(jnp.int32, (16, tq), 0)
        g_id = lax.broadcasted_iota(jnp.int32, (16, tq), 1) // GRID_W
        lo = jnp.clip(r0 + g_id - NA_KH // 2, 0, rows - NA_KH) - ws
        mask_scr[gi] = jnp.where((w_id >= lo) & (w_id < lo + NA_KH), 0.0, NEG_INF).astype(F32)
        q = _head_rms(q_ref[gi * tq:(gi + 1) * tq, :].astype(F32), ones_bf,
                      qg_ref[...] * (NA_HEAD_DIM ** -0.5 * LOG2E))
        for h in range(NA_HEADS):
            sel = first if h % 2 == 0 else jnp.logical_not(first)
            qm_scr[gi * NA_HEADS + h, rows_of(0, tq), :] = jnp.where(sel, q[:, head_lanes(h)], 0.0).astype(BF16)

    col_max = {}

    def qk_piece(n, j):
        gi, h = divmod(n, NA_HEADS)
        last = j == n_kt - 1
        keys = kcn_scr[:, head_lanes(h)] if last else kn_scr[pl.ds(key0[gi] + j * kt, kt), head_lanes(h)]
        sv = _dot_nt(keys, qm_scr[n, rows_of(0, tq), :])
        for dw in range(rows_per_tile):
            sw = sv[dw * GRID_W:(dw + 1) * GRID_W, :]
            if not last:
                w = j * rows_per_tile + dw
                bias = jnp.concatenate([bias_scr[dr_base[gi] + (w - g), h] for g in range(0, NA_G, 2)], axis=-1)
                sw = sw + bias + mask_scr[gi, w:w + 1, :]
            s_bufs[n % 2][rows_of(j * kt + dw * GRID_W, GRID_W), :] = sw
            part = sw.reshape(GRID_W // 8, 8, tq).max(axis=0)
            col_max[n] = part if (j == 0 and dw == 0) else jnp.maximum(col_max[n], part)

    def exp_chunk(n, c):
        if c == 0:
            col_max[n] = col_max[n].max(axis=0, keepdims=True)
        rows_c = rows_of(c * GRID_W, GRID_W)
        p_bufs[n % 2][rows_c, :] = jnp.exp2(s_bufs[n % 2][rows_c, :] - col_max[n]).astype(BF16)
        if c == n_ch - 1:
            del col_max[n]

    def pv(n):
        gi, h = divmod(n, NA_HEADS)
        vt, vct = (vat_scr, vcat_scr) if h % 2 == 0 else (vbt_scr, vcbt_scr)
        ot = _dot(vct[0, head_lanes(h), :], p_bufs[n % 2][rows_of(nwin, Lc), :])
        for j in range(n_kt - 1):
            ot += _dot(vt[tile0[gi] + j, head_lanes(h), :], p_bufs[n % 2][rows_of(j * kt, kt), :])
        own, den = (ot[0:NA_HEAD_DIM], ot[NA_HEAD_DIM:]) if h % 2 == 0 else (ot[NA_HEAD_DIM:], ot[0:NA_HEAD_DIM])
        ot_scr[rows_of(n * NA_HEAD_DIM, NA_HEAD_DIM), :] = own * (1.0 / den)
        if h % 2 == 1:
            o_pair = ot_scr[rows_of((n - 1) * NA_HEAD_DIM, pair_w), :].T
            q_rows = slice(gi * tq, (gi + 1) * tq)
            gate = _silu(g_ref[q_rows, head_lanes(h)].astype(F32))
            o_ref[q_rows, head_lanes(h)] = (o_pair * gate).astype(o_ref.dtype)

    for t in range(n_items + 2):
        for i in range(n_ch):
            if 0 <= t - 1 < n_items:
                exp_chunk(t - 1, i)
            if i % 4 == 0 and t < n_items and i // 4 < n_kt:
                qk_piece(t, i // 4)
            if i == 2 and 0 <= t - 2:
                pv(t - 2)


def _neighborhood_attention(u, uc, q_gain, k_gain, rpb, B, L, Lc):
    rows = L // GRID_W
    ngrp = rows // (NA_G * NA_GPS)
    tq = NA_G * GRID_W
    tstep = tq * NA_GPS
    ones = jnp.asarray(_head_block_ones(NORM_W)).astype(BF16)
    qg = jnp.tile(q_gain, NA_HEADS)[None, :]
    kg = jnp.tile(k_gain, NA_HEADS)[None, :]
    n_dr = 2 * NA_KH - 1
    rpb_t = jnp.transpose(rpb.astype(F32)[:, :, ::-1] * LOG2E, (1, 0, 2))
    lo_pad = 1 - NA_DR_MIN
    rpb_t = jnp.pad(rpb_t, ((lo_pad, NA_DR_NUM + 1 - lo_pad - n_dr), (0, 0), (0, 2 * GRID_W - (2 * NA_KW - 1))))
    return pl.pallas_call(
        functools.partial(_na_kernel, rows=rows),
        grid=(B, ngrp),
        in_specs=[
            pl.BlockSpec((tstep, NA_W), lambda b, g: (b * ngrp + g, 0)),
            pl.BlockSpec((L, NA_W), lambda b, g: (b, 1)),
            pl.BlockSpec((L, NA_W), lambda b, g: (b, 2)),
            pl.BlockSpec((tstep, NA_W), lambda b, g: (b * ngrp + g, 3)),
            pl.BlockSpec((Lc, NA_W), lambda b, g: (b, 1)),
            pl.BlockSpec((Lc, NA_W), lambda b, g: (b, 2)),
            _resident((1, NA_W), lambda b, g: (0, 0)),
            _resident((1, NA_W), lambda b, g: (0, 0)),
            _resident((NORM_W, NORM_W), lambda b, g: (0, 0)),
            _resident((NA_DR_NUM + 1, NA_HEADS, 2 * GRID_W), lambda b, g: (0, 0, 0)),
        ],
        out_specs=pl.BlockSpec((tstep, NA_W), lambda b, g: (b * ngrp + g, 0)),
        out_shape=jax.ShapeDtypeStruct((B * L, NA_W), BF16),
        scratch_shapes=[
            pltpu.VMEM((L, NA_W), BF16), pltpu.VMEM((Lc, NA_W), BF16),
            pltpu.VMEM((L // NA_KT, NA_W, NA_KT), BF16), pltpu.VMEM((L // NA_KT, NA_W, NA_KT), BF16),
            pltpu.VMEM((Lc // NA_KT, NA_W, NA_KT), BF16), pltpu.VMEM((Lc // NA_KT, NA_W, NA_KT), BF16),
            pltpu.VMEM((NA_DR_NUM, NA_HEADS, GRID_W, 2 * GRID_W), F32),
            pltpu.VMEM((NA_GPS, 16, tq), F32),
            pltpu.VMEM((NA_WIN * GRID_W + Lc, tq), F32), pltpu.VMEM((NA_WIN * GRID_W + Lc, tq), F32),
            pltpu.VMEM((NA_WIN * GRID_W + Lc, tq), BF16), pltpu.VMEM((NA_WIN * GRID_W + Lc, tq), BF16),
            pltpu.VMEM((NA_GPS * NA_HEADS, tq, 2 * NA_HEAD_DIM), BF16), pltpu.VMEM((NA_GPS * NA_W, tq), F32),
        ],
        compiler_params=_cparams("arbitrary", "arbitrary"),
        name="neighborhood_attention",
    )(u, u, u, u, uc, uc, qg, kg, ones, rpb_t)


def _ctx_attn_kernel(q_ref, k_ref, v_ref, g_ref, qg_ref, kg_ref, ones_ref, o_ref):
    ones_bf = ones_ref[...]
    q = _head_rms(q_ref[...].astype(F32), ones_bf, qg_ref[...] * (NA_HEAD_DIM ** -0.5))
    k = _head_rms(k_ref[...].astype(F32), ones_bf, kg_ref[...]).astype(BF16)
    o = _attend_heads(q, k, v_ref[...])
    o_ref[...] = (o * _silu(g_ref[...].astype(F32))).astype(o_ref.dtype)


def _context_attention(uc, q_gain, k_gain, B, Lc):
    ones = jnp.asarray(_head_block_ones(NORM_W)).astype(BF16)
    qg = jnp.tile(q_gain, NA_HEADS)[None, :]
    kg = jnp.tile(k_gain, NA_HEADS)[None, :]
    return pl.pallas_call(
        _ctx_attn_kernel,
        grid=(B,),
        in_specs=[
            pl.BlockSpec((Lc, NA_W), lambda b: (b, 0)),
            pl.BlockSpec((Lc, NA_W), lambda b: (b, 1)),
            pl.BlockSpec((Lc, NA_W), lambda b: (b, 2)),
            pl.BlockSpec((Lc, NA_W), lambda b: (b, 3)),
            _resident((1, NA_W), lambda b: (0, 0)),
            _resident((1, NA_W), lambda b: (0, 0)),
            _resident((NORM_W, NORM_W), lambda b: (0, 0)),
        ],
        out_specs=pl.BlockSpec((Lc, NA_W), lambda b: (b, 0)),
        out_shape=jax.ShapeDtypeStruct((B * Lc, NA_W), BF16),
        compiler_params=_cparams("parallel"),
        name="context_attention",
    )(uc, uc, uc, uc, qg, kg, ones)


def _alt_sign(shape):
    row = lax.broadcasted_iota(jnp.int32, shape, 0)
    return (1 - 2 * (row & 1)).astype(F32)


def _lane_halves(x):
    return [x[:, h * 128:(h + 1) * 128] for h in range(HY_W // 128)]


class _Spec:
    def __init__(self, c, s):
        self.c, self.s = list(c), list(s)


def _combine(P, R, tw):
    lo_c, lo_s, mi_c, mi_s = [], [], [], []
    for (pc, ps, rc, rs, (ct, st, single)) in zip(P.c, P.s, R.c, R.s, tw):
        if single:
            lo_c.append(pc)
            lo_s.append(rc)
            continue
        qc, qs = (rc, rs) if ct is None else (ct * rc - st * rs, st * rc + ct * rs)
        lo_c.append(pc + qc)
        lo_s.append(ps + qs)
        mi_c.append(pc - qc)
        mi_s.append(qs - ps)
    return _Spec(lo_c + mi_c, lo_s + mi_s)


def _combine_t(Z, tw):
    n_lo = len(tw)
    lo_c, lo_s, mi_c, mi_s = Z.c[:n_lo], Z.s[:n_lo], Z.c[n_lo:], Z.s[n_lo:]
    pc, ps, rc, rs = [], [], [], []
    mi = 0
    for i, (ct, st, single) in enumerate(tw):
        if single:
            pc.append(lo_c[i])
            ps.append(jnp.zeros_like(lo_c[i]))
            rc.append(lo_s[i])
            rs.append(jnp.zeros_like(lo_c[i]))
            continue
        g_qc = lo_c[i] - mi_c[mi]
        g_qs = lo_s[i] + mi_s[mi]
        pc.append(lo_c[i] + mi_c[mi])
        ps.append(lo_s[i] - mi_s[mi])
        rc.append(g_qc if ct is None else ct * g_qc + st * g_qs)
        rs.append(g_qs if ct is None else ct * g_qs - st * g_qc)
        mi += 1
    return _Spec(pc, ps), _Spec(rc, rs)


def _forward(level, leaves, tws):
    if level == 0:
        return leaves[0]
    half = len(leaves) // 2
    return _combine(_forward(level - 1, leaves[:half], tws), _forward(level - 1, leaves[half:], tws), tws[level - 1])


def _backward(level, Z, tws):
    if level == 0:
        return [Z]
    P, R = _combine_t(Z, tws[level - 1])
    return _backward(level - 1, P, tws) + _backward(level - 1, R, tws)


def _leaf_samples(depth, off=0, stride=1):
    if depth == 0:
        return [(off, stride)]
    return _leaf_samples(depth - 1, off, 2 * stride) + _leaf_samples(depth - 1, off + stride, 2 * stride)


def _block_twiddles(plan, tw_ref, k0, k1):
    return [[(None, None, False) if lv == 0 else (tw_ref[i, k0:k1, :], tw_ref[i + 1, k0:k1, :], False)
             for i in plan["tw_index"][lv]] for lv in range(plan["depth"])]


def _leaf_matrices(plan, mats, i, transposed):
    if plan["depth"] == 0 or i % 2 == 0:
        return mats[0], mats[1]
    return (mats[4], mats[5]) if transposed else (mats[2], mats[3])


def _spectrum_of(plan, src_scr, xbf_scr, mats, tw_ref, consume):
    depth, m = plan["depth"], plan["m"]
    cm = min(HY_ROWS // 2, m)
    sign = _alt_sign((cm, HY_W))
    tops = []
    for i, (off, stride) in enumerate(_leaf_samples(depth)):
        top = jnp.zeros((1, HY_W), F32)
        for j0 in range(0, m, cm):
            x = jnp.concatenate([src_scr[h, pl.ds(off + stride * j0, cm, stride=stride), :]
                                 for h in range(HY_W // 128)], axis=1)
            xbf_scr[i, j0:j0 + cm, :] = x.astype(BF16)
            top += jnp.sum(x * sign, axis=0, keepdims=True)
        tops.append(top)
    for k0 in range(0, m, cm):
        k1 = k0 + cm
        base = []
        for i in range(len(tops)):
            c_ref, s_ref = _leaf_matrices(plan, mats, i, False)
            base.append(_Spec([_dot(c_ref[k0:k1, :], xbf_scr[i])], [_dot(s_ref[k0:k1, :], xbf_scr[i])]))
        consume(k0, k1, _forward(depth, base, _block_twiddles(plan, tw_ref, k0, k1)))
    zero = jnp.zeros((1, HY_W), F32)
    return _forward(depth, [_Spec([t], [zero]) for t in tops], plan["tw_extras"])


def _sequence_of(plan, gc_scr, gs_scr, tops, mats, dst_scr):
    depth, m = plan["depth"], plan["m"]
    cm = min(HY_ROWS // 2, m)
    sign = _alt_sign((cm, HY_W))
    for i, (off, stride) in enumerate(_leaf_samples(depth)):
        c_ref, s_ref = _leaf_matrices(plan, mats, i, True)
        for j0 in range(0, m, cm):
            y = _dot(c_ref[j0:j0 + cm, :], gc_scr[i]) + _dot(s_ref[j0:j0 + cm, :], gs_scr[i]) + sign * tops[i]
            for h in range(HY_W // 128):
                dst_scr[h, pl.ds(off + stride * j0, cm, stride=stride), :] = y[:, h * 128:(h + 1) * 128]


def _filter_kernel(z_ref, win_ref, w1_ref, b1_ref, w2_ref, b2_ref, w3_ref, sf_ref, *refs, L):
    plan = _hy_plan(L)
    n_mats = len(plan["base"])
    mats = refs[:n_mats]
    tw_ref, wt_ref, hr_ref, hi_ref, hx_ref, h_scr, a_scr, d_scr, xbf_scr = refs[n_mats:]
    hp = lax.Precision.HIGHEST
    dot = lambda a, b: jnp.dot(a, b, preferred_element_type=F32, precision=hp)

    @pl.when(pl.program_id(0) == 0)
    def _():
        h = jnp.sin(sf_ref[0:1, :] * (dot(z_ref[...], w1_ref[...]) + b1_ref[...]))
        h_scr[...] = jnp.sin(sf_ref[1:2, :] * (dot(h, w2_ref[...]) + b2_ref[...]))

    taps = dot(h_scr[...], w3_ref[...])
    win = win_ref[...]
    row = lax.broadcasted_iota(jnp.int32, (L, HY_W), 0)
    hf = jnp.concatenate([taps[:, 0:HY_W], taps[:, 2 * HY_W:3 * HY_W]], axis=0) * win
    hb = jnp.concatenate([taps[:, HY_W:2 * HY_W], taps[:, 3 * HY_W:4 * HY_W]], axis=0) * win
    hb = jnp.where(row == 0, 0.0, hb)
    norm = jnp.sum(jnp.abs(hf), axis=0, keepdims=True) + jnp.sum(jnp.abs(hb), axis=0, keepdims=True)
    inv = 1.0 / norm
    for h, (a_half, d_half) in enumerate(zip(_lane_halves((hf + hb) * inv), _lane_halves((hf - hb) * inv))):
        a_scr[h] = a_half
        d_scr[h] = d_half

    def store_real(k0, k1, spec):
        for b, blk in enumerate(spec.c):
            hr_ref[b, k0:k1, :] = blk * wt_ref[b, k0:k1, :]

    def store_imag(k0, k1, spec):
        for b, blk in enumerate(spec.s):
            hi_ref[b, k0:k1, :] = -blk * wt_ref[b, k0:k1, :]

    extra_a = _spectrum_of(plan, a_scr, xbf_scr, mats, tw_ref, store_real)
    extra_d = _spectrum_of(plan, d_scr, xbf_scr, mats, tw_ref, store_imag)
    for e, w in enumerate(plan["w_extras"]):
        hx_ref[0, e] = extra_a.c[e] * w
        hx_ref[1, e] = -extra_d.s[e] * w


def _block_diag2(w):
    z = jnp.zeros_like(w)
    return jnp.concatenate([jnp.concatenate([w, z], axis=1), jnp.concatenate([z, w], axis=1)], axis=0)


def _hyena_consts(L):
    plan = _hy_plan(L)
    mats = tuple(jnp.asarray(a).astype(BF16) for a in plan["base"])
    return mats, jnp.asarray(plan["tw_table"]), jnp.asarray(plan["w_table"])


def _hyena_filter(L, w1, b1, w2, b2, w3, sin_freq, consts):
    plan = _hy_plan(L)
    m, nb, ne = plan["m"], plan["n_blocks"], plan["n_extras"]
    mats, tw, wt = consts
    z, window = _filter_features(L)
    z2 = np.concatenate([z[:L // 2], z[L // 2:]], axis=1)
    w1d = _block_diag2(jnp.pad(w1, ((0, HY_EMB_PAD - HY_EMB), (0, 0))))
    w2d = _block_diag2(w2)
    w3d = jnp.stack([_block_diag2(w3[:, o * 2 * HY_W:(o + 1) * 2 * HY_W]) for o in range(2)])
    pair = lambda v: jnp.tile(v, (1, 2))
    full = lambda *shape: _resident(shape, lambda o: (0,) * len(shape))
    spectrum = pl.BlockSpec((None, nb, m, HY_W), lambda o: (o, 0, 0, 0))
    return pl.pallas_call(
        functools.partial(_filter_kernel, L=L),
        grid=(2,),
        in_specs=[
            full(L // 2, 2 * HY_EMB_PAD), full(L, HY_W), full(2 * HY_EMB_PAD, 2 * HY_FFN), full(1, 2 * HY_FFN),
            full(2 * HY_FFN, 2 * HY_FFN), full(1, 2 * HY_FFN),
            pl.BlockSpec((None, 2 * HY_FFN, 4 * HY_W), lambda o: (o, 0, 0)),
            full(2, 2 * HY_FFN), *[full(m, m) for _ in mats], full(*tw.shape), full(*wt.shape),
        ],
        out_specs=[spectrum, spectrum, pl.BlockSpec((None, 2, ne, 1, HY_W), lambda o: (o, 0, 0, 0, 0))],
        out_shape=[jax.ShapeDtypeStruct((2, nb, m, HY_W), F32), jax.ShapeDtypeStruct((2, nb, m, HY_W), F32),
                   jax.ShapeDtypeStruct((2, 2, ne, 1, HY_W), F32)],
        scratch_shapes=[pltpu.VMEM((L // 2, 2 * HY_FFN), F32), pltpu.VMEM((HY_W // 128, L, 128), F32),
                        pltpu.VMEM((HY_W // 128, L, 128), F32), pltpu.VMEM((nb, m, HY_W), BF16)],
        compiler_params=_cparams("arbitrary"),
        name="hyena_filter",
    )(jnp.asarray(z2), jnp.asarray(window), w1d, pair(b1[None, :]), w2d, pair(b2[None, :]), w3d, pair(sin_freq),
      *mats, tw, wt)


def _hyena_kernel(v_ref, x1_ref, x2_ref, g_ref, cw_ref, cb_ref, skip_ref, *refs, L):
    plan = _hy_plan(L)
    depth = plan["depth"]
    n_mats = len(plan["base"])
    mats = refs[:n_mats]
    tw_ref, hr_ref, hi_ref, hx_ref, o_ref, a_scr, x_scr, y_scr, xbf_scr, gc_scr, gs_scr = refs[n_mats:]
    ck = min(HY_ROWS, L)
    halo = 16
    chunks = [(t0, t0 + ck) for t0 in range(0, L, ck)]

    def split_store(scr, t0, t1, val):
        for h, part in enumerate(_lane_halves(val)):
            scr[h, t0:t1, :] = part

    def joined(scr, t0, t1):
        return jnp.concatenate([scr[h, t0:t1, :] for h in range(HY_W // 128)], axis=1)

    def short_conv(ref, j, t0, t1):
        lo, hi = max(t0 - halo, 0), min(t1 + halo, L)
        u = ref[lo:hi, :].astype(F32)
        row = lax.broadcasted_iota(jnp.int32, (hi - lo, HY_W), 0)
        prev = pltpu.roll(u, 1, 0)
        nxt = pltpu.roll(u, hi - lo - 1, 0)
        if lo == 0:
            prev = jnp.where(row == 0, 0.0, prev)
        if hi == L:
            nxt = jnp.where(row == hi - lo - 1, 0.0, nxt)
        w = cw_ref[:, j * HY_W:(j + 1) * HY_W]
        z = prev * w[0:1, :] + u * w[1:2, :] + nxt * w[2:3, :] + cb_ref[:, j * HY_W:(j + 1) * HY_W]
        return z[t0 - lo:t1 - lo, :]

    def long_conv(order):
        def product(spec, hr, hi):
            return _Spec([c * r + s * i for c, s, r, i in zip(spec.c, spec.s, hr, hi)],
                         [s * r - c * i for c, s, r, i in zip(spec.c, spec.s, hr, hi)])

        def consume(k0, k1, spec):
            nb = len(spec.c)
            z = product(spec, [hr_ref[order, b, k0:k1, :] for b in range(nb)],
                        [hi_ref[order, b, k0:k1, :] for b in range(nb)])
            for i, leaf in enumerate(_backward(depth, z, _block_twiddles(plan, tw_ref, k0, k1))):
                gc_scr[i, k0:k1, :] = leaf.c[0].astype(BF16)
                gs_scr[i, k0:k1, :] = leaf.s[0].astype(BF16)

        extras = _spectrum_of(plan, a_scr, xbf_scr, mats, tw_ref, consume)
        ne = len(extras.c)
        zx = product(extras, [hx_ref[order, 0, e] for e in range(ne)], [hx_ref[order, 1, e] for e in range(ne)])
        tops = [leaf.c[0] for leaf in _backward(depth, zx, plan["tw_extras"])]
        _sequence_of(plan, gc_scr, gs_scr, tops, mats, y_scr)

    for t0, t1 in chunks:
        split_store(a_scr, t0, t1, short_conv(v_ref, 0, t0, t1))
        x_scr[t0:t1, :] = short_conv(x1_ref, 1, t0, t1)
    long_conv(0)
    for t0, t1 in chunks:
        conv = joined(y_scr, t0, t1) + joined(a_scr, t0, t1) * skip_ref[0:1, :]
        split_store(a_scr, t0, t1, x_scr[t0:t1, :] * conv)
    long_conv(1)
    for t0, t1 in chunks:
        conv = joined(y_scr, t0, t1) + joined(a_scr, t0, t1) * skip_ref[1:2, :]
        y = short_conv(x2_ref, 2, t0, t1) * conv
        o_ref[t0:t1, :] = (y * _silu(g_ref[t0:t1, :].astype(F32))).astype(o_ref.dtype)


def _hyena(u, conv_w, conv_b, skip, consts, hr, hi, hx, B, L):
    plan = _hy_plan(L)
    m, nb, ne = plan["m"], plan["n_blocks"], plan["n_extras"]
    mats, tw, _ = consts
    col = lambda j: pl.BlockSpec((L, CB), lambda b: (b, j))
    const = lambda *shape: _resident(shape, lambda b: (0,) * len(shape))
    halves = HY_W // 128
    return pl.pallas_call(
        functools.partial(_hyena_kernel, L=L),
        grid=(B,),
        in_specs=[
            col(CB_HY_V), col(CB_HY_X1), col(CB_HY_X2), col(CB_HY_G),
            const(3, 3 * HY_W), const(1, 3 * HY_W), const(2, HY_W),
            *[const(m, m) for _ in mats], const(*tw.shape),
            const(2, nb, m, HY_W), const(2, nb, m, HY_W), const(2, 2, ne, 1, HY_W),
        ],
        out_specs=pl.BlockSpec((L, HY_W), lambda b: (b, 0)),
        out_shape=jax.ShapeDtypeStruct((B * L, HY_W), BF16),
        scratch_shapes=[
            pltpu.VMEM((halves, L, 128), F32), pltpu.VMEM((L, HY_W), F32), pltpu.VMEM((halves, L, 128), F32),
            pltpu.VMEM((nb, m, HY_W), BF16), pltpu.VMEM((nb, m, HY_W), BF16), pltpu.VMEM((nb, m, HY_W), BF16),
        ],
        compiler_params=_cparams("parallel"),
        name="hyena",
    )(u, u, u, u, conv_w, conv_b[None, :], skip, *mats, tw, hr, hi, hx)


def _ret_kernel(*refs, L, Lc, has_init):
    if has_init:
        (q_ref, k_ref, v_ref, g_ref, kc_ref, vc_ref, cos_ref, sin_ref, rate_ref, rrow_ref, ones_ref,
         o_ref, q_scr, k_scr, sf_scr, sb_scr, r_scr) = refs
    else:
        (q_ref, k_ref, v_ref, g_ref, cos_ref, sin_ref, rate_ref, rrow_ref, ones_ref,
         o_ref, q_scr, k_scr, sf_scr, sb_scr, r_scr) = refs
    C = min(RET_CHUNK, L)
    nch = L // C
    W = RET_W
    quarter = RET_HEAD_DIM // 4
    block = ones_ref[...]
    lg = -jnp.exp(rate_ref[...])
    lg_f, lg_b = lg[0:1, :], lg[1:2, :]

    lane = lax.broadcasted_iota(jnp.int32, (L, W), 1)
    first_quarter = (lane % (2 * quarter)) < quarter

    def rope(a):
        swapped = jnp.where(first_quarter, pltpu.roll(a, W - quarter, 1), pltpu.roll(a, quarter, 1))
        return a * cos_ref[...] + swapped * sin_ref[...]

    if has_init:
        q_scr[...] = rope(q_ref[...].astype(F32)).astype(BF16)
        k_scr[...] = rope(k_ref[...].astype(F32) * (RET_HEAD_DIM ** -0.5)).astype(BF16)
    else:
        q_scr[...] = q_ref[...]
        k_scr[...] = (k_ref[...].astype(F32) * (RET_HEAD_DIM ** -0.5)).astype(BF16)

    def decays(n_rows):
        pos = lax.broadcasted_iota(jnp.int32, (n_rows, W), 0).astype(F32)
        return (jnp.exp(lg_f * (n_rows - 1.0 - pos)),
                jnp.exp(lg_b * pos))

    def chunk_states(k_bf, v_bf, zf, zb):
        kf = (k_bf.astype(F32) * zf).astype(BF16)
        kb = (k_bf.astype(F32) * zb).astype(BF16)
        return _dot_tn(kf, v_bf) * block, _dot_tn(kb, v_bf) * block

    zeta_f, zeta_b = decays(C)
    if has_init:
        zc_f, zc_b = decays(Lc)
        kc = (kc_ref[...].astype(F32) * (RET_HEAD_DIM ** -0.5)).astype(BF16)
        s0_f, s0_b = chunk_states(kc, vc_ref[...], zc_f, zc_b)
    else:
        s0_f = jnp.zeros((W, W), F32)
        s0_b = jnp.zeros((W, W), F32)

    for n in range(nch):
        kv_f, kv_b = chunk_states(k_scr[n * C:(n + 1) * C, :], v_ref[n * C:(n + 1) * C, :], zeta_f, zeta_b)
        sf_scr[n] = kv_f
        sb_scr[n] = kv_b
    dec_f = jnp.exp(lg_f * float(C))
    dec_b = jnp.exp(lg_b * float(C))
    state = s0_f
    for n in range(nch):
        kv = sf_scr[n]
        sf_scr[n] = state
        state = dec_f * state + kv
    state = s0_b
    for n in range(nch - 1, -1, -1):
        kv = sb_scr[n]
        sb_scr[n] = state
        state = dec_b * state + kv

    posc = lax.broadcasted_iota(jnp.int32, (C, W), 0).astype(F32)
    xi_f = jnp.exp(lg_f * (posc + 1.0))
    xi_b = jnp.exp(lg_b * (float(C) - posc))
    diff = (lax.broadcasted_iota(jnp.int32, (C, C), 0) - lax.broadcasted_iota(jnp.int32, (C, C), 1)).astype(F32)
    lane_c = lax.broadcasted_iota(jnp.int32, (1, W), 1)
    dmask = []
    for h in range(RET_HEADS):
        rf = -jnp.exp(rrow_ref[h:h + 1, 0:C])
        rb = -jnp.exp(rrow_ref[RET_HEADS + h:RET_HEADS + h + 1, 0:C])
        dmask.append(jnp.where(diff >= 0, jnp.exp(rf * jnp.maximum(diff, 0.0)), 0.0)
                     + jnp.where(diff <= 0, jnp.exp(rb * jnp.maximum(-diff, 0.0)), 0.0))

    for n in range(nch):
        qn = q_scr[n * C:(n + 1) * C, :]
        kn = k_scr[n * C:(n + 1) * C, :]
        vn = v_ref[n * C:(n + 1) * C, :]
        qf = qn.astype(F32)
        lhs = [(qf * xi_f).astype(BF16), (qf * xi_b).astype(BF16)]
        rhs = [sf_scr[n].astype(BF16), sb_scr[n].astype(BF16)]
        zero = jnp.zeros_like(qn)
        for h in range(RET_HEADS):
            sel = (lane_c // RET_HEAD_DIM) == h
            s = _dot_nt(jnp.where(sel, qn, zero), kn) * dmask[h]
            lhs.append(s.astype(BF16))
            rhs.append(jnp.where(sel, vn, zero))
        r_scr[n * C:(n + 1) * C, :] = _dot(jnp.concatenate(lhs, axis=1), jnp.concatenate(rhs, axis=0))

    r = r_scr[...]
    ss = _dot((r * r).astype(BF16), block.astype(BF16))
    rn = r * lax.rsqrt(ss * (1.0 / RET_HEAD_DIM) + EPS)
    o_ref[...] = (rn * _silu(g_ref[...].astype(F32))).astype(o_ref.dtype)


def _retention(u, uc, ret_log_rate, B, L, Lc, has_init):
    C = min(RET_CHUNK, L)
    nch = L // C
    col = lambda j: pl.BlockSpec((L, CB), lambda b: (b, j))
    ccol = lambda j: pl.BlockSpec((Lc, CB), lambda b: (b, j))
    cos, sin = _rope_tables(L)
    rate_lane = jnp.repeat(ret_log_rate, RET_HEAD_DIM, axis=1)
    rate_row = jnp.broadcast_to(ret_log_rate.reshape(2 * RET_HEADS, 1), (2 * RET_HEADS, RET_W))
    ones = jnp.asarray(_head_block_ones(RET_W))
    in_specs = [col(CB_RE_Q), col(CB_RE_K), col(CB_RE_V), col(CB_RE_G)]
    args = [u, u, u, u]
    if has_init:
        in_specs += [ccol(CB_RE_K), ccol(CB_RE_V)]
        args += [uc, uc]
    in_specs += [
        _resident((L, RET_W), lambda b: (0, 0)),
        _resident((L, RET_W), lambda b: (0, 0)),
        _resident((2, RET_W), lambda b: (0, 0)),
        _resident((2 * RET_HEADS, RET_W), lambda b: (0, 0)),
        _resident((RET_W, RET_W), lambda b: (0, 0)),
    ]
    args += [jnp.asarray(cos), jnp.asarray(sin), rate_lane, rate_row, ones]
    return pl.pallas_call(
        functools.partial(_ret_kernel, L=L, Lc=Lc, has_init=has_init),
        grid=(B,),
        in_specs=in_specs,
        out_specs=pl.BlockSpec((L, RET_W), lambda b: (b, 0)),
        out_shape=jax.ShapeDtypeStruct((B * L, RET_W), BF16),
        scratch_shapes=[
            pltpu.VMEM((L, RET_W), BF16), pltpu.VMEM((L, RET_W), BF16),
            pltpu.VMEM((nch, RET_W, RET_W), F32), pltpu.VMEM((nch, RET_W, RET_W), F32),
            pltpu.VMEM((L, RET_W), F32),
        ],
        compiler_params=_cparams("parallel"),
        name="retention",
    )(*args)


def kernel(x, c, ctx, c_ctx, norm_w, ada_w, ada_b, w_in, w_out, na_q_gain, na_k_gain, na_rpb, hy_conv_w, hy_conv_b,
           hy_w1, hy_b1, hy_w2, hy_b2, hy_w3, hy_sin_freq, hy_skip, ret_log_rate):
    B, L, D = x.shape
    Lc = ctx.shape[1]
    assert D == D_MODEL and L % (GRID_W * NA_G * NA_GPS) == 0 and L // GRID_W >= NA_WIN
    assert NA_G == NA_KT // GRID_W and (L // GRID_W - NA_WIN) % NA_G == 0 and Lc == NA_KT

    n_cond = 16
    cc = jnp.concatenate([c, c_ctx[None, :], jnp.zeros((n_cond - B - 1, D), F32)], axis=0)
    mods = _modulation(cc, ada_w, ada_b)

    dft_x = _hyena_consts(L)
    dft_c = _hyena_consts(Lc)

    tm_x = 1024
    xf = x.reshape(B * L, D)
    cf = ctx.reshape(B * Lc, D)
    x_mod = lambda i: i // (L // tm_x)
    c_mod = lambda i: B

    w_in_bf_all = w_in.astype(BF16)
    w_out_bf_all = w_out.astype(BF16)

    def layer_params(i):
        return mods[i].reshape(n_cond, 1, 3 * D), norm_w[i][None, :], (w_in_bf_all, i)

    mod, nw, w_in_bf = layer_params(0)
    u = _projection(xf, x_mod, tm_x, in_args=(mod, nw, w_in_bf))
    uc = _projection(cf, c_mod, Lc, in_args=(mod, nw, w_in_bf))
    for i in range(DEPTH):
        w_out_bf = (w_out_bf_all, i)
        filt = (hy_w1[i], hy_b1[i], hy_w2[i], hy_b2[i], hy_w3[i], hy_sin_freq[i])
        last = i == DEPTH - 1
        nxt = None if last else layer_params(i + 1)

        za = _neighborhood_attention(u, uc, na_q_gain[i], na_k_gain[i], na_rpb[i], B, L, Lc)
        hr, hi, hm = _hyena_filter(L, *filt, dft_x)
        zy = _hyena(u, hy_conv_w[i], hy_conv_b[i], hy_skip[i], dft_x, hr, hi, hm, B, L)
        zr = _retention(u, uc, ret_log_rate[i], B, L, Lc, True)
        if not last:
            zac = _context_attention(uc, na_q_gain[i], na_k_gain[i], B, Lc)
            hrc, hic, hmc = _hyena_filter(Lc, *filt, dft_c)
            zyc = _hyena(uc, hy_conv_w[i], hy_conv_b[i], hy_skip[i], dft_c, hrc, hic, hmc, B, Lc)
            zrc = _retention(uc, uc, ret_log_rate[i], B, Lc, Lc, False)
            cf, uc = _projection(cf, c_mod, Lc, out_args=(zac, zyc, zrc, mod, w_out_bf), in_args=nxt)
            xf, u = _projection(xf, x_mod, tm_x, out_args=(za, zy, zr, mod, w_out_bf), in_args=nxt)
            mod = nxt[0]
        else:
            tm_last = 2 * tm_x
            xf = _projection(xf, lambda i: i // (L // tm_last), tm_last, out_args=(za, zy, zr, mod, w_out_bf))

    return xf.reshape(B, L, D)
```

```python
import functools
import math

import numpy as np
import jax
import jax.numpy as jnp
from jax import lax
from jax.experimental import pallas as pl
from jax.experimental.pallas import tpu as pltpu

F32 = jnp.float32
BF16 = jnp.bfloat16

D_MODEL = 1024
DEPTH = 2
GRID_W = 64
NA_HEADS = 8
NA_HEAD_DIM = 64
NA_W = NA_HEADS * NA_HEAD_DIM
NA_KH = 8
NA_KW = 16
HY_W = 256
HY_BANDS = 8
HY_EMB = 1 + 2 * HY_BANDS
HY_EMB_PAD = 32
HY_FFN = 64
HY_FAST_DECAY = 0.3
HY_SLOW_DECAY = 1.5
HY_TARGET = 1e-2
RET_HEADS = 4
RET_HEAD_DIM = 64
RET_W = RET_HEADS * RET_HEAD_DIM
ROPE_BASE = 10000.0
EPS = 1e-6
NEG_INF = -1e30
LOG2E = 1.4426950408889634
IN_W = 4 * NA_W + 4 * HY_W + 4 * RET_W
MIX_W = NA_W + HY_W + RET_W

CB = 256
CB_HY_V, CB_HY_X1, CB_HY_X2, CB_HY_G = 8, 9, 10, 11
CB_RE_Q, CB_RE_K, CB_RE_V, CB_RE_G = 12, 13, 14, 15
CTX_KV_COLS = (1, 2, 6, 7)
CTX_KV_NA = (0, 1)
CTX_KV_RE = (5, 6)

NA_G = 4
NA_WIN = NA_G + NA_KH
NORM_W = 256
NA_GPS = 4
NA_KT = 256
NA_DR_MIN = (NA_KH - 1) - NA_KH - (NA_G - 1)
NA_DR_NUM = (NA_WIN - 1) + (NA_KH - 1) - NA_DR_MIN + 1
RET_CHUNK = 256
HY_ROWS = 512
PROJ_COLS = 512
HY_BASE = 512
VMEM_LIMIT = 56 * 1024 * 1024


def _cparams(*sem):
    return pltpu.CompilerParams(dimension_semantics=sem, vmem_limit_bytes=VMEM_LIMIT)


def _resident(shape, index_map):
    return pl.BlockSpec(shape, index_map, pipeline_mode=pl.Buffered(1))


def _silu(x):
    return x * (1.0 / (1.0 + jnp.exp(-x)))


def _dot(a, b):
    return jnp.dot(a, b, preferred_element_type=F32)


def _dot_nt(a, b):
    return lax.dot_general(a, b, (((1,), (1,)), ((), ())), preferred_element_type=F32)


def _dot_tn(a, b):
    return lax.dot_general(a, b, (((0,), (0,)), ((), ())), preferred_element_type=F32)


@functools.lru_cache(maxsize=None)
def _bin_map(n, depth):
    if depth == 0:
        return [np.arange(n)], [n]
    hb, he = _bin_map(n // 2, depth - 1)
    blocks = [b for b in hb] + [n - b for b in hb]
    extras = list(he) + [n - e for e in he if e < n // 2]
    return blocks, extras


@functools.lru_cache(maxsize=None)
def _hy_plan(L):
    depth = 0
    while L >> depth > HY_BASE:
        depth += 1
    m = L >> depth
    k = np.arange(m, dtype=np.int64)[:, None] * np.arange(m, dtype=np.int64)[None, :]
    ang = (k % (2 * m)).astype(np.float64) * (math.pi / m)
    base = [np.cos(ang), np.sin(ang)]
    if depth:
        odd = (np.arange(m, dtype=np.int64)[:, None] * (2 * np.arange(m, dtype=np.int64)[None, :] + 1)) % (4 * m)
        odd = odd.astype(np.float64) * (math.pi / (2 * m))
        base += [np.cos(odd), np.sin(odd), np.cos(odd).T, np.sin(odd).T]
    base = tuple(np.ascontiguousarray(a).astype(np.float32) for a in base)
    tw_blocks, tw_extras = [], []
    for level in range(1, depth + 1):
        n = m << level
        hb, he = _bin_map(n // 2, level - 1)
        tw_blocks.append([(np.cos(math.pi * b / n), np.sin(math.pi * b / n)) for b in hb])
        tw_extras.append([(math.cos(math.pi * e / n), math.sin(math.pi * e / n), e == n // 2) for e in he])
    blocks, extras = _bin_map(L, depth)
    inv_n = 1.0 / (2 * L)
    seen = set()

    def weight(b):
        w = 0.0 if b in seen else (inv_n if b in (0, L) else 2.0 * inv_n)
        seen.add(b)
        return w

    w_blocks = [np.array([weight(int(b)) for b in blk]) for blk in blocks]
    w_extras = [weight(int(e)) for e in extras]
    tables = [np.broadcast_to(t[:, None], (m, HY_W)) for lvl in tw_blocks for pair in lvl for t in pair]
    tw_table = np.stack(tables).astype(np.float32) if tables else np.zeros((1, 8, HY_W), np.float32)
    w_table = np.stack([np.broadcast_to(w[:, None], (m, HY_W)) for w in w_blocks]).astype(np.float32)
    return dict(depth=depth, m=m, base=base, tw_extras=tw_extras, n_blocks=len(blocks), n_extras=len(extras),
                w_extras=w_extras, tw_table=tw_table, w_table=w_table,
                tw_index=[[2 * (sum(len(l) for l in tw_blocks[:lv]) + i) for i in range(len(tw_blocks[lv]))]
                          for lv in range(depth)])


@functools.lru_cache(maxsize=None)
def _filter_features(L):
    t = np.linspace(0.0, 1.0, L)[:, None]
    omega = 2.0 * math.pi * np.arange(L)[:, None] / L
    bands = np.linspace(1e-4, HY_BANDS - 1, HY_BANDS)[None, :]
    z = np.concatenate([t, np.cos(bands * omega), -np.sin(bands * omega)], axis=-1)
    z = np.pad(z, ((0, 0), (0, HY_EMB_PAD - HY_EMB)))
    deltas = np.abs(np.linspace(math.log(HY_TARGET) / HY_SLOW_DECAY, math.log(HY_TARGET) / HY_FAST_DECAY, HY_W))
    window = np.exp(-t * deltas[None, :])
    return z.astype(np.float32), window.astype(np.float32)


@functools.lru_cache(maxsize=None)
def _rope_tables(L):
    half = RET_HEAD_DIM // 2
    quarter = half // 2
    t = np.arange(L)
    pos = np.stack([t // GRID_W, t % GRID_W], axis=0).astype(np.float64)
    freqs = ROPE_BASE ** (-np.arange(quarter, dtype=np.float64) / quarter)
    cos = np.zeros((L, RET_HEAD_DIM))
    sin = np.zeros((L, RET_HEAD_DIM))
    for a in range(2):
        ang = pos[a][:, None] * freqs[None, :]
        base = a * half
        cos[:, base:base + quarter] = np.cos(ang)
        cos[:, base + quarter:base + half] = np.cos(ang)
        sin[:, base:base + quarter] = -np.sin(ang)
        sin[:, base + quarter:base + half] = np.sin(ang)
    return (np.tile(cos, (1, RET_HEADS)).astype(np.float32), np.tile(sin, (1, RET_HEADS)).astype(np.float32))


@functools.lru_cache(maxsize=None)
def _head_block_ones(width):
    i = np.arange(width) // NA_HEAD_DIM
    return (i[:, None] == i[None, :]).astype(np.float32)


def _mod_kernel(c_ref, w_ref, b_ref, o_ref):
    s = _silu(c_ref[...])
    w = w_ref[...]
    s_hi = s.astype(BF16)
    w_hi = w.astype(BF16)
    s_lo = (s - s_hi.astype(F32)).astype(BF16)
    w_lo = (w - w_hi.astype(F32)).astype(BF16)
    o_ref[...] = _dot(s_hi, w_hi) + (_dot(s_hi, w_lo) + _dot(s_lo, w_hi)) + b_ref[...]


def _modulation(cc, ada_w, ada_b):
    R = cc.shape[0]
    tn = 1024
    return pl.pallas_call(
        _mod_kernel,
        grid=(DEPTH, 3 * D_MODEL // tn),
        in_specs=[
            pl.BlockSpec((R, D_MODEL), lambda i, j: (0, 0)),
            pl.BlockSpec((None, D_MODEL, tn), lambda i, j: (i, 0, j)),
            pl.BlockSpec((None, 1, tn), lambda i, j: (i, 0, j)),
        ],
        out_specs=pl.BlockSpec((None, R, tn), lambda i, j: (i, 0, j)),
        out_shape=jax.ShapeDtypeStruct((DEPTH, R, 3 * D_MODEL), F32),
        compiler_params=_cparams("parallel", "parallel"),
        name="modulation",
    )(cc, ada_w, ada_b.reshape(DEPTH, 1, 3 * D_MODEL))


def _proj_kernel(*refs, with_out, with_in, in_cols):
    refs = list(refs)
    if with_out:
        za_ref, zy_ref, zr_ref, x_ref, mod_o_ref, w_out_ref = refs[:6]
        del refs[:6]
    else:
        x_ref = refs.pop(0)
    if with_in:
        mod_i_ref, nw_ref, w_in_ref = refs[:3]
        del refs[:3]
    x = x_ref[...]
    if with_out:
        acc = _dot(za_ref[...], w_out_ref[0:NA_W, :])
        acc += _dot(zy_ref[...], w_out_ref[NA_W:NA_W + HY_W, :])
        acc += _dot(zr_ref[...], w_out_ref[NA_W + HY_W:MIX_W, :])
        x = x + mod_o_ref[:, 2 * D_MODEL:3 * D_MODEL] * acc
        refs.pop(0)[...] = x
    if with_in:
        u_ref = refs.pop(0)
        xn = x * lax.rsqrt(jnp.mean(x * x, axis=-1, keepdims=True) + EPS)
        shift = mod_i_ref[:, 0:D_MODEL]
        scale = mod_i_ref[:, D_MODEL:2 * D_MODEL]
        h = (xn * nw_ref[...] * (1.0 + scale) + shift).astype(BF16)
        tn = PROJ_COLS
        for j, c in enumerate(in_cols):
            u_ref[:, j * tn:(j + 1) * tn] = _dot(h, w_in_ref[:, c * tn:(c + 1) * tn]).astype(u_ref.dtype)


def _projection(xf, mod_index, tm, out_args=None, in_args=None, in_cols=None):
    R = xf.shape[0]
    row = lambda w: pl.BlockSpec((tm, w), lambda i: (i, 0))
    mod_spec = pl.BlockSpec((None, 1, 3 * D_MODEL), lambda i: (mod_index(i), 0, 0))
    args, in_specs, out_specs, out_shape = [], [], [], []
    if out_args is not None:
        za, zy, zr, mod_o, (w_out_bf, layer_o) = out_args
        args += [za, zy, zr, xf, mod_o, w_out_bf]
        in_specs += [row(NA_W), row(HY_W), row(RET_W), row(D_MODEL), mod_spec,
                     _resident((None, MIX_W, D_MODEL), lambda i: (layer_o, 0, 0))]
        out_specs.append(row(D_MODEL))
        out_shape.append(jax.ShapeDtypeStruct((R, D_MODEL), F32))
    else:
        args.append(xf)
        in_specs.append(row(D_MODEL))
    if in_args is not None:
        mod_i, norm_w, (w_in_bf, layer_i) = in_args
        args += [mod_i, norm_w, w_in_bf]
        in_specs += [mod_spec, _resident((1, D_MODEL), lambda i: (0, 0)),
                     _resident((None, D_MODEL, IN_W), lambda i: (layer_i, 0, 0))]
        in_cols = tuple(range(IN_W // PROJ_COLS)) if in_cols is None else tuple(in_cols)
        out_specs.append(row(len(in_cols) * PROJ_COLS))
        out_shape.append(jax.ShapeDtypeStruct((R, len(in_cols) * PROJ_COLS), BF16))
    outs = pl.pallas_call(
        functools.partial(_proj_kernel, with_out=out_args is not None, with_in=in_args is not None, in_cols=in_cols),
        grid=(R // tm,),
        in_specs=in_specs,
        out_specs=out_specs,
        out_shape=out_shape,
        compiler_params=_cparams("parallel"),
        name="projection",
    )(*args)
    return outs if len(outs) > 1 else outs[0]


def _head_rms(x, ones_bf, gain):
    sq = (x * x).astype(BF16)
    ss = jnp.concatenate([_dot(sq[:, c:c + NORM_W], ones_bf) for c in range(0, x.shape[1], NORM_W)], axis=1)
    return x * lax.rsqrt(ss * (1.0 / NA_HEAD_DIM) + EPS) * gain


def _attend_heads(q, k, v):
    first = lax.broadcasted_iota(jnp.int32, (1, 2 * NA_HEAD_DIM), 1) < NA_HEAD_DIM
    outs = []
    for pair in range(NA_HEADS // 2):
        lanes = slice(pair * 2 * NA_HEAD_DIM, (pair + 1) * 2 * NA_HEAD_DIM)
        o_pair = None
        for sub in range(2):
            sel = first if sub == 0 else jnp.logical_not(first)
            s = _dot_nt(jnp.where(sel, q[:, lanes], 0.0).astype(BF16), k[:, lanes])
            p = jnp.exp(s - s.max(axis=-1, keepdims=True))
            o_h = _dot(p.astype(BF16), v[:, lanes]) * (1.0 / p.sum(axis=-1, keepdims=True))
            o_pair = o_h if o_pair is None else jnp.where(first, o_pair, o_h)
        outs.append(o_pair)
    return jnp.concatenate(outs, axis=-1)


def _na_kernel(q_ref, k_ref, v_ref, g_ref, kc_ref, vc_ref, qg_ref, kg_ref, ones_ref, rpb_ref, o_ref,
               kn_scr, kcn_scr, vat_scr, vbt_scr, vcat_scr, vcbt_scr, bias_scr, mask_scr, s0_scr, s1_scr,
               p0_scr, p1_scr, qm_scr, ot_scr, *, rows):
    batch = pl.program_id(0)
    grp = pl.program_id(1)
    ones_bf = ones_ref[...]
    tq = NA_G * GRID_W
    nwin = NA_WIN * GRID_W
    Lc = kc_ref.shape[0]
    pair_w = 2 * NA_HEAD_DIM
    kt = NA_KT
    first = lax.broadcasted_iota(jnp.int32, (1, pair_w), 1) < NA_HEAD_DIM

    @pl.when((batch == 0) & (grp == 0))
    def _():
        ck = lax.broadcasted_iota(jnp.int32, (GRID_W, pair_w), 0)
        lane = lax.broadcasted_iota(jnp.int32, (GRID_W, pair_w), 1)
        cq = lane % GRID_W
        col_start = jnp.clip(cq - NA_KW // 2, 0, GRID_W - NA_KW)
        col_ok = (ck >= col_start) & (ck < col_start + NA_KW)
        left = lane < GRID_W

        def body(i, carry):
            for h in range(NA_HEADS):
                v0 = jnp.broadcast_to(rpb_ref[i + 1, h:h + 1, :], (GRID_W, pair_w))
                v1 = jnp.broadcast_to(rpb_ref[i, h:h + 1, :], (GRID_W, pair_w))
                t0 = pltpu.roll(v0, pair_w - (NA_KW - 1), 1, stride=1, stride_axis=0)
                t1 = pltpu.roll(v1, GRID_W - (NA_KW - 1), 1, stride=1, stride_axis=0)
                bias_scr[i, h] = jnp.where(col_ok, jnp.where(left, t0, t1), NEG_INF)
            return carry

        lax.fori_loop(0, NA_DR_NUM, body, 0)

    @pl.when(grp == 0)
    def _():
        kgain = kg_ref[...]
        own_a = (lax.broadcasted_iota(jnp.int32, (NA_W, 1), 0) % pair_w) < NA_HEAD_DIM
        for src, dst, vat, vbt, vsrc in ((k_ref, kn_scr, vat_scr, vbt_scr, v_ref),
                                         (kc_ref, kcn_scr, vcat_scr, vcbt_scr, vc_ref)):
            for i in range(src.shape[0] // kt):
                sl = slice(i * kt, (i + 1) * kt)
                dst[sl, :] = _head_rms(src[sl, :].astype(F32), ones_bf, kgain).astype(BF16)
                vt = vsrc[sl, :].astype(F32).T
                vat[i] = jnp.where(own_a, vt, 1.0).astype(BF16)
                vbt[i] = jnp.where(own_a, 1.0, vt).astype(BF16)

    n_kt = (nwin + Lc) // kt
    rows_per_tile = kt // GRID_W
    n_ch = (nwin + Lc) // GRID_W
    n_items = NA_GPS * NA_HEADS

    s_bufs = (s0_scr, s1_scr)
    p_bufs = (p0_scr, p1_scr)
    row_zero = jnp.minimum(grp, 0)

    def rows_of(start, size):
        return pl.ds(pl.multiple_of(row_zero + start, GRID_W), size)

    def head_lanes(h):
        return slice((h // 2) * pair_w, (h // 2 + 1) * pair_w)

    key0, tile0, dr_base = [], [], []
    for gi in range(NA_GPS):
        r0 = (grp * NA_GPS + gi) * NA_G
        ws = jnp.clip(r0 - NA_KH // 2, 0, rows - NA_WIN)
        key0.append(pl.multiple_of(ws * GRID_W, kt))
        tile0.append(ws // rows_per_tile)
        dr_base.append(ws - r0 + (NA_KH - 1) - NA_DR_MIN)
        w_id = lax.broadcasted_iota(jnp.int32, (16, tq), 0)
        g_id = lax.broadcasted_iota(jnp.int32, (16, tq), 1) // GRID_W
        lo = jnp.clip(r0 + g_id - NA_KH // 2, 0, rows - NA_KH) - ws
        mask_scr[gi] = jnp.where((w_id >= lo) & (w_id < lo + NA_KH), 0.0, NEG_INF).astype(F32)
        q = _head_rms(q_ref[gi * tq:(gi + 1) * tq, :].astype(F32), ones_bf,
                      qg_ref[...] * (NA_HEAD_DIM ** -0.5 * LOG2E))
        for h in range(NA_HEADS):
            sel = first if h % 2 == 0 else jnp.logical_not(first)
            qm_scr[gi * NA_HEADS + h, rows_of(0, tq), :] = jnp.where(sel, q[:, head_lanes(h)], 0.0).astype(BF16)

    col_max = {}

    def qk_piece(n, j):
        gi, h = divmod(n, NA_HEADS)
        last = j == n_kt - 1
        keys = kcn_scr[:, head_lanes(h)] if last else kn_scr[pl.ds(key0[gi] + j * kt, kt), head_lanes(h)]
        sv = _dot_nt(keys, qm_scr[n, rows_of(0, tq), :])
        for dw in range(rows_per_tile):
            sw = sv[dw * GRID_W:(dw + 1) * GRID_W, :]
            if not last:
                w = j * rows_per_tile + dw
                bias = jnp.concatenate([bias_scr[dr_base[gi] + (w - g), h] for g in range(0, NA_G, 2)], axis=-1)
                sw = sw + bias + mask_scr[gi, w:w + 1, :]
            s_bufs[n % 2][rows_of(j * kt + dw * GRID_W, GRID_W), :] = sw
            part = sw.reshape(GRID_W // 8, 8, tq).max(axis=0)
            col_max[n] = part if (j == 0 and dw == 0) else jnp.maximum(col_max[n], part)

    def exp_chunk(n, c):
        if c == 0:
            col_max[n] = col_max[n].max(axis=0, keepdims=True)
        rows_c = rows_of(c * GRID_W, GRID_W)
        p_bufs[n % 2][rows_c, :] = jnp.exp2(s_bufs[n % 2][rows_c, :] - col_max[n]).astype(BF16)
        if c == n_ch - 1:
            del col_max[n]

    def pv(n):
        gi, h = divmod(n, NA_HEADS)
        vt, vct = (vat_scr, vcat_scr) if h % 2 == 0 else (vbt_scr, vcbt_scr)
        ot = _dot(vct[0, head_lanes(h), :], p_bufs[n % 2][rows_of(nwin, Lc), :])
        for j in range(n_kt - 1):
            ot += _dot(vt[tile0[gi] + j, head_lanes(h), :], p_bufs[n % 2][rows_of(j * kt, kt), :])
        own, den = (ot[0:NA_HEAD_DIM], ot[NA_HEAD_DIM:]) if h % 2 == 0 else (ot[NA_HEAD_DIM:], ot[0:NA_HEAD_DIM])
        ot_scr[rows_of(n * NA_HEAD_DIM, NA_HEAD_DIM), :] = own * (1.0 / den)
        if h % 2 == 1:
            o_pair = ot_scr[rows_of((n - 1) * NA_HEAD_DIM, pair_w), :].T
            q_rows = slice(gi * tq, (gi + 1) * tq)
            gate = _silu(g_ref[q_rows, head_lanes(h)].astype(F32))
            o_ref[q_rows, head_lanes(h)] = (o_pair * gate).astype(o_ref.dtype)

    for t in range(n_items + 2):
        for i in range(n_ch):
            if 0 <= t - 1 < n_items:
                exp_chunk(t - 1, i)
            if i % 4 == 0 and t < n_items and i // 4 < n_kt:
                qk_piece(t, i // 4)
            if i == 2 and 0 <= t - 2:
                pv(t - 2)


def _neighborhood_attention(u, uc, q_gain, k_gain, rpb, B, L, Lc, ctx_kv=(1, 2)):
    rows = L // GRID_W
    ngrp = rows // (NA_G * NA_GPS)
    tq = NA_G * GRID_W
    tstep = tq * NA_GPS
    ones = jnp.asarray(_head_block_ones(NORM_W)).astype(BF16)
    qg = jnp.tile(q_gain, NA_HEADS)[None, :]
    kg = jnp.tile(k_gain, NA_HEADS)[None, :]
    n_dr = 2 * NA_KH - 1
    rpb_t = jnp.transpose(rpb.astype(F32)[:, :, ::-1] * LOG2E, (1, 0, 2))
    lo_pad = 1 - NA_DR_MIN
    rpb_t = jnp.pad(rpb_t, ((lo_pad, NA_DR_NUM + 1 - lo_pad - n_dr), (0, 0), (0, 2 * GRID_W - (2 * NA_KW - 1))))
    return pl.pallas_call(
        functools.partial(_na_kernel, rows=rows),
        grid=(B, ngrp),
        in_specs=[
            pl.BlockSpec((tstep, NA_W), lambda b, g: (b * ngrp + g, 0)),
            pl.BlockSpec((L, NA_W), lambda b, g: (b, 1)),
            pl.BlockSpec((L, NA_W), lambda b, g: (b, 2)),
            pl.BlockSpec((tstep, NA_W), lambda b, g: (b * ngrp + g, 3)),
            pl.BlockSpec((Lc, NA_W), lambda b, g: (b, ctx_kv[0])),
            pl.BlockSpec((Lc, NA_W), lambda b, g: (b, ctx_kv[1])),
            _resident((1, NA_W), lambda b, g: (0, 0)),
            _resident((1, NA_W), lambda b, g: (0, 0)),
            _resident((NORM_W, NORM_W), lambda b, g: (0, 0)),
            _resident((NA_DR_NUM + 1, NA_HEADS, 2 * GRID_W), lambda b, g: (0, 0, 0)),
        ],
        out_specs=pl.BlockSpec((tstep, NA_W), lambda b, g: (b * ngrp + g, 0)),
        out_shape=jax.ShapeDtypeStruct((B * L, NA_W), BF16),
        scratch_shapes=[
            pltpu.VMEM((L, NA_W), BF16), pltpu.VMEM((Lc, NA_W), BF16),
            pltpu.VMEM((L // NA_KT, NA_W, NA_KT), BF16), pltpu.VMEM((L // NA_KT, NA_W, NA_KT), BF16),
            pltpu.VMEM((Lc // NA_KT, NA_W, NA_KT), BF16), pltpu.VMEM((Lc // NA_KT, NA_W, NA_KT), BF16),
            pltpu.VMEM((NA_DR_NUM, NA_HEADS, GRID_W, 2 * GRID_W), F32),
            pltpu.VMEM((NA_GPS, 16, tq), F32),
            pltpu.VMEM((NA_WIN * GRID_W + Lc, tq), F32), pltpu.VMEM((NA_WIN * GRID_W + Lc, tq), F32),
            pltpu.VMEM((NA_WIN * GRID_W + Lc, tq), BF16), pltpu.VMEM((NA_WIN * GRID_W + Lc, tq), BF16),
            pltpu.VMEM((NA_GPS * NA_HEADS, tq, 2 * NA_HEAD_DIM), BF16), pltpu.VMEM((NA_GPS * NA_W, tq), F32),
        ],
        compiler_params=_cparams("arbitrary", "arbitrary"),
        name="neighborhood_attention",
    )(u, u, u, u, uc, uc, qg, kg, ones, rpb_t)


def _ctx_attn_kernel(q_ref, k_ref, v_ref, g_ref, qg_ref, kg_ref, ones_ref, o_ref):
    ones_bf = ones_ref[...]
    q = _head_rms(q_ref[...].astype(F32), ones_bf, qg_ref[...] * (NA_HEAD_DIM ** -0.5))
    k = _head_rms(k_ref[...].astype(F32), ones_bf, kg_ref[...]).astype(BF16)
    o = _attend_heads(q, k, v_ref[...])
    o_ref[...] = (o * _silu(g_ref[...].astype(F32))).astype(o_ref.dtype)


def _context_attention(uc, q_gain, k_gain, B, Lc):
    ones = jnp.asarray(_head_block_ones(NORM_W)).astype(BF16)
    qg = jnp.tile(q_gain, NA_HEADS)[None, :]
    kg = jnp.tile(k_gain, NA_HEADS)[None, :]
    return pl.pallas_call(
        _ctx_attn_kernel,
        grid=(B,),
        in_specs=[
            pl.BlockSpec((Lc, NA_W), lambda b: (b, 0)),
            pl.BlockSpec((Lc, NA_W), lambda b: (b, 1)),
            pl.BlockSpec((Lc, NA_W), lambda b: (b, 2)),
            pl.BlockSpec((Lc, NA_W), lambda b: (b, 3)),
            _resident((1, NA_W), lambda b: (0, 0)),
            _resident((1, NA_W), lambda b: (0, 0)),
            _resident((NORM_W, NORM_W), lambda b: (0, 0)),
        ],
        out_specs=pl.BlockSpec((Lc, NA_W), lambda b: (b, 0)),
        out_shape=jax.ShapeDtypeStruct((B * Lc, NA_W), BF16),
        compiler_params=_cparams("parallel"),
        name="context_attention",
    )(uc, uc, uc, uc, qg, kg, ones)


def _alt_sign(shape):
    row = lax.broadcasted_iota(jnp.int32, shape, 0)
    return (1 - 2 * (row & 1)).astype(F32)


def _lane_halves(x):
    return [x[:, h * 128:(h + 1) * 128] for h in range(HY_W // 128)]


class _Spec:
    def __init__(self, c, s):
        self.c, self.s = list(c), list(s)


def _combine(P, R, tw):
    lo_c, lo_s, mi_c, mi_s = [], [], [], []
    for (pc, ps, rc, rs, (ct, st, single)) in zip(P.c, P.s, R.c, R.s, tw):
        if single:
            lo_c.append(pc)
            lo_s.append(rc)
            continue
        qc, qs = (rc, rs) if ct is None else (ct * rc - st * rs, st * rc + ct * rs)
        lo_c.append(pc + qc)
        lo_s.append(ps + qs)
        mi_c.append(pc - qc)
        mi_s.append(qs - ps)
    return _Spec(lo_c + mi_c, lo_s + mi_s)


def _combine_t(Z, tw):
    n_lo = len(tw)
    lo_c, lo_s, mi_c, mi_s = Z.c[:n_lo], Z.s[:n_lo], Z.c[n_lo:], Z.s[n_lo:]
    pc, ps, rc, rs = [], [], [], []
    mi = 0
    for i, (ct, st, single) in enumerate(tw):
        if single:
            pc.append(lo_c[i])
            ps.append(jnp.zeros_like(lo_c[i]))
            rc.append(lo_s[i])
            rs.append(jnp.zeros_like(lo_c[i]))
            continue
        g_qc = lo_c[i] - mi_c[mi]
        g_qs = lo_s[i] + mi_s[mi]
        pc.append(lo_c[i] + mi_c[mi])
        ps.append(lo_s[i] - mi_s[mi])
        rc.append(g_qc if ct is None else ct * g_qc + st * g_qs)
        rs.append(g_qs if ct is None else ct * g_qs - st * g_qc)
        mi += 1
    return _Spec(pc, ps), _Spec(rc, rs)


def _forward(level, leaves, tws):
    if level == 0:
        return leaves[0]
    half = len(leaves) // 2
    return _combine(_forward(level - 1, leaves[:half], tws), _forward(level - 1, leaves[half:], tws), tws[level - 1])


def _backward(level, Z, tws):
    if level == 0:
        return [Z]
    P, R = _combine_t(Z, tws[level - 1])
    return _backward(level - 1, P, tws) + _backward(level - 1, R, tws)


def _leaf_samples(depth, off=0, stride=1):
    if depth == 0:
        return [(off, stride)]
    return _leaf_samples(depth - 1, off, 2 * stride) + _leaf_samples(depth - 1, off + stride, 2 * stride)


def _block_twiddles(plan, tw_ref, k0, k1):
    return [[(None, None, False) if lv == 0 else (tw_ref[i, k0:k1, :], tw_ref[i + 1, k0:k1, :], False)
             for i in plan["tw_index"][lv]] for lv in range(plan["depth"])]


def _leaf_matrices(plan, mats, i, transposed):
    if plan["depth"] == 0 or i % 2 == 0:
        return mats[0], mats[1]
    return (mats[4], mats[5]) if transposed else (mats[2], mats[3])


def _spectrum_of(plan, src_scr, xbf_scr, mats, tw_ref, consume):
    depth, m = plan["depth"], plan["m"]
    cm = min(HY_ROWS, m)
    sign = _alt_sign((cm, HY_W))
    tops = []
    for i, (off, stride) in enumerate(_leaf_samples(depth)):
        top = jnp.zeros((1, HY_W), F32)
        for j0 in range(0, m, cm):
            x = jnp.concatenate([src_scr[h, pl.ds(off + stride * j0, cm, stride=stride), :]
                                 for h in range(HY_W // 128)], axis=1)
            xbf_scr[i, j0:j0 + cm, :] = x.astype(BF16)
            top += jnp.sum(x * sign, axis=0, keepdims=True)
        tops.append(top)
    for k0 in range(0, m, cm):
        k1 = k0 + cm
        base = []
        for i in range(len(tops)):
            c_ref, s_ref = _leaf_matrices(plan, mats, i, False)
            base.append(_Spec([_dot(c_ref[k0:k1, :], xbf_scr[i])], [_dot(s_ref[k0:k1, :], xbf_scr[i])]))
        consume(k0, k1, _forward(depth, base, _block_twiddles(plan, tw_ref, k0, k1)))
    zero = jnp.zeros((1, HY_W), F32)
    return _forward(depth, [_Spec([t], [zero]) for t in tops], plan["tw_extras"])


def _sequence_of(plan, gc_scr, gs_scr, tops, mats, dst_scr):
    depth, m = plan["depth"], plan["m"]
    cm = min(HY_ROWS, m)
    sign = _alt_sign((cm, HY_W))
    for i, (off, stride) in enumerate(_leaf_samples(depth)):
        c_ref, s_ref = _leaf_matrices(plan, mats, i, True)
        for j0 in range(0, m, cm):
            y = _dot(c_ref[j0:j0 + cm, :], gc_scr[i]) + _dot(s_ref[j0:j0 + cm, :], gs_scr[i]) + sign * tops[i]
            for h in range(HY_W // 128):
                dst_scr[h, pl.ds(off + stride * j0, cm, stride=stride), :] = y[:, h * 128:(h + 1) * 128]


def _filter_kernel(z_ref, win_ref, w1_ref, b1_ref, w2_ref, b2_ref, w3_ref, sf_ref, *refs, L):
    plan = _hy_plan(L)
    n_mats = len(plan["base"])
    mats = refs[:n_mats]
    tw_ref, wt_ref, hr_ref, hi_ref, hx_ref, h_scr, a_scr, d_scr, xbf_scr = refs[n_mats:]
    hp = lax.Precision.HIGHEST
    dot = lambda a, b: jnp.dot(a, b, preferred_element_type=F32, precision=hp)

    @pl.when(pl.program_id(0) == 0)
    def _():
        h = jnp.sin(sf_ref[0:1, :] * (dot(z_ref[...], w1_ref[...]) + b1_ref[...]))
        h_scr[...] = jnp.sin(sf_ref[1:2, :] * (dot(h, w2_ref[...]) + b2_ref[...]))

    taps = dot(h_scr[...], w3_ref[...])
    win = win_ref[...]
    row = lax.broadcasted_iota(jnp.int32, (L, HY_W), 0)
    hf = jnp.concatenate([taps[:, 0:HY_W], taps[:, 2 * HY_W:3 * HY_W]], axis=0) * win
    hb = jnp.concatenate([taps[:, HY_W:2 * HY_W], taps[:, 3 * HY_W:4 * HY_W]], axis=0) * win
    hb = jnp.where(row == 0, 0.0, hb)
    norm = jnp.sum(jnp.abs(hf), axis=0, keepdims=True) + jnp.sum(jnp.abs(hb), axis=0, keepdims=True)
    inv = 1.0 / norm
    for h, (a_half, d_half) in enumerate(zip(_lane_halves((hf + hb) * inv), _lane_halves((hf - hb) * inv))):
        a_scr[h] = a_half
        d_scr[h] = d_half

    def store_real(k0, k1, spec):
        for b, blk in enumerate(spec.c):
            hr_ref[b, k0:k1, :] = blk * wt_ref[b, k0:k1, :]

    def store_imag(k0, k1, spec):
        for b, blk in enumerate(spec.s):
            hi_ref[b, k0:k1, :] = -blk * wt_ref[b, k0:k1, :]

    extra_a = _spectrum_of(plan, a_scr, xbf_scr, mats, tw_ref, store_real)
    extra_d = _spectrum_of(plan, d_scr, xbf_scr, mats, tw_ref, store_imag)
    for e, w in enumerate(plan["w_extras"]):
        hx_ref[0, e] = extra_a.c[e] * w
        hx_ref[1, e] = -extra_d.s[e] * w


def _block_diag2(w):
    z = jnp.zeros_like(w)
    return jnp.concatenate([jnp.concatenate([w, z], axis=1), jnp.concatenate([z, w], axis=1)], axis=0)


def _hyena_consts(L):
    plan = _hy_plan(L)
    mats = tuple(jnp.asarray(a).astype(BF16) for a in plan["base"])
    return mats, jnp.asarray(plan["tw_table"]), jnp.asarray(plan["w_table"])


def _hyena_filter(L, w1, b1, w2, b2, w3, sin_freq, consts):
    plan = _hy_plan(L)
    m, nb, ne = plan["m"], plan["n_blocks"], plan["n_extras"]
    mats, tw, wt = consts
    z, window = _filter_features(L)
    z2 = np.concatenate([z[:L // 2], z[L // 2:]], axis=1)
    w1d = _block_diag2(jnp.pad(w1, ((0, HY_EMB_PAD - HY_EMB), (0, 0))))
    w2d = _block_diag2(w2)
    w3d = jnp.stack([_block_diag2(w3[:, o * 2 * HY_W:(o + 1) * 2 * HY_W]) for o in range(2)])
    pair = lambda v: jnp.tile(v, (1, 2))
    full = lambda *shape: _resident(shape, lambda o: (0,) * len(shape))
    spectrum = pl.BlockSpec((None, nb, m, HY_W), lambda o: (o, 0, 0, 0))
    return pl.pallas_call(
        functools.partial(_filter_kernel, L=L),
        grid=(2,),
        in_specs=[
            full(L // 2, 2 * HY_EMB_PAD), full(L, HY_W), full(2 * HY_EMB_PAD, 2 * HY_FFN), full(1, 2 * HY_FFN),
            full(2 * HY_FFN, 2 * HY_FFN), full(1, 2 * HY_FFN),
            pl.BlockSpec((None, 2 * HY_FFN, 4 * HY_W), lambda o: (o, 0, 0)),
            full(2, 2 * HY_FFN), *[full(m, m) for _ in mats], full(*tw.shape), full(*wt.shape),
        ],
        out_specs=[spectrum, spectrum, pl.BlockSpec((None, 2, ne, 1, HY_W), lambda o: (o, 0, 0, 0, 0))],
        out_shape=[jax.ShapeDtypeStruct((2, nb, m, HY_W), F32), jax.ShapeDtypeStruct((2, nb, m, HY_W), F32),
                   jax.ShapeDtypeStruct((2, 2, ne, 1, HY_W), F32)],
        scratch_shapes=[pltpu.VMEM((L // 2, 2 * HY_FFN), F32), pltpu.VMEM((HY_W // 128, L, 128), F32),
                        pltpu.VMEM((HY_W // 128, L, 128), F32), pltpu.VMEM((nb, m, HY_W), BF16)],
        compiler_params=_cparams("arbitrary"),
        name="hyena_filter",
    )(jnp.asarray(z2), jnp.asarray(window), w1d, pair(b1[None, :]), w2d, pair(b2[None, :]), w3d, pair(sin_freq),
      *mats, tw, wt)


def _hyena_kernel(v_ref, x1_ref, x2_ref, g_ref, cw_ref, cb_ref, skip_ref, *refs, L):
    plan = _hy_plan(L)
    depth = plan["depth"]
    n_mats = len(plan["base"])
    mats = refs[:n_mats]
    tw_ref, hr_ref, hi_ref, hx_ref, o_ref, a_scr, x_scr, y_scr, xbf_scr, gc_scr, gs_scr = refs[n_mats:]
    ck = min(HY_ROWS, L)
    halo = 16
    chunks = [(t0, t0 + ck) for t0 in range(0, L, ck)]

    def split_store(scr, t0, t1, val):
        for h, part in enumerate(_lane_halves(val)):
            scr[h, t0:t1, :] = part

    def joined(scr, t0, t1):
        return jnp.concatenate([scr[h, t0:t1, :] for h in range(HY_W // 128)], axis=1)

    def short_conv(ref, j, t0, t1):
        lo, hi = max(t0 - halo, 0), min(t1 + halo, L)
        u = ref[lo:hi, :].astype(F32)
        row = lax.broadcasted_iota(jnp.int32, (hi - lo, HY_W), 0)
        prev = pltpu.roll(u, 1, 0)
        nxt = pltpu.roll(u, hi - lo - 1, 0)
        if lo == 0:
            prev = jnp.where(row == 0, 0.0, prev)
        if hi == L:
            nxt = jnp.where(row == hi - lo - 1, 0.0, nxt)
        w = cw_ref[:, j * HY_W:(j + 1) * HY_W]
        z = prev * w[0:1, :] + u * w[1:2, :] + nxt * w[2:3, :] + cb_ref[:, j * HY_W:(j + 1) * HY_W]
        return z[t0 - lo:t1 - lo, :]

    def long_conv(order):
        def product(spec, hr, hi):
            return _Spec([c * r + s * i for c, s, r, i in zip(spec.c, spec.s, hr, hi)],
                         [s * r - c * i for c, s, r, i in zip(spec.c, spec.s, hr, hi)])

        def consume(k0, k1, spec):
            nb = len(spec.c)
            z = product(spec, [hr_ref[order, b, k0:k1, :] for b in range(nb)],
                        [hi_ref[order, b, k0:k1, :] for b in range(nb)])
            for i, leaf in enumerate(_backward(depth, z, _block_twiddles(plan, tw_ref, k0, k1))):
                gc_scr[i, k0:k1, :] = leaf.c[0].astype(BF16)
                gs_scr[i, k0:k1, :] = leaf.s[0].astype(BF16)

        extras = _spectrum_of(plan, a_scr, xbf_scr, mats, tw_ref, consume)
        ne = len(extras.c)
        zx = product(extras, [hx_ref[order, 0, e] for e in range(ne)], [hx_ref[order, 1, e] for e in range(ne)])
        tops = [leaf.c[0] for leaf in _backward(depth, zx, plan["tw_extras"])]
        _sequence_of(plan, gc_scr, gs_scr, tops, mats, y_scr)

    for t0, t1 in chunks:
        split_store(a_scr, t0, t1, short_conv(v_ref, 0, t0, t1))
        x_scr[t0:t1, :] = short_conv(x1_ref, 1, t0, t1)
    long_conv(0)
    for t0, t1 in chunks:
        conv = joined(y_scr, t0, t1) + joined(a_scr, t0, t1) * skip_ref[0:1, :]
        split_store(a_scr, t0, t1, x_scr[t0:t1, :] * conv)
    long_conv(1)
    for t0, t1 in chunks:
        conv = joined(y_scr, t0, t1) + joined(a_scr, t0, t1) * skip_ref[1:2, :]
        y = short_conv(x2_ref, 2, t0, t1) * conv
        o_ref[t0:t1, :] = (y * _silu(g_ref[t0:t1, :].astype(F32))).astype(o_ref.dtype)


def _hyena(u, conv_w, conv_b, skip, consts, hr, hi, hx, B, L):
    plan = _hy_plan(L)
    m, nb, ne = plan["m"], plan["n_blocks"], plan["n_extras"]
    mats, tw, _ = consts
    col = lambda j: pl.BlockSpec((L, CB), lambda b: (b, j))
    const = lambda *shape: _resident(shape, lambda b: (0,) * len(shape))
    halves = HY_W // 128
    return pl.pallas_call(
        functools.partial(_hyena_kernel, L=L),
        grid=(B,),
        in_specs=[
            col(CB_HY_V), col(CB_HY_X1), col(CB_HY_X2), col(CB_HY_G),
            const(3, 3 * HY_W), const(1, 3 * HY_W), const(2, HY_W),
            *[const(m, m) for _ in mats], const(*tw.shape),
            const(2, nb, m, HY_W), const(2, nb, m, HY_W), const(2, 2, ne, 1, HY_W),
        ],
        out_specs=pl.BlockSpec((L, HY_W), lambda b: (b, 0)),
        out_shape=jax.ShapeDtypeStruct((B * L, HY_W), BF16),
        scratch_shapes=[
            pltpu.VMEM((halves, L, 128), F32), pltpu.VMEM((L, HY_W), F32), pltpu.VMEM((halves, L, 128), F32),
            pltpu.VMEM((nb, m, HY_W), BF16), pltpu.VMEM((nb, m, HY_W), BF16), pltpu.VMEM((nb, m, HY_W), BF16),
        ],
        compiler_params=_cparams("parallel"),
        name="hyena",
    )(u, u, u, u, conv_w, conv_b[None, :], skip, *mats, tw, hr, hi, hx)


def _ret_kernel(*refs, L, Lc, has_init):
    if has_init:
        (q_ref, k_ref, v_ref, g_ref, kc_ref, vc_ref, cos_ref, sin_ref, rate_ref, rrow_ref, ones_ref,
         o_ref, q_scr, k_scr, sf_scr, sb_scr, r_scr) = refs
    else:
        (q_ref, k_ref, v_ref, g_ref, cos_ref, sin_ref, rate_ref, rrow_ref, ones_ref,
         o_ref, q_scr, k_scr, sf_scr, sb_scr, r_scr) = refs
    C = min(RET_CHUNK, L)
    nch = L // C
    W = RET_W
    quarter = RET_HEAD_DIM // 4
    block = ones_ref[...]
    lg = -jnp.exp(rate_ref[...])
    lg_f, lg_b = lg[0:1, :], lg[1:2, :]

    lane = lax.broadcasted_iota(jnp.int32, (L, W), 1)
    first_quarter = (lane % (2 * quarter)) < quarter

    def rope(a):
        swapped = jnp.where(first_quarter, pltpu.roll(a, W - quarter, 1), pltpu.roll(a, quarter, 1))
        return a * cos_ref[...] + swapped * sin_ref[...]

    if has_init:
        q_scr[...] = rope(q_ref[...].astype(F32)).astype(BF16)
        k_scr[...] = rope(k_ref[...].astype(F32) * (RET_HEAD_DIM ** -0.5)).astype(BF16)
    else:
        q_scr[...] = q_ref[...]
        k_scr[...] = (k_ref[...].astype(F32) * (RET_HEAD_DIM ** -0.5)).astype(BF16)

    def decays(n_rows):
        pos = lax.broadcasted_iota(jnp.int32, (n_rows, W), 0).astype(F32)
        return (jnp.exp(lg_f * (n_rows - 1.0 - pos)),
                jnp.exp(lg_b * pos))

    def chunk_states(k_bf, v_bf, zf, zb):
        kf = (k_bf.astype(F32) * zf).astype(BF16)
        kb = (k_bf.astype(F32) * zb).astype(BF16)
        return _dot_tn(kf, v_bf) * block, _dot_tn(kb, v_bf) * block

    zeta_f, zeta_b = decays(C)
    if has_init:
        zc_f, zc_b = decays(Lc)
        kc = (kc_ref[...].astype(F32) * (RET_HEAD_DIM ** -0.5)).astype(BF16)
        s0_f, s0_b = chunk_states(kc, vc_ref[...], zc_f, zc_b)
    else:
        s0_f = jnp.zeros((W, W), F32)
        s0_b = jnp.zeros((W, W), F32)

    for n in range(nch):
        kv_f, kv_b = chunk_states(k_scr[n * C:(n + 1) * C, :], v_ref[n * C:(n + 1) * C, :], zeta_f, zeta_b)
        sf_scr[n] = kv_f
        sb_scr[n] = kv_b
    dec_f = jnp.exp(lg_f * float(C))
    dec_b = jnp.exp(lg_b * float(C))
    state = s0_f
    for n in range(nch):
        kv = sf_scr[n]
        sf_scr[n] = state
        state = dec_f * state + kv
    state = s0_b
    for n in range(nch - 1, -1, -1):
        kv = sb_scr[n]
        sb_scr[n] = state
        state = dec_b * state + kv

    posc = lax.broadcasted_iota(jnp.int32, (C, W), 0).astype(F32)
    xi_f = jnp.exp(lg_f * (posc + 1.0))
    xi_b = jnp.exp(lg_b * (float(C) - posc))
    diff = (lax.broadcasted_iota(jnp.int32, (C, C), 0) - lax.broadcasted_iota(jnp.int32, (C, C), 1)).astype(F32)
    lane_c = lax.broadcasted_iota(jnp.int32, (1, W), 1)
    dmask = []
    for h in range(RET_HEADS):
        rf = -jnp.exp(rrow_ref[h:h + 1, 0:C])
        rb = -jnp.exp(rrow_ref[RET_HEADS + h:RET_HEADS + h + 1, 0:C])
        dmask.append(jnp.where(diff >= 0, jnp.exp(rf * jnp.maximum(diff, 0.0)), 0.0)
                     + jnp.where(diff <= 0, jnp.exp(rb * jnp.maximum(-diff, 0.0)), 0.0))

    for n in range(nch):
        qn = q_scr[n * C:(n + 1) * C, :]
        kn = k_scr[n * C:(n + 1) * C, :]
        vn = v_ref[n * C:(n + 1) * C, :]
        qf = qn.astype(F32)
        lhs = [(qf * xi_f).astype(BF16), (qf * xi_b).astype(BF16)]
        rhs = [sf_scr[n].astype(BF16), sb_scr[n].astype(BF16)]
        zero = jnp.zeros_like(qn)
        for h in range(RET_HEADS):
            sel = (lane_c // RET_HEAD_DIM) == h
            s = _dot_nt(jnp.where(sel, qn, zero), kn) * dmask[h]
            lhs.append(s.astype(BF16))
            rhs.append(jnp.where(sel, vn, zero))
        r_scr[n * C:(n + 1) * C, :] = _dot(jnp.concatenate(lhs, axis=1), jnp.concatenate(rhs, axis=0))

    r = r_scr[...]
    ss = _dot((r * r).astype(BF16), block.astype(BF16))
    rn = r * lax.rsqrt(ss * (1.0 / RET_HEAD_DIM) + EPS)
    o_ref[...] = (rn * _silu(g_ref[...].astype(F32))).astype(o_ref.dtype)


def _retention(u, uc, ret_log_rate, B, L, Lc, has_init, ctx_kv=(CB_RE_K, CB_RE_V)):
    C = min(RET_CHUNK, L)
    nch = L // C
    col = lambda j: pl.BlockSpec((L, CB), lambda b: (b, j))
    ccol = lambda j: pl.BlockSpec((Lc, CB), lambda b: (b, j))
    cos, sin = _rope_tables(L)
    rate_lane = jnp.repeat(ret_log_rate, RET_HEAD_DIM, axis=1)
    rate_row = jnp.broadcast_to(ret_log_rate.reshape(2 * RET_HEADS, 1), (2 * RET_HEADS, RET_W))
    ones = jnp.asarray(_head_block_ones(RET_W))
    in_specs = [col(CB_RE_Q), col(CB_RE_K), col(CB_RE_V), col(CB_RE_G)]
    args = [u, u, u, u]
    if has_init:
        in_specs += [ccol(ctx_kv[0]), ccol(ctx_kv[1])]
        args += [uc, uc]
    in_specs += [
        _resident((L, RET_W), lambda b: (0, 0)),
        _resident((L, RET_W), lambda b: (0, 0)),
        _resident((2, RET_W), lambda b: (0, 0)),
        _resident((2 * RET_HEADS, RET_W), lambda b: (0, 0)),
        _resident((RET_W, RET_W), lambda b: (0, 0)),
    ]
    args += [jnp.asarray(cos), jnp.asarray(sin), rate_lane, rate_row, ones]
    return pl.pallas_call(
        functools.partial(_ret_kernel, L=L, Lc=Lc, has_init=has_init),
        grid=(B,),
        in_specs=in_specs,
        out_specs=pl.BlockSpec((L, RET_W), lambda b: (b, 0)),
        out_shape=jax.ShapeDtypeStruct((B * L, RET_W), BF16),
        scratch_shapes=[
            pltpu.VMEM((L, RET_W), BF16), pltpu.VMEM((L, RET_W), BF16),
            pltpu.VMEM((nch, RET_W, RET_W), F32), pltpu.VMEM((nch, RET_W, RET_W), F32),
            pltpu.VMEM((L, RET_W), F32),
        ],
        compiler_params=_cparams("parallel"),
        name="retention",
    )(*args)


def kernel(x, c, ctx, c_ctx, norm_w, ada_w, ada_b, w_in, w_out, na_q_gain, na_k_gain, na_rpb, hy_conv_w, hy_conv_b,
           hy_w1, hy_b1, hy_w2, hy_b2, hy_w3, hy_sin_freq, hy_skip, ret_log_rate):
    B, L, D = x.shape
    Lc = ctx.shape[1]
    assert D == D_MODEL and L % (GRID_W * NA_G * NA_GPS) == 0 and L // GRID_W >= NA_WIN
    assert NA_G == NA_KT // GRID_W and (L // GRID_W - NA_WIN) % NA_G == 0 and Lc == NA_KT

    n_cond = 16
    cc = jnp.concatenate([c, c_ctx[None, :], jnp.zeros((n_cond - B - 1, D), F32)], axis=0)
    mods = _modulation(cc, ada_w, ada_b)

    dft_x = _hyena_consts(L)
    dft_c = _hyena_consts(Lc)

    tm_x = 1024
    xf = x.reshape(B * L, D)
    cf = ctx.reshape(B * Lc, D)
    x_mod = lambda i: i // (L // tm_x)
    c_mod = lambda i: B

    w_in_bf_all = w_in.astype(BF16)
    w_out_bf_all = w_out.astype(BF16)

    def layer_params(i):
        return mods[i].reshape(n_cond, 1, 3 * D), norm_w[i][None, :], (w_in_bf_all, i)

    mod, nw, w_in_bf = layer_params(0)
    u = _projection(xf, x_mod, tm_x, in_args=(mod, nw, w_in_bf))
    uc = _projection(cf, c_mod, Lc, in_args=(mod, nw, w_in_bf))
    ctx_na, ctx_re = (1, 2), (CB_RE_K, CB_RE_V)
    for i in range(DEPTH):
        w_out_bf = (w_out_bf_all, i)
        filt = (hy_w1[i], hy_b1[i], hy_w2[i], hy_b2[i], hy_w3[i], hy_sin_freq[i])
        last = i == DEPTH - 1
        nxt = None if last else layer_params(i + 1)

        za = _neighborhood_attention(u, uc, na_q_gain[i], na_k_gain[i], na_rpb[i], B, L, Lc, ctx_na)
        hr, hi, hm = _hyena_filter(L, *filt, dft_x)
        zy = _hyena(u, hy_conv_w[i], hy_conv_b[i], hy_skip[i], dft_x, hr, hi, hm, B, L)
        zr = _retention(u, uc, ret_log_rate[i], B, L, Lc, True, ctx_re)
        if not last:
            zac = _context_attention(uc, na_q_gain[i], na_k_gain[i], B, Lc)
            hrc, hic, hmc = _hyena_filter(Lc, *filt, dft_c)
            zyc = _hyena(uc, hy_conv_w[i], hy_conv_b[i], hy_skip[i], dft_c, hrc, hic, hmc, B, Lc)
            zrc = _retention(uc, uc, ret_log_rate[i], B, Lc, Lc, False)
            only_kv = i + 1 == DEPTH - 1
            cf, uc = _projection(cf, c_mod, Lc, out_args=(zac, zyc, zrc, mod, w_out_bf), in_args=nxt,
                                 in_cols=CTX_KV_COLS if only_kv else None)
            if only_kv:
                ctx_na, ctx_re = CTX_KV_NA, CTX_KV_RE
            xf, u = _projection(xf, x_mod, tm_x, out_args=(za, zy, zr, mod, w_out_bf), in_args=nxt)
            mod = nxt[0]
        else:
            tm_last = 2 * tm_x
            xf = _projection(xf, lambda i: i // (L // tm_last), tm_last, out_args=(za, zy, zr, mod, w_out_bf))

    return xf.reshape(B, L, D)
```
